```python
import jax, jax.numpy as jnp
from jax import lax
import numpy as np

D_MODEL = 2048
BATCH = 4
SEQ = 4096
DEPTH = 4

CTX_LEN = 256
GRID_W = 64
EPS = 1e-6
F_MIN = 1e-30
N_MOD = 6

HEAD_DIM = 128
ATTN_HEADS = 8
ATTN_KV_HEADS = 2
ATTN_GROUP = ATTN_HEADS // ATTN_KV_HEADS
ATTN_Q_BLOCK = 128
ROPE_THETA = 10000.0
ATTN_Q_W = ATTN_HEADS * HEAD_DIM
ATTN_KV_W = ATTN_KV_HEADS * HEAD_DIM

HG_HEADS = 4
HG_DK = 128
HG_DV = 128
HG_CHUNK = 16
HG_K_W = HG_HEADS * HG_DK
HG_V_W = HG_HEADS * HG_DV

SG_GROUPS = 4
SG_DIM = 128
SG_CHUNK = 128
SG_W = SG_GROUPS * SG_DIM

D_MIX = ATTN_Q_W + HG_V_W + SG_W
IN_SIZES = (ATTN_Q_W, ATTN_KV_W, ATTN_KV_W, HG_K_W, HG_K_W, HG_K_W, HG_V_W, HG_V_W, SG_W, SG_W)
IN_COLS = 5120

D_FF = 5632
CONV_W = 3

kernel_name = "hybrid_parallel_mixer_dit_block"


def rms_norm(x, g):
    xf = x.astype(jnp.float32)
    y = xf * lax.rsqrt(jnp.mean(xf * xf, axis=-1, keepdims=True) + EPS)
    return (y * g.astype(jnp.float32)).astype(x.dtype)


def axial_rope_tables(n_tokens):
    rows = n_tokens // GRID_W
    row = jnp.repeat(jnp.arange(rows, dtype=jnp.float32), GRID_W)
    col = jnp.tile(jnp.arange(GRID_W, dtype=jnp.float32), rows)
    n_freq = HEAD_DIM // 4
    inv = ROPE_THETA ** (-jnp.arange(n_freq, dtype=jnp.float32) / n_freq)
    ang = jnp.concatenate([row[:, None] * inv, col[:, None] * inv], axis=-1)
    return jnp.cos(ang), jnp.sin(ang)


def apply_rope(x, cos, sin):
    xf = x.astype(jnp.float32).reshape(*x.shape[:-1], HEAD_DIM // 2, 2)
    x1, x2 = xf[..., 0], xf[..., 1]
    cs, sn = cos[None, :, None, :], sin[None, :, None, :]
    out = jnp.stack([x1 * cs - x2 * sn, x1 * sn + x2 * cs], axis=-1)
    return out.reshape(x.shape).astype(x.dtype)


def gqa_softmax(q, k, v):
    s = jnp.einsum('bqkgd,bskd->bkgqs', q, k).astype(jnp.float32) * (HEAD_DIM ** -0.5)
    p = jax.nn.softmax(s, axis=-1).astype(v.dtype)
    return jnp.einsum('bkgqs,bskd->bqkgd', p, v)


def latent_attention(q, k_all, v_all):
    B, T, H, Dh = q.shape
    nb = T // ATTN_Q_BLOCK
    qb = q.reshape(B, nb, ATTN_Q_BLOCK, ATTN_KV_HEADS, ATTN_GROUP, Dh).transpose(1, 0, 2, 3, 4, 5)
    o = lax.map(lambda qblk: gqa_softmax(qblk, k_all, v_all), qb)
    return o.transpose(1, 0, 2, 3, 4, 5).reshape(B, T, H * Dh)


def context_attention(q, k, v):
    B, L, H, Dh = q.shape
    o = gqa_softmax(q.reshape(B, L, ATTN_KV_HEADS, ATTN_GROUP, Dh), k, v)
    return o.reshape(B, L, H * Dh)


def hgrn_lower_bounds(lb_param):
    p = jax.nn.softmax(lb_param.astype(jnp.float32), axis=1)
    return jnp.cumsum(p, axis=1) - p[:, :1]


def hgrn2_gates(f_raw, lb):
    z = f_raw.astype(jnp.float32)
    lb = lb.reshape(HG_HEADS, HG_DK)
    f = lb + (1.0 - lb) * jax.nn.sigmoid(z)
    log_f = jnp.log(jnp.maximum(f, F_MIN))
    k = (1.0 - lb) * jax.nn.sigmoid(-z)
    return log_f, k


def gla_chunkwise(q, k, v, log_f, s0):
    B, T, H, K = q.shape
    V = v.shape[-1]
    C = HG_CHUNK
    N = T // C
    q, k, log_f = [a.reshape(B, N, C, H, K) for a in (q, k, log_f)]
    v = v.reshape(B, N, C, H, V)
    b = jnp.cumsum(log_f, axis=2)
    tri = jnp.tril(jnp.ones((C, C), dtype=bool))[None, None, :, :, None, None]
    diff = b[:, :, :, None] - b[:, :, None, :]
    decay = jnp.where(tri, jnp.exp(jnp.where(tri, diff, 0.0)), 0.0)
    scores = jnp.einsum('bnthk,bnshk,bntshk->bnhts', q, k, decay)
    o_intra = jnp.einsum('bnhts,bnshv->bnthv', scores, v)
    b_last = b[:, :, -1]
    q_dec = q * jnp.exp(b)
    k_dec = k * jnp.exp(b_last[:, :, None] - b)

    def step(S, xs):
        qd, kd, vv, dl = xs
        o = jnp.einsum('bthk,bhkv->bthv', qd, S)
        S = S * dl[..., None] + jnp.einsum('bthk,bthv->bhkv', kd, vv)
        return S, o

    xs = tuple(jnp.moveaxis(a, 1, 0) for a in (q_dec, k_dec, v, jnp.exp(b_last)))
    s_fin, o_inter = lax.scan(step, s0, xs)
    o = o_intra + jnp.moveaxis(o_inter, 0, 1)
    return o.reshape(B, T, H, V), s_fin


def hgrn2_direction(q_l, i_l, f_l, q_c, i_c, f_c, lb):
    B = q_l.shape[0]
    s0 = jnp.zeros((B, HG_HEADS, HG_DK, HG_DV), jnp.float32)
    logf_c, k_c = hgrn2_gates(f_c, lb)
    o_c, s_ctx = gla_chunkwise(q_c, k_c, i_c, logf_c, s0)
    logf_l, k_l = hgrn2_gates(f_l, lb)
    o_l, _ = gla_chunkwise(q_l, k_l, i_l, logf_l, s_ctx)
    return o_l, o_c


def spatial_gating(u, v, norm_g, w_s, b_s):
    B, T, _ = u.shape
    N = T // SG_CHUNK
    u = jax.nn.gelu(u)
    v = rms_norm(jax.nn.gelu(v).reshape(B, T, SG_GROUPS, SG_DIM), norm_g.reshape(SG_GROUPS, SG_DIM))
    v = v.reshape(B, N, SG_CHUNK, SG_GROUPS, SG_DIM)
    mixed = jnp.einsum('gts,bnsgd->bntgd', w_s, v) + b_s.T[:, :, None]
    return u * mixed.reshape(B, T, SG_W)


def conv_ffn(h, w_up, conv_w, conv_b, w_down):
    T = h.shape[1]
    up = h @ w_up
    pad = jnp.pad(up, ((0, 0), (CONV_W // 2, CONV_W // 2), (0, 0)))
    y = conv_b + sum(pad[:, j:j + T] * conv_w[j] for j in range(CONV_W))
    gate, val = jnp.split(y, 2, axis=-1)
    return (jax.nn.silu(gate) * val) @ w_down


def token_mixers(h, hc, w_in, q_g, k_g, lb_f, lb_b, hg_g, sg_g, sg_w, sg_b, cos, sin, need_ctx):
    B, T, _ = h.shape
    L = hc.shape[1]
    splits = np.cumsum(IN_SIZES)[:-1].tolist()
    (aq, ak, av, hq, hff, hfb, hi, hgt, su, sv) = jnp.split(h @ w_in, splits, axis=-1)
    (aqc, akc, avc, hqc, hffc, hfbc, hic, hgtc, suc, svc) = jnp.split(hc @ w_in, splits, axis=-1)

    q = apply_rope(rms_norm(aq.reshape(B, T, ATTN_HEADS, HEAD_DIM), q_g), cos, sin)
    k = apply_rope(rms_norm(ak.reshape(B, T, ATTN_KV_HEADS, HEAD_DIM), k_g), cos, sin)
    v = av.reshape(B, T, ATTN_KV_HEADS, HEAD_DIM)
    kc = rms_norm(akc.reshape(B, L, ATTN_KV_HEADS, HEAD_DIM), k_g)
    vc = avc.reshape(B, L, ATTN_KV_HEADS, HEAD_DIM)
    attn = latent_attention(q, jnp.concatenate([k, kc], axis=1), jnp.concatenate([v, vc], axis=1))

    def heads(a, d):
        return a.astype(jnp.float32).reshape(a.shape[0], a.shape[1], -1, d)
    flip = lambda a: jnp.flip(a, axis=1)
    q_l, i_l = jax.nn.silu(heads(hq, HG_DK)), heads(hi, HG_DV)
    q_c, i_c = jax.nn.silu(heads(hqc, HG_DK)), heads(hic, HG_DV)
    ff_l, fb_l, ff_c, fb_c = heads(hff, HG_DK), heads(hfb, HG_DK), heads(hffc, HG_DK), heads(hfbc, HG_DK)
    o_f, oc_f = hgrn2_direction(q_l, i_l, ff_l, q_c, i_c, ff_c, lb_f)
    o_b, oc_b = hgrn2_direction(flip(q_l), flip(i_l), flip(fb_l), flip(q_c), flip(i_c), flip(fb_c), lb_b)
    hg_out = (rms_norm(o_f + flip(o_b), hg_g).reshape(B, T, HG_V_W)
              * jax.nn.silu(hgt.astype(jnp.float32))).astype(h.dtype)

    sg = spatial_gating(su, sv, sg_g, sg_w, sg_b)

    mix = jnp.concatenate([attn, hg_out, sg], axis=-1)
    if not need_ctx:
        return mix, None
    qc = rms_norm(aqc.reshape(B, L, ATTN_HEADS, HEAD_DIM), q_g)
    attn_c = context_attention(qc, kc, vc)
    hg_c = (rms_norm(oc_f + flip(oc_b), hg_g).reshape(B, L, HG_V_W)
            * jax.nn.silu(hgtc.astype(jnp.float32))).astype(hc.dtype)
    sg_c = spatial_gating(suc, svc, sg_g, sg_w, sg_b)
    mix_c = jnp.concatenate([attn_c, hg_c, sg_c], axis=-1)
    return mix, mix_c


def setup_inputs(seed: int = 0) -> dict:
    key = jax.random.key(seed)
    ks = jax.random.split(key, 24)
    nrm = lambda k, shape, s: jax.random.normal(k, shape, jnp.float32) * s
    L = DEPTH
    return {
        "x": nrm(ks[0], (BATCH, SEQ, D_MODEL), 1.0),
        "c": nrm(ks[1], (BATCH, D_MODEL), 1.0),
        "ctx": nrm(ks[2], (BATCH, CTX_LEN, D_MODEL), 1.0),
        "c_ctx": nrm(ks[3], (D_MODEL,), 1.0),
        "w_ada": nrm(ks[4], (L, D_MODEL, N_MOD * D_MODEL), 0.5 * D_MODEL ** -0.5),
        "b_ada": nrm(ks[5], (L, N_MOD * D_MODEL), 0.02),
        "norm1_g": 1.0 + nrm(ks[6], (L, D_MODEL), 0.02),
        "w_in": nrm(ks[7], (L, D_MODEL, IN_COLS), D_MODEL ** -0.5),
        "q_norm_g": 1.0 + nrm(ks[8], (L, HEAD_DIM), 0.02),
        "k_norm_g": 1.0 + nrm(ks[9], (L, HEAD_DIM), 0.02),
        "hg_lower_bounds": nrm(ks[10], (2, L, HG_K_W), 0.1),
        "hg_norm_g": 1.0 + nrm(ks[11], (L, HG_DV), 0.02),
        "sg_norm_g": 1.0 + nrm(ks[12], (L, SG_W), 0.02),
        "sg_w": nrm(ks[13], (L, SG_GROUPS, SG_CHUNK, SG_CHUNK), SG_CHUNK ** -0.5),
        "sg_b": 1.0 + nrm(ks[14], (L, SG_GROUPS, SG_CHUNK), 0.02),
        "w_out": nrm(ks[15], (L, D_MIX, D_MODEL), D_MIX ** -0.5),
        "norm2_g": 1.0 + nrm(ks[16], (L, D_MODEL), 0.02),
        "w_up": nrm(ks[17], (L, D_MODEL, 2 * D_FF), D_MODEL ** -0.5),
        "conv_w": nrm(ks[18], (L, CONV_W, 2 * D_FF), CONV_W ** -0.5),
        "conv_b": nrm(ks[19], (L, 2 * D_FF), 0.02),
        "w_down": nrm(ks[20], (L, D_FF, D_MODEL), D_FF ** -0.5),
        "final_norm_g": 1.0 + nrm(ks[21], (D_MODEL,), 0.02),
    }


def reference(x, c, ctx, c_ctx, w_ada, b_ada, norm1_g, w_in, q_norm_g, k_norm_g,
              hg_lower_bounds, hg_norm_g, sg_norm_g, sg_w, sg_b, w_out,
              norm2_g, w_up, conv_w, conv_b, w_down, final_norm_g):
    B, T, _ = x.shape
    cos, sin = axial_rope_tables(T)
    lbs = hgrn_lower_bounds(hg_lower_bounds)
    silu_c = jax.nn.silu(c)
    silu_cc = jax.nn.silu(c_ctx)
    cx = ctx
    for l in range(DEPTH):
        need_ctx = l < DEPTH - 1
        mod = (silu_c @ w_ada[l] + b_ada[l]).reshape(B, N_MOD, 1, D_MODEL)
        mod_c = (silu_cc @ w_ada[l] + b_ada[l]).reshape(N_MOD, D_MODEL)

        h = rms_norm(x, norm1_g[l]) * (1.0 + mod[:, 1]) + mod[:, 0]
        hc = rms_norm(cx, norm1_g[l]) * (1.0 + mod_c[1]) + mod_c[0]
        mix, mix_c = token_mixers(h, hc, w_in[l], q_norm_g[l], k_norm_g[l], lbs[0, l], lbs[1, l],
                                  hg_norm_g[l], sg_norm_g[l], sg_w[l], sg_b[l], cos, sin, need_ctx)
        x = x + mod[:, 2] * (mix @ w_out[l])

        h2 = rms_norm(x, norm2_g[l]) * (1.0 + mod[:, 4]) + mod[:, 3]
        x = x + mod[:, 5] * conv_ffn(h2, w_up[l], conv_w[l], conv_b[l], w_down[l])

        if need_ctx:
            cx = cx + mod_c[2] * (mix_c @ w_out[l])
            hc2 = rms_norm(cx, norm2_g[l]) * (1.0 + mod_c[4]) + mod_c[3]
            cx = cx + mod_c[5] * conv_ffn(hc2, w_up[l], conv_w[l], conv_b[l], w_down[l])
    return rms_norm(x, final_norm_g)
```

```python
import functools

import jax
import jax.numpy as jnp
from jax import lax
from jax.experimental import pallas as pl
from jax.experimental.pallas import tpu as pltpu

F32 = jnp.float32
BF16 = jnp.bfloat16

EPS = 1e-6
F_MIN = 1e-30
N_MOD = 6
HEAD_DIM = 128
ATTN_HEADS = 8
ATTN_KV_HEADS = 2
ATTN_GROUP = ATTN_HEADS // ATTN_KV_HEADS
ROPE_THETA = 10000.0
GRID_W = 64
HG_HEADS = 4
HG_D = 128
SG_GROUPS = 4
SG_DIM = 128
SG_CHUNK = 128
CONV_W = 3
LANE = 128
HG_CHUNK = 128
VMEM_LIMIT = 56 * 1024 * 1024


def _cparams(sem):
    return pltpu.CompilerParams(dimension_semantics=sem, vmem_limit_bytes=VMEM_LIMIT)


def _dot(a, b):
    return jnp.dot(a, b, preferred_element_type=F32)


def _dot_nt(a, b):
    return lax.dot_general(a, b, (((1,), (1,)), ((), ())), preferred_element_type=F32)


def _rms(x, g):
    return x * lax.rsqrt(jnp.mean(x * x, axis=-1, keepdims=True) + EPS) * g


def _silu(x):
    return x * jax.nn.sigmoid(x)


def _ada_kernel(c_ref, w_ref, b_ref, o_ref):
    s = _silu(c_ref[...]).astype(BF16)
    o_ref[...] = _dot(s, w_ref[...].astype(BF16)) + b_ref[...]


def ada_table(cin, w_ada, b_ada, tn=1024):
    depth, d, n = w_ada.shape
    assert n % tn == 0
    return pl.pallas_call(
        _ada_kernel,
        out_shape=jax.ShapeDtypeStruct((depth, 8, n), F32),
        grid=(depth, n // tn),
        in_specs=[
            pl.BlockSpec((8, d), lambda l, j: (0, 0)),
            pl.BlockSpec((None, d, tn), lambda l, j: (l, 0, j)),
            pl.BlockSpec((None, 1, tn), lambda l, j: (l, 0, j)),
        ],
        out_specs=pl.BlockSpec((None, 8, tn), lambda l, j: (l, 0, j)),
        compiler_params=_cparams(("parallel", "parallel")),
        name="ada_table",
    )(cin, w_ada, b_ada.reshape(depth, 1, n))


def _in_kernel(x_ref, g_ref, sh_ref, sc_ref, w_ref, o_ref, h_ref, *, rows):
    @pl.when(pl.program_id(1) == 0)
    def _():
        for r in range(0, x_ref.shape[0], rows):
            y = _rms(x_ref[r:r + rows, :], g_ref[...])
            h_ref[r:r + rows, :] = (y * (1.0 + sc_ref[0]) + sh_ref[0]).astype(BF16)

    o_ref[...] = _dot(h_ref[...], w_ref[...])


def in_proj(x, g, mod, w, *, tile_mod, tm, tn=512):
    n, d = x.shape
    cols = w.shape[1]
    assert n % tm == 0 and cols % tn == 0
    return pl.pallas_call(
        functools.partial(_in_kernel, rows=min(tm, 256)),
        out_shape=jax.ShapeDtypeStruct((n, cols), F32),
        grid=(n // tm, cols // tn),
        in_specs=[
            pl.BlockSpec((tm, d), lambda i, j: (i, 0)),
            pl.BlockSpec((1, d), lambda i, j: (0, 0)),
            pl.BlockSpec((1, 1, d), lambda i, j: (tile_mod(i) * N_MOD + 0, 0, 0)),
            pl.BlockSpec((1, 1, d), lambda i, j: (tile_mod(i) * N_MOD + 1, 0, 0)),
            pl.BlockSpec((d, tn), lambda i, j: (0, j)),
        ],
        out_specs=pl.BlockSpec((tm, tn), lambda i, j: (i, j)),
        scratch_shapes=[pltpu.VMEM((tm, d), BF16)],
        compiler_params=_cparams(("parallel", "arbitrary")),
        name="in_proj",
    )(x, g.reshape(1, d), mod, mod, w)


def _rope(y, cos2, sin2):
    lane = lax.broadcasted_iota(jnp.int32, y.shape, 1)
    swapped = jnp.where((lane & 1) == 0, pltpu.roll(y, LANE - 1, 1), pltpu.roll(y, 1, 1))
    return y * cos2 + swapped * sin2


def _qk_kernel(q_ref, kv_ref, cos_ref, sin_ref, qg_ref, kg_ref, qo_ref, ko_ref, vo_ref):
    cos2, sin2 = cos_ref[...], sin_ref[...]
    scale = HEAD_DIM ** -0.5
    for h in range(ATTN_HEADS):
        sl = slice(h * HEAD_DIM, (h + 1) * HEAD_DIM)
        y = _rms(q_ref[:, sl], qg_ref[...])
        qo_ref[:, sl] = (_rope(y, cos2, sin2) * scale).astype(BF16)
    for h in range(ATTN_KV_HEADS):
        sl = slice(h * HEAD_DIM, (h + 1) * HEAD_DIM)
        y = _rms(kv_ref[:, sl], kg_ref[...])
        ko_ref[:, sl] = _rope(y, cos2, sin2).astype(BF16)
    kvw = ATTN_KV_HEADS * HEAD_DIM
    vo_ref[...] = kv_ref[:, kvw:2 * kvw].astype(BF16)


def qk_prep(p, cos2, sin2, qg, kg, *, tile_rope, tm):
    n = p.shape[0]
    qw = ATTN_HEADS * HEAD_DIM
    kvw = ATTN_KV_HEADS * HEAD_DIM
    return pl.pallas_call(
        _qk_kernel,
        out_shape=(jax.ShapeDtypeStruct((n, qw), BF16),
                   jax.ShapeDtypeStruct((n, kvw), BF16),
                   jax.ShapeDtypeStruct((n, kvw), BF16)),
        grid=(n // tm,),
        in_specs=[
            pl.BlockSpec((tm, qw), lambda i: (i, 0)),
            pl.BlockSpec((tm, 2 * kvw), lambda i: (i, qw // (2 * kvw))),
            pl.BlockSpec((tm, HEAD_DIM), lambda i: (tile_rope(i), 0)),
            pl.BlockSpec((tm, HEAD_DIM), lambda i: (tile_rope(i), 0)),
            pl.BlockSpec((1, HEAD_DIM), lambda i: (0, 0)),
            pl.BlockSpec((1, HEAD_DIM), lambda i: (0, 0)),
        ],
        out_specs=(pl.BlockSpec((tm, qw), lambda i: (i, 0)),
                   pl.BlockSpec((tm, kvw), lambda i: (i, 0)),
                   pl.BlockSpec((tm, kvw), lambda i: (i, 0))),
        compiler_params=_cparams(("parallel",)),
        name="qk_prep",
    )(p, p, cos2, sin2, qg.reshape(1, HEAD_DIM), kg.reshape(1, HEAD_DIM))


def _softmax_pv(q, kvs):
    ss = [_dot_nt(q, k) for k, _ in kvs]
    m = functools.reduce(jnp.maximum, [s.max(axis=-1, keepdims=True) for s in ss])
    ps = [jnp.exp(s - m) for s in ss]
    l = functools.reduce(jnp.add, [p.sum(axis=-1, keepdims=True) for p in ps])
    o = functools.reduce(jnp.add, [_dot(p.astype(BF16), v) for p, (_, v) in zip(ps, kvs)])
    return o / l


def _attn_kernel(q_ref, kl_ref, kc_ref, vl_ref, vc_ref, o_ref, *, n_lat, heads_per_pass):
    qi = pl.program_id(2)
    tq = q_ref.shape[0]

    def run(kvs):
        for g0 in range(0, ATTN_GROUP, heads_per_pass):
            hs = range(g0, g0 + heads_per_pass)
            q = jnp.concatenate([q_ref[:, h * HEAD_DIM:(h + 1) * HEAD_DIM] for h in hs], axis=0)
            o = _softmax_pv(q, kvs)
            for n, h in enumerate(hs):
                o_ref[:, h * HEAD_DIM:(h + 1) * HEAD_DIM] = o[n * tq:(n + 1) * tq].astype(BF16)

    @pl.when(qi < n_lat)
    def _():
        run([(kl_ref[...], vl_ref[...]), (kc_ref[...], vc_ref[...])])

    @pl.when(qi >= n_lat)
    def _():
        run([(kc_ref[...], vc_ref[...])])


def attention(q, k, v, *, batch, t_lat, t_ctx, tq=256):
    n = q.shape[0]
    n_lat = t_lat // tq
    n_ctx = t_ctx // tq
    gw = ATTN_GROUP * HEAD_DIM
    ctx0 = batch * t_lat

    def qmap(b, j, i):
        return (jnp.where(i < n_lat, b * n_lat + i, ctx0 // tq + b * n_ctx + (i - n_lat)), j)

    return pl.pallas_call(
        functools.partial(_attn_kernel, n_lat=n_lat, heads_per_pass=2),
        out_shape=jax.ShapeDtypeStruct((n, ATTN_HEADS * HEAD_DIM), BF16),
        grid=(batch, ATTN_KV_HEADS, n_lat + n_ctx),
        in_specs=[
            pl.BlockSpec((tq, gw), qmap),
            pl.BlockSpec((t_lat, HEAD_DIM), lambda b, j, i: (b, j)),
            pl.BlockSpec((t_ctx, HEAD_DIM), lambda b, j, i: (ctx0 // t_ctx + b, j)),
            pl.BlockSpec((t_lat, HEAD_DIM), lambda b, j, i: (b, j)),
            pl.BlockSpec((t_ctx, HEAD_DIM), lambda b, j, i: (ctx0 // t_ctx + b, j)),
        ],
        out_specs=pl.BlockSpec((tq, gw), qmap),
        compiler_params=_cparams(("parallel", "parallel", "arbitrary")),
        name="attention",
    )(q, k, k, v, v)


def _hgrn_chunk(hq, z, v, lb, st, *, reverse):
    c = hq.shape[0]
    sig = jax.nn.sigmoid(z)
    f = lb + (1.0 - lb) * sig
    logf = jnp.log(jnp.maximum(f, F_MIN))
    kk = (1.0 - lb) * jax.nn.sigmoid(-z)
    q = _silu(hq)

    row = lax.broadcasted_iota(jnp.int32, (c, c), 0)
    col = lax.broadcasted_iota(jnp.int32, (c, c), 1)
    rid = lax.broadcasted_iota(jnp.int32, (c, HG_D), 0)
    later = (row < col) if reverse else (row > col)

    a = jnp.where(row == col, _dot_nt(q.astype(BF16), kk.astype(BF16)), 0.0)
    cum, tot = logf, logf
    d, ld = 1, 0
    while d < c:
        second = (rid & d) != 0
        sel = jnp.logical_not(second) if reverse else second
        e = jnp.exp(jnp.where(sel, cum, tot - cum))
        ad = _dot_nt((q * e).astype(BF16), (kk * e).astype(BF16))
        a = a + jnp.where(jnp.logical_and(((row ^ col) >> ld) == 1, later), ad, 0.0)
        up = pltpu.roll(tot, d, 0)
        dn = pltpu.roll(tot, c - d, 0)
        prev, nxt = (dn, up) if reverse else (up, dn)
        cum = cum + jnp.where(sel, prev, 0.0)
        tot = tot + jnp.where(sel, prev, nxt)
        d, ld = d * 2, ld + 1

    qd = (q * jnp.exp(cum)).astype(BF16)
    kd = (kk * jnp.exp(tot - cum)).astype(BF16)
    vb = v.astype(BF16)
    o = _dot(a.astype(BF16), vb) + _dot_nt(qd, st.astype(BF16))
    st_new = st * jnp.exp(tot[0:1, :]) + _dot(vb.T, kd)
    return o, st_new


def _hgrn_kernel(*refs, reverse, finalize):
    if finalize:
        hq_ref, hf_ref, hi_ref, lb_ref, of_ref, gt_ref, g_ref, o_ref, st_ref = refs
    else:
        hq_ref, hf_ref, hi_ref, lb_ref, o_ref, st_ref = refs

    @pl.when(pl.program_id(1) == 0)
    def _():
        st_ref[...] = jnp.zeros_like(st_ref)

    for h in range(HG_HEADS):
        sl = slice(h * HG_D, (h + 1) * HG_D)
        o, st_new = _hgrn_chunk(hq_ref[:, sl], hf_ref[:, sl], hi_ref[:, sl], lb_ref[:, sl],
                                st_ref[h], reverse=reverse)
        st_ref[h] = st_new
        if finalize:
            y = _rms(o + of_ref[:, sl], g_ref[...])
            o_ref[:, sl] = (y * _silu(gt_ref[:, sl])).astype(o_ref.dtype)
        else:
            o_ref[:, sl] = o


def hgrn_scan(p, lb, *, batch, t_lat, t_ctx, col0, reverse, fwd_out=None, gain=None):
    n = p.shape[0]
    c = HG_CHUNK
    w = HG_HEADS * HG_D
    n_lat, n_ctx = t_lat // c, t_ctx // c
    ctx0 = batch * n_lat

    def rows(b, s):
        if reverse:
            return jnp.where(s < n_ctx, ctx0 + b * n_ctx + (n_ctx - 1 - s),
                             b * n_lat + (n_lat - 1 - (s - n_ctx)))
        return jnp.where(s < n_ctx, ctx0 + b * n_ctx + s, b * n_lat + (s - n_ctx))

    def spec(cb):
        return pl.BlockSpec((c, w), lambda b, s: (rows(b, s), cb))

    finalize = fwd_out is not None
    in_specs = [spec(col0), spec(col0 + (2 if reverse else 1)), spec(col0 + 3),
                pl.BlockSpec((1, w), lambda b, s: (0, 0))]
    args = [p, p, p, lb.reshape(1, w)]
    if finalize:
        in_specs += [spec(0), spec(col0 + 4), pl.BlockSpec((1, HG_D), lambda b, s: (0, 0))]
        args += [fwd_out, p, gain.reshape(1, HG_D)]
    return pl.pallas_call(
        functools.partial(_hgrn_kernel, reverse=reverse, finalize=finalize),
        out_shape=jax.ShapeDtypeStruct((n, w), BF16 if finalize else F32),
        grid=(batch, n_lat + n_ctx),
        in_specs=in_specs,
        out_specs=spec(0),
        scratch_shapes=[pltpu.VMEM((HG_HEADS, HG_D, HG_D), F32)],
        compiler_params=_cparams(("parallel", "arbitrary")),
        name="hgrn_bwd" if reverse else "hgrn_fwd",
    )(*args)


def _sg_kernel(u_ref, v_ref, g_ref, w_ref, b_ref, o_ref):
    for r in range(0, u_ref.shape[0], SG_CHUNK):
        rs = slice(r, r + SG_CHUNK)
        for g in range(SG_GROUPS):
            sl = slice(g * SG_DIM, (g + 1) * SG_DIM)
            vn = _rms(jax.nn.gelu(v_ref[rs, sl]), g_ref[:, sl])
            mixed = _dot(w_ref[g], vn.astype(BF16)) + b_ref[:, sl]
            o_ref[rs, sl] = (jax.nn.gelu(u_ref[rs, sl]) * mixed).astype(BF16)


def spatial_gate(p, g, w, bias_full, *, col0, tm):
    n = p.shape[0]
    sw = SG_GROUPS * SG_DIM
    return pl.pallas_call(
        _sg_kernel,
        out_shape=jax.ShapeDtypeStruct((n, sw), BF16),
        grid=(n // tm,),
        in_specs=[
            pl.BlockSpec((tm, sw), lambda i: (i, col0)),
            pl.BlockSpec((tm, sw), lambda i: (i, col0 + 1)),
            pl.BlockSpec((1, sw), lambda i: (0, 0)),
            pl.BlockSpec((SG_GROUPS, SG_CHUNK, SG_CHUNK), lambda i: (0, 0, 0)),
            pl.BlockSpec((SG_CHUNK, sw), lambda i: (0, 0)),
        ],
        out_specs=pl.BlockSpec((tm, sw), lambda i: (i, 0)),
        compiler_params=_cparams(("parallel",)),
        name="spatial_gate",
    )(p, p, g.reshape(1, sw), w, bias_full)


def _out_kernel(a_ref, h_ref, s_ref, wa_ref, wh_ref, ws_ref, x_ref, gate_ref, o_ref):
    acc = _dot(a_ref[...], wa_ref[...]) + _dot(h_ref[...], wh_ref[...]) + _dot(s_ref[...], ws_ref[...])
    o_ref[...] = x_ref[...] + gate_ref[0] * acc


def out_proj(attn, hg, sg, w, x, mod, *, tile_mod, tm, tn=512):
    n, d = x.shape
    wa, wh, ws = attn.shape[1], hg.shape[1], sg.shape[1]
    assert wa % wh == 0 and wh == ws and n % tm == 0 and d % tn == 0
    return pl.pallas_call(
        _out_kernel,
        out_shape=jax.ShapeDtypeStruct((n, d), F32),
        grid=(n // tm, d // tn),
        in_specs=[
            pl.BlockSpec((tm, wa), lambda i, j: (i, 0)),
            pl.BlockSpec((tm, wh), lambda i, j: (i, 0)),
            pl.BlockSpec((tm, ws), lambda i, j: (i, 0)),
            pl.BlockSpec((wa, tn), lambda i, j: (0, j)),
            pl.BlockSpec((wh, tn), lambda i, j: (wa // wh, j)),
            pl.BlockSpec((ws, tn), lambda i, j: (wa // wh + 1, j)),
            pl.BlockSpec((tm, tn), lambda i, j: (i, j)),
            pl.BlockSpec((1, 1, tn), lambda i, j: (tile_mod(i) * N_MOD + 2, 0, j)),
        ],
        out_specs=pl.BlockSpec((tm, tn), lambda i, j: (i, j)),
        compiler_params=_cparams(("parallel", "parallel")),
        name="out_proj",
    )(attn, hg, sg, w, w, w, x, mod)


def _ffn_kernel(x_ref, xp_ref, xn_ref, g_ref, sh_ref, sc_ref, gate_ref,
                wg_ref, wv_ref, cwg_ref, cwv_ref, cbg_ref, cbv_ref, wd_ref,
                o_ref, h_ref, *, seq_of_tile, rows):
    i, j = pl.program_id(0), pl.program_id(1)
    tm = x_ref.shape[0]
    halo = xp_ref.shape[0]

    def norm_mod(x):
        return (_rms(x, g_ref[...]) * (1.0 + sc_ref[0]) + sh_ref[0]).astype(BF16)

    @pl.when(j == 0)
    def _():
        for r in range(0, tm, rows):
            h_ref[r:r + rows, :] = norm_mod(x_ref[r:r + rows, :])
        h_ref[tm:tm + halo, :] = norm_mod(xp_ref[...])
        h_ref[tm + halo:tm + 2 * halo, :] = norm_mod(xn_ref[...])

    rid = lax.broadcasted_iota(jnp.int32, (tm, 1), 0)
    pos = (i * tm + rid) & (seq_of_tile(i) - 1)
    first, last = pos == 0, pos == seq_of_tile(i) - 1

    def conv(w_ref, cw_ref, cb_ref):
        u = _dot(h_ref[0:tm, :], w_ref[...])
        uh = _dot(h_ref[tm:tm + 2 * halo, :], w_ref[...])
        before = jnp.where(rid == 0, uh[halo - 1:halo, :], pltpu.roll(u, 1, 0))
        after = jnp.where(rid == tm - 1, uh[halo:halo + 1, :], pltpu.roll(u, tm - 1, 0))
        before = jnp.where(first, 0.0, before)
        after = jnp.where(last, 0.0, after)
        return cb_ref[...] + before * cw_ref[0:1, :] + u * cw_ref[1:2, :] + after * cw_ref[2:3, :]

    act = (_silu(conv(wg_ref, cwg_ref, cbg_ref)) * conv(wv_ref, cwv_ref, cbv_ref)).astype(BF16)
    part = _dot(act, wd_ref[...])

    @pl.when(j == 0)
    def _():
        o_ref[...] = part

    @pl.when(j > 0)
    def _():
        o_ref[...] += part

    @pl.when(j == pl.num_programs(1) - 1)
    def _():
        o_ref[...] = x_ref[...] + gate_ref[0] * o_ref[...]


def conv_ffn(x, g, mod, w_up, conv_w, conv_b, w_down, *, tile_mod, seq_of_tile, tm, tf=512):
    n, d = x.shape
    dff = w_down.shape[0]
    assert n % tm == 0 and dff % tf == 0
    nj = dff // tf
    halo = 8
    hb = tm // halo
    nhb = n // halo
    cb = conv_b.reshape(1, 2 * dff)
    return pl.pallas_call(
        functools.partial(_ffn_kernel, seq_of_tile=seq_of_tile, rows=min(tm, 256)),
        out_shape=jax.ShapeDtypeStruct((n, d), F32),
        grid=(n // tm, nj),
        in_specs=[
            pl.BlockSpec((tm, d), lambda i, j: (i, 0)),
            pl.BlockSpec((halo, d), lambda i, j: (jnp.maximum(i * hb - 1, 0), 0)),
            pl.BlockSpec((halo, d), lambda i, j: (jnp.minimum((i + 1) * hb, nhb - 1), 0)),
            pl.BlockSpec((1, d), lambda i, j: (0, 0)),
            pl.BlockSpec((1, 1, d), lambda i, j: (tile_mod(i) * N_MOD + 3, 0, 0)),
            pl.BlockSpec((1, 1, d), lambda i, j: (tile_mod(i) * N_MOD + 4, 0, 0)),
            pl.BlockSpec((1, 1, d), lambda i, j: (tile_mod(i) * N_MOD + 5, 0, 0)),
            pl.BlockSpec((d, tf), lambda i, j: (0, j)),
            pl.BlockSpec((d, tf), lambda i, j: (0, nj + j)),
            pl.BlockSpec((CONV_W, tf), lambda i, j: (0, j)),
            pl.BlockSpec((CONV_W, tf), lambda i, j: (0, nj + j)),
            pl.BlockSpec((1, tf), lambda i, j: (0, j)),
            pl.BlockSpec((1, tf), lambda i, j: (0, nj + j)),
            pl.BlockSpec((tf, d), lambda i, j: (j, 0)),
        ],
        out_specs=pl.BlockSpec((tm, d), lambda i, j: (i, 0)),
        scratch_shapes=[pltpu.VMEM((tm + 2 * halo, d), BF16)],
        compiler_params=_cparams(("parallel", "arbitrary")),
        name="conv_ffn",
    )(x, x, x, g.reshape(1, d), mod, mod, mod, w_up, w_up, conv_w, conv_w, cb, cb, w_down)


def _norm_kernel(x_ref, g_ref, o_ref):
    o_ref[...] = _rms(x_ref[...], g_ref[...])


def final_norm(x, g, *, rows, tm=256):
    d = x.shape[1]
    return pl.pallas_call(
        _norm_kernel,
        out_shape=jax.ShapeDtypeStruct((rows, d), F32),
        grid=(rows // tm,),
        in_specs=[pl.BlockSpec((tm, d), lambda i: (i, 0)), pl.BlockSpec((1, d), lambda i: (0, 0))],
        out_specs=pl.BlockSpec((tm, d), lambda i: (i, 0)),
        compiler_params=_cparams(("parallel",)),
        name="final_norm",
    )(x, g.reshape(1, d))


def _rope_tables(t_lat, tm):
    rows = t_lat // GRID_W
    row = jnp.repeat(jnp.arange(rows, dtype=F32), GRID_W)
    col = jnp.tile(jnp.arange(GRID_W, dtype=F32), rows)
    n_freq = HEAD_DIM // 4
    inv = ROPE_THETA ** (-jnp.arange(n_freq, dtype=F32) / n_freq)
    ang = jnp.concatenate([row[:, None] * inv, col[:, None] * inv], axis=-1)
    cos2 = jnp.repeat(jnp.cos(ang), 2, axis=-1)
    sin2 = jnp.repeat(jnp.sin(ang), 2, axis=-1) * jnp.tile(jnp.array([-1.0, 1.0], F32), HEAD_DIM // 2)
    cos2 = jnp.concatenate([cos2, jnp.ones((tm, HEAD_DIM), F32)], axis=0)
    sin2 = jnp.concatenate([sin2, jnp.zeros((tm, HEAD_DIM), F32)], axis=0)
    return cos2, sin2


def _lower_bounds(lb_param):
    p = jax.nn.softmax(lb_param.astype(F32), axis=1)
    return jnp.cumsum(p, axis=1) - p[:, :1]


def kernel(x, c, ctx, c_ctx, w_ada, b_ada, norm1_g, w_in, q_norm_g, k_norm_g, hg_lower_bounds,
           hg_norm_g, sg_norm_g, sg_w, sg_b, w_out, norm2_g, w_up, conv_w, conv_b, w_down,
           final_norm_g):
    batch, t_lat, d = x.shape
    t_ctx = ctx.shape[1]
    depth = w_in.shape[0]
    n_lat, n_ctx = batch * t_lat, batch * t_ctx
    assert t_lat & (t_lat - 1) == 0 and t_ctx & (t_ctx - 1) == 0

    tm = min(1024, n_ctx)
    tf = tm // 2
    assert t_lat % tm == 0 and n_ctx % tm == 0

    def tile_mod(rows):
        return lambda i: jnp.minimum(i // (t_lat // rows), batch)

    def seq_of_tile(i):
        return jnp.where(i < n_lat // tf, t_lat, t_ctx)

    def tile_rope(i):
        return jnp.where(i < n_lat // tm, i % (t_lat // tm), t_lat // tm)

    xs = jnp.concatenate([x.reshape(n_lat, d), ctx.reshape(n_ctx, d)], axis=0)
    cin = jnp.concatenate([c, c_ctx[None, :], jnp.zeros((8 - batch - 1, d), F32)], axis=0)
    mods = ada_table(cin, w_ada, b_ada)[:, :batch + 1, :]
    mods = mods.reshape(depth, (batch + 1) * N_MOD, 1, d)
    cos2, sin2 = _rope_tables(t_lat, tm)
    lbs = _lower_bounds(hg_lower_bounds)
    qw = ATTN_HEADS * HEAD_DIM
    kvw = ATTN_KV_HEADS * HEAD_DIM
    hg_col0 = (qw + 2 * kvw) // (HG_HEADS * HG_D)
    sg_col0 = hg_col0 + 5

    for l in range(depth):
        mod = mods[l]
        p = in_proj(xs, norm1_g[l], mod, w_in[l].astype(BF16), tile_mod=tile_mod(tm), tm=tm)
        q, k, v = qk_prep(p, cos2, sin2, q_norm_g[l], k_norm_g[l], tile_rope=tile_rope, tm=tm)
        attn = attention(q, k, v, batch=batch, t_lat=t_lat, t_ctx=t_ctx)
        o_f = hgrn_scan(p, lbs[0, l], batch=batch, t_lat=t_lat, t_ctx=t_ctx, col0=hg_col0,
                        reverse=False)
        hg = hgrn_scan(p, lbs[1, l], batch=batch, t_lat=t_lat, t_ctx=t_ctx, col0=hg_col0,
                       reverse=True, fwd_out=o_f, gain=hg_norm_g[l])
        bias_full = jnp.repeat(sg_b[l].T, SG_DIM, axis=1)
        sg = spatial_gate(p, sg_norm_g[l], sg_w[l].astype(BF16), bias_full, col0=sg_col0, tm=tm)
        xs = out_proj(attn, hg, sg, w_out[l].astype(BF16), xs, mod, tile_mod=tile_mod(tm), tm=tm)
        xs = conv_ffn(xs, norm2_g[l], mod, w_up[l].astype(BF16), conv_w[l], conv_b[l],
                      w_down[l].astype(BF16), tile_mod=tile_mod(tf), seq_of_tile=seq_of_tile, tm=tf)

    return final_norm(xs, final_norm_g, rows=n_lat).reshape(batch, t_lat, d)
```

```python
import functools

import jax
import jax.numpy as jnp
from jax import lax
from jax.experimental import pallas as pl
from jax.experimental.pallas import tpu as pltpu

F32 = jnp.float32
BF16 = jnp.bfloat16

EPS = 1e-6
F_MIN = 1e-30
N_MOD = 6
HEAD_DIM = 128
ATTN_HEADS = 8
ATTN_KV_HEADS = 2
ATTN_GROUP = ATTN_HEADS // ATTN_KV_HEADS
ROPE_THETA = 10000.0
GRID_W = 64
HG_HEADS = 4
HG_D = 128
SG_GROUPS = 4
SG_DIM = 128
SG_CHUNK = 128
CONV_W = 3
LANE = 128
HG_CHUNK = 128
VMEM_LIMIT = 56 * 1024 * 1024


def _cparams(sem):
    return pltpu.CompilerParams(dimension_semantics=sem, vmem_limit_bytes=VMEM_LIMIT)


def _dot(a, b):
    return jnp.dot(a, b, preferred_element_type=F32)


def _dot_nt(a, b):
    return lax.dot_general(a, b, (((1,), (1,)), ((), ())), preferred_element_type=F32)


def _rms(x, g):
    return x * lax.rsqrt(jnp.mean(x * x, axis=-1, keepdims=True) + EPS) * g


def _silu(x):
    return x * jax.nn.sigmoid(x)


def _ada_kernel(c_ref, w_ref, b_ref, o_ref):
    s = _silu(c_ref[...]).astype(BF16)
    o_ref[...] = _dot(s, w_ref[...].astype(BF16)) + b_ref[...]


def ada_table(cin, w_ada, b_ada, tn=1024):
    depth, d, n = w_ada.shape
    assert n % tn == 0
    return pl.pallas_call(
        _ada_kernel,
        out_shape=jax.ShapeDtypeStruct((depth, 8, n), F32),
        grid=(depth, n // tn),
        in_specs=[
            pl.BlockSpec((8, d), lambda l, j: (0, 0)),
            pl.BlockSpec((None, d, tn), lambda l, j: (l, 0, j)),
            pl.BlockSpec((None, 1, tn), lambda l, j: (l, 0, j)),
        ],
        out_specs=pl.BlockSpec((None, 8, tn), lambda l, j: (l, 0, j)),
        compiler_params=_cparams(("parallel", "parallel")),
        name="ada_table",
    )(cin, w_ada, b_ada.reshape(depth, 1, n))


def _in_kernel(x_ref, g_ref, sh_ref, sc_ref, w_ref, o_ref, h_ref, *, rows):
    @pl.when(pl.program_id(1) == 0)
    def _():
        for r in range(0, x_ref.shape[0], rows):
            y = _rms(x_ref[r:r + rows, :], g_ref[...])
            h_ref[r:r + rows, :] = (y * (1.0 + sc_ref[0]) + sh_ref[0]).astype(BF16)

    o_ref[...] = _dot(h_ref[...], w_ref[...])


def in_proj(x, g, mod, w, *, tile_mod, tm, tn=512):
    n, d = x.shape
    cols = w.shape[1]
    assert n % tm == 0 and cols % tn == 0
    return pl.pallas_call(
        functools.partial(_in_kernel, rows=min(tm, 256)),
        out_shape=jax.ShapeDtypeStruct((n, cols), F32),
        grid=(n // tm, cols // tn),
        in_specs=[
            pl.BlockSpec((tm, d), lambda i, j: (i, 0)),
            pl.BlockSpec((1, d), lambda i, j: (0, 0)),
            pl.BlockSpec((1, 1, d), lambda i, j: (tile_mod(i) * N_MOD + 0, 0, 0)),
            pl.BlockSpec((1, 1, d), lambda i, j: (tile_mod(i) * N_MOD + 1, 0, 0)),
            pl.BlockSpec((d, tn), lambda i, j: (0, j)),
        ],
        out_specs=pl.BlockSpec((tm, tn), lambda i, j: (i, j)),
        scratch_shapes=[pltpu.VMEM((tm, d), BF16)],
        compiler_params=_cparams(("parallel", "arbitrary")),
        name="in_proj",
    )(x, g.reshape(1, d), mod, mod, w)


def _rope(y, cos2, sin2):
    lane = lax.broadcasted_iota(jnp.int32, y.shape, 1)
    swapped = jnp.where((lane & 1) == 0, pltpu.roll(y, LANE - 1, 1), pltpu.roll(y, 1, 1))
    return y * cos2 + swapped * sin2


def _qk_kernel(q_ref, kv_ref, cos_ref, sin_ref, qg_ref, kg_ref, qo_ref, ko_ref, vo_ref):
    cos2, sin2 = cos_ref[...], sin_ref[...]
    scale = HEAD_DIM ** -0.5
    for h in range(ATTN_HEADS):
        sl = slice(h * HEAD_DIM, (h + 1) * HEAD_DIM)
        y = _rms(q_ref[:, sl], qg_ref[...])
        qo_ref[:, sl] = (_rope(y, cos2, sin2) * scale).astype(BF16)
    for h in range(ATTN_KV_HEADS):
        sl = slice(h * HEAD_DIM, (h + 1) * HEAD_DIM)
        y = _rms(kv_ref[:, sl], kg_ref[...])
        ko_ref[:, sl] = _rope(y, cos2, sin2).astype(BF16)
    kvw = ATTN_KV_HEADS * HEAD_DIM
    vo_ref[...] = kv_ref[:, kvw:2 * kvw].astype(BF16)


def qk_prep(p, cos2, sin2, qg, kg, *, tile_rope, tm):
    n = p.shape[0]
    qw = ATTN_HEADS * HEAD_DIM
    kvw = ATTN_KV_HEADS * HEAD_DIM
    return pl.pallas_call(
        _qk_kernel,
        out_shape=(jax.ShapeDtypeStruct((n, qw), BF16),
                   jax.ShapeDtypeStruct((n, kvw), BF16),
                   jax.ShapeDtypeStruct((n, kvw), BF16)),
        grid=(n // tm,),
        in_specs=[
            pl.BlockSpec((tm, qw), lambda i: (i, 0)),
            pl.BlockSpec((tm, 2 * kvw), lambda i: (i, qw // (2 * kvw))),
            pl.BlockSpec((tm, HEAD_DIM), lambda i: (tile_rope(i), 0)),
            pl.BlockSpec((tm, HEAD_DIM), lambda i: (tile_rope(i), 0)),
            pl.BlockSpec((1, HEAD_DIM), lambda i: (0, 0)),
            pl.BlockSpec((1, HEAD_DIM), lambda i: (0, 0)),
        ],
        out_specs=(pl.BlockSpec((tm, qw), lambda i: (i, 0)),
                   pl.BlockSpec((tm, kvw), lambda i: (i, 0)),
                   pl.BlockSpec((tm, kvw), lambda i: (i, 0))),
        compiler_params=_cparams(("parallel",)),
        name="qk_prep",
    )(p, p, cos2, sin2, qg.reshape(1, HEAD_DIM), kg.reshape(1, HEAD_DIM))


def _softmax_pv(q, segs, ck):
    m = l = acc = None
    for k_ref, v_ref in segs:
        n = k_ref.shape[0]
        cs = min(ck, n)
        for c0 in range(0, n, cs):
            s = _dot_nt(q, k_ref[c0:c0 + cs, :])
            m_c = s.max(axis=-1, keepdims=True)
            m_new = m_c if m is None else jnp.maximum(m, m_c)
            p = jnp.exp(s - m_new)
            p_lanes = functools.reduce(jnp.add, [p[:, i:i + LANE] for i in range(0, cs, LANE)])
            pv = _dot(p.astype(BF16), v_ref[c0:c0 + cs, :])
            if m is None:
                l, acc = p_lanes, pv
            else:
                alpha = jnp.exp(m - m_new)
                l, acc = alpha * l + p_lanes, alpha * acc + pv
            m = m_new
    return acc / l.sum(axis=-1, keepdims=True)


def _attn_kernel(q_ref, kl_ref, kc_ref, vl_ref, vc_ref, o_ref, *, n_lat, ck):
    qi = pl.program_id(2)
    tq = q_ref.shape[0]

    def run(segs):
        hs = range(ATTN_GROUP)
        q = jnp.concatenate([q_ref[:, h * HEAD_DIM:(h + 1) * HEAD_DIM] for h in hs], axis=0)
        o = _softmax_pv(q, segs, ck)
        for h in hs:
            o_ref[:, h * HEAD_DIM:(h + 1) * HEAD_DIM] = o[h * tq:(h + 1) * tq].astype(BF16)

    @pl.when(qi < n_lat)
    def _():
        run([(kl_ref, vl_ref), (kc_ref, vc_ref)])

    @pl.when(qi >= n_lat)
    def _():
        run([(kc_ref, vc_ref)])


def attention(q, k, v, *, batch, t_lat, t_ctx, tq=256, ck=512):
    n = q.shape[0]
    n_lat = t_lat // tq
    n_ctx = t_ctx // tq
    gw = ATTN_GROUP * HEAD_DIM
    ctx0 = batch * t_lat

    def qmap(b, j, i):
        return (jnp.where(i < n_lat, b * n_lat + i, ctx0 // tq + b * n_ctx + (i - n_lat)), j)

    return pl.pallas_call(
        functools.partial(_attn_kernel, n_lat=n_lat, ck=ck),
        out_shape=jax.ShapeDtypeStruct((n, ATTN_HEADS * HEAD_DIM), BF16),
        grid=(batch, ATTN_KV_HEADS, n_lat + n_ctx),
        in_specs=[
            pl.BlockSpec((tq, gw), qmap),
            pl.BlockSpec((t_lat, HEAD_DIM), lambda b, j, i: (b, j)),
            pl.BlockSpec((t_ctx, HEAD_DIM), lambda b, j, i: (ctx0 // t_ctx + b, j)),
            pl.BlockSpec((t_lat, HEAD_DIM), lambda b, j, i: (b, j)),
            pl.BlockSpec((t_ctx, HEAD_DIM), lambda b, j, i: (ctx0 // t_ctx + b, j)),
        ],
        out_specs=pl.BlockSpec((tq, gw), qmap),
        compiler_params=_cparams(("parallel", "parallel", "arbitrary")),
        name="attention",
    )(q, k, k, v, v)


def _hgrn_chunk(hq, z, v, lb, st, *, reverse):
    c = hq.shape[0]
    sig = jax.nn.sigmoid(z)
    f = lb + (1.0 - lb) * sig
    logf = jnp.log(jnp.maximum(f, F_MIN))
    kk = (1.0 - lb) * jax.nn.sigmoid(-z)
    q = _silu(hq)

    row = lax.broadcasted_iota(jnp.int32, (c, c), 0)
    col = lax.broadcasted_iota(jnp.int32, (c, c), 1)
    rid = lax.broadcasted_iota(jnp.int32, (c, HG_D), 0)
    later = (row < col) if reverse else (row > col)

    a = jnp.where(row == col, _dot_nt(q.astype(BF16), kk.astype(BF16)), 0.0)
    cum, tot = logf, logf
    d, ld = 1, 0
    while d < c:
        second = (rid & d) != 0
        sel = jnp.logical_not(second) if reverse else second
        e = jnp.exp(jnp.where(sel, cum, tot - cum))
        ad = _dot_nt((q * e).astype(BF16), (kk * e).astype(BF16))
        a = a + jnp.where(jnp.logical_and(((row ^ col) >> ld) == 1, later), ad, 0.0)
        up = pltpu.roll(tot, d, 0)
        dn = pltpu.roll(tot, c - d, 0)
        prev, nxt = (dn, up) if reverse else (up, dn)
        cum = cum + jnp.where(sel, prev, 0.0)
        tot = tot + jnp.where(sel, prev, nxt)
        d, ld = d * 2, ld + 1

    qd = (q * jnp.exp(cum)).astype(BF16)
    kd = (kk * jnp.exp(tot - cum)).astype(BF16)
    vb = v.astype(BF16)
    o = _dot(a.astype(BF16), vb) + _dot_nt(qd, st.astype(BF16))
    st_new = st * jnp.exp(tot[0:1, :]) + _dot(vb.T, kd)
    return o, st_new


def _hgrn_kernel(*refs, reverse, finalize):
    if finalize:
        hq_ref, hf_ref, hi_ref, lb_ref, of_ref, gt_ref, g_ref, o_ref, st_ref = refs
    else:
        hq_ref, hf_ref, hi_ref, lb_ref, o_ref, st_ref = refs

    @pl.when(pl.program_id(1) == 0)
    def _():
        st_ref[...] = jnp.zeros_like(st_ref)

    for h in range(HG_HEADS):
        sl = slice(h * HG_D, (h + 1) * HG_D)
        o, st_new = _hgrn_chunk(hq_ref[:, sl], hf_ref[:, sl], hi_ref[:, sl], lb_ref[:, sl],
                                st_ref[h], reverse=reverse)
        st_ref[h] = st_new
        if finalize:
            y = _rms(o + of_ref[:, sl], g_ref[...])
            o_ref[:, sl] = (y * _silu(gt_ref[:, sl])).astype(o_ref.dtype)
        else:
            o_ref[:, sl] = o


def hgrn_scan(p, lb, *, batch, t_lat, t_ctx, col0, reverse, fwd_out=None, gain=None):
    n = p.shape[0]
    c = HG_CHUNK
    w = HG_HEADS * HG_D
    n_lat, n_ctx = t_lat // c, t_ctx // c
    ctx0 = batch * n_lat

    def rows(b, s):
        if reverse:
            return jnp.where(s < n_ctx, ctx0 + b * n_ctx + (n_ctx - 1 - s),
                             b * n_lat + (n_lat - 1 - (s - n_ctx)))
        return jnp.where(s < n_ctx, ctx0 + b * n_ctx + s, b * n_lat + (s - n_ctx))

    def spec(cb):
        return pl.BlockSpec((c, w), lambda b, s: (rows(b, s), cb))

    finalize = fwd_out is not None
    in_specs = [spec(col0), spec(col0 + (2 if reverse else 1)), spec(col0 + 3),
                pl.BlockSpec((1, w), lambda b, s: (0, 0))]
    args = [p, p, p, lb.reshape(1, w)]
    if finalize:
        in_specs += [spec(0), spec(col0 + 4), pl.BlockSpec((1, HG_D), lambda b, s: (0, 0))]
        args += [fwd_out, p, gain.reshape(1, HG_D)]
    return pl.pallas_call(
        functools.partial(_hgrn_kernel, reverse=reverse, finalize=finalize),
        out_shape=jax.ShapeDtypeStruct((n, w), BF16 if finalize else F32),
        grid=(batch, n_lat + n_ctx),
        in_specs=in_specs,
        out_specs=spec(0),
        scratch_shapes=[pltpu.VMEM((HG_HEADS, HG_D, HG_D), F32)],
        compiler_params=_cparams(("parallel", "arbitrary")),
        name="hgrn_bwd" if reverse else "hgrn_fwd",
    )(*args)


def _sg_kernel(u_ref, v_ref, g_ref, w_ref, b_ref, o_ref):
    for r in range(0, u_ref.shape[0], SG_CHUNK):
        rs = slice(r, r + SG_CHUNK)
        for g in range(SG_GROUPS):
            sl = slice(g * SG_DIM, (g + 1) * SG_DIM)
            vn = _rms(jax.nn.gelu(v_ref[rs, sl]), g_ref[:, sl])
            mixed = _dot(w_ref[g], vn.astype(BF16)) + b_ref[:, sl]
            o_ref[rs, sl] = (jax.nn.gelu(u_ref[rs, sl]) * mixed).astype(BF16)


def spatial_gate(p, g, w, bias_full, *, col0, tm):
    n = p.shape[0]
    sw = SG_GROUPS * SG_DIM
    return pl.pallas_call(
        _sg_kernel,
        out_shape=jax.ShapeDtypeStruct((n, sw), BF16),
        grid=(n // tm,),
        in_specs=[
            pl.BlockSpec((tm, sw), lambda i: (i, col0)),
            pl.BlockSpec((tm, sw), lambda i: (i, col0 + 1)),
            pl.BlockSpec((1, sw), lambda i: (0, 0)),
            pl.BlockSpec((SG_GROUPS, SG_CHUNK, SG_CHUNK), lambda i: (0, 0, 0)),
            pl.BlockSpec((SG_CHUNK, sw), lambda i: (0, 0)),
        ],
        out_specs=pl.BlockSpec((tm, sw), lambda i: (i, 0)),
        compiler_params=_cparams(("parallel",)),
        name="spatial_gate",
    )(p, p, g.reshape(1, sw), w, bias_full)


def _out_kernel(a_ref, h_ref, s_ref, wa_ref, wh_ref, ws_ref, x_ref, gate_ref, o_ref):
    acc = _dot(a_ref[...], wa_ref[...]) + _dot(h_ref[...], wh_ref[...]) + _dot(s_ref[...], ws_ref[...])
    o_ref[...] = x_ref[...] + gate_ref[0] * acc


def out_proj(attn, hg, sg, w, x, mod, *, tile_mod, tm, tn=512):
    n, d = x.shape
    wa, wh, ws = attn.shape[1], hg.shape[1], sg.shape[1]
    assert wa % wh == 0 and wh == ws and n % tm == 0 and d % tn == 0
    return pl.pallas_call(
        _out_kernel,
        out_shape=jax.ShapeDtypeStruct((n, d), F32),
        grid=(n // tm, d // tn),
        in_specs=[
            pl.BlockSpec((tm, wa), lambda i, j: (i, 0)),
            pl.BlockSpec((tm, wh), lambda i, j: (i, 0)),
            pl.BlockSpec((tm, ws), lambda i, j: (i, 0)),
            pl.BlockSpec((wa, tn), lambda i, j: (0, j)),
            pl.BlockSpec((wh, tn), lambda i, j: (wa // wh, j)),
            pl.BlockSpec((ws, tn), lambda i, j: (wa // wh + 1, j)),
            pl.BlockSpec((tm, tn), lambda i, j: (i, j)),
            pl.BlockSpec((1, 1, tn), lambda i, j: (tile_mod(i) * N_MOD + 2, 0, j)),
        ],
        out_specs=pl.BlockSpec((tm, tn), lambda i, j: (i, j)),
        compiler_params=_cparams(("parallel", "parallel")),
        name="out_proj",
    )(attn, hg, sg, w, w, w, x, mod)


def _ffn_kernel(x_ref, xp_ref, xn_ref, g_ref, sh_ref, sc_ref, gate_ref,
                wg_ref, wv_ref, cwg_ref, cwv_ref, cbg_ref, cbv_ref, wd_ref,
                o_ref, h_ref, *, seq_of_tile, rows, chunk):
    i, j = pl.program_id(0), pl.program_id(1)
    tm = x_ref.shape[0]
    halo = xp_ref.shape[0]

    def norm_mod(x):
        return (_rms(x, g_ref[...]) * (1.0 + sc_ref[0]) + sh_ref[0]).astype(BF16)

    @pl.when(j == 0)
    def _():
        for r in range(0, tm, rows):
            h_ref[r:r + rows, :] = norm_mod(x_ref[r:r + rows, :])
        h_ref[tm:tm + halo, :] = norm_mod(xp_ref[...])
        h_ref[tm + halo:tm + 2 * halo, :] = norm_mod(xn_ref[...])
        o_ref[...] = jnp.zeros_like(o_ref)

    rid = lax.broadcasted_iota(jnp.int32, (tm, 1), 0)
    pos = (i * tm + rid) & (seq_of_tile(i) - 1)
    first, last = pos == 0, pos == seq_of_tile(i) - 1

    def conv(w_ref, cw_ref, cb_ref, cs):
        u = _dot(h_ref[0:tm, :], w_ref[:, cs])
        uh = _dot(h_ref[tm:tm + 2 * halo, :], w_ref[:, cs])
        before = jnp.where(rid == 0, uh[halo - 1:halo, :], pltpu.roll(u, 1, 0))
        after = jnp.where(rid == tm - 1, uh[halo:halo + 1, :], pltpu.roll(u, tm - 1, 0))
        before = jnp.where(first, 0.0, before)
        after = jnp.where(last, 0.0, after)
        return (cb_ref[:, cs] + before * cw_ref[0:1, cs] + u * cw_ref[1:2, cs]
                + after * cw_ref[2:3, cs])

    part = None
    for c0 in range(0, wd_ref.shape[0], chunk):
        cs = slice(c0, c0 + chunk)
        act = (_silu(conv(wg_ref, cwg_ref, cbg_ref, cs)) * conv(wv_ref, cwv_ref, cbv_ref, cs))
        p = _dot(act.astype(BF16), wd_ref[cs, :])
        part = p if part is None else part + p
    o_ref[...] += part

    @pl.when(j == pl.num_programs(1) - 1)
    def _():
        o_ref[...] = x_ref[...] + gate_ref[0] * o_ref[...]


def conv_ffn(x, g, mod, w_up, conv_w, conv_b, w_down, *, tile_mod, seq_of_tile, tm, tf=512):
    n, d = x.shape
    dff = w_down.shape[0]
    assert n % tm == 0 and dff % tf == 0
    nj = dff // tf
    halo = 8
    hb = tm // halo
    nhb = n // halo
    cb = conv_b.reshape(1, 2 * dff)
    return pl.pallas_call(
        functools.partial(_ffn_kernel, seq_of_tile=seq_of_tile, rows=min(tm, 256), chunk=256),
        out_shape=jax.ShapeDtypeStruct((n, d), F32),
        grid=(n // tm, nj),
        in_specs=[
            pl.BlockSpec((tm, d), lambda i, j: (i, 0)),
            pl.BlockSpec((halo, d), lambda i, j: (jnp.maximum(i * hb - 1, 0), 0)),
            pl.BlockSpec((halo, d), lambda i, j: (jnp.minimum((i + 1) * hb, nhb - 1), 0)),
            pl.BlockSpec((1, d), lambda i, j: (0, 0)),
            pl.BlockSpec((1, 1, d), lambda i, j: (tile_mod(i) * N_MOD + 3, 0, 0)),
            pl.BlockSpec((1, 1, d), lambda i, j: (tile_mod(i) * N_MOD + 4, 0, 0)),
            pl.BlockSpec((1, 1, d), lambda i, j: (tile_mod(i) * N_MOD + 5, 0, 0)),
            pl.BlockSpec((d, tf), lambda i, j: (0, j)),
            pl.BlockSpec((d, tf), lambda i, j: (0, nj + j)),
            pl.BlockSpec((CONV_W, tf), lambda i, j: (0, j)),
            pl.BlockSpec((CONV_W, tf), lambda i, j: (0, nj + j)),
            pl.BlockSpec((1, tf), lambda i, j: (0, j)),
            pl.BlockSpec((1, tf), lambda i, j: (0, nj + j)),
            pl.BlockSpec((tf, d), lambda i, j: (j, 0)),
        ],
        out_specs=pl.BlockSpec((tm, d), lambda i, j: (i, 0)),
        scratch_shapes=[pltpu.VMEM((tm + 2 * halo, d), BF16)],
        compiler_params=_cparams(("parallel", "arbitrary")),
        name="conv_ffn",
    )(x, x, x, g.reshape(1, d), mod, mod, mod, w_up, w_up, conv_w, conv_w, cb, cb, w_down)


def _norm_kernel(x_ref, g_ref, o_ref):
    o_ref[...] = _rms(x_ref[...], g_ref[...])


def final_norm(x, g, *, rows, tm=256):
    d = x.shape[1]
    return pl.pallas_call(
        _norm_kernel,
        out_shape=jax.ShapeDtypeStruct((rows, d), F32),
        grid=(rows // tm,),
        in_specs=[pl.BlockSpec((tm, d), lambda i: (i, 0)), pl.BlockSpec((1, d), lambda i: (0, 0))],
        out_specs=pl.BlockSpec((tm, d), lambda i: (i, 0)),
        compiler_params=_cparams(("parallel",)),
        name="final_norm",
    )(x, g.reshape(1, d))


def _rope_tables(t_lat, tm):
    rows = t_lat // GRID_W
    row = jnp.repeat(jnp.arange(rows, dtype=F32), GRID_W)
    col = jnp.tile(jnp.arange(GRID_W, dtype=F32), rows)
    n_freq = HEAD_DIM // 4
    inv = ROPE_THETA ** (-jnp.arange(n_freq, dtype=F32) / n_freq)
    ang = jnp.concatenate([row[:, None] * inv, col[:, None] * inv], axis=-1)
    cos2 = jnp.repeat(jnp.cos(ang), 2, axis=-1)
    sin2 = jnp.repeat(jnp.sin(ang), 2, axis=-1) * jnp.tile(jnp.array([-1.0, 1.0], F32), HEAD_DIM // 2)
    cos2 = jnp.concatenate([cos2, jnp.ones((tm, HEAD_DIM), F32)], axis=0)
    sin2 = jnp.concatenate([sin2, jnp.zeros((tm, HEAD_DIM), F32)], axis=0)
    return cos2, sin2


def _lower_bounds(lb_param):
    p = jax.nn.softmax(lb_param.astype(F32), axis=1)
    return jnp.cumsum(p, axis=1) - p[:, :1]


def kernel(x, c, ctx, c_ctx, w_ada, b_ada, norm1_g, w_in, q_norm_g, k_norm_g, hg_lower_bounds,
           hg_norm_g, sg_norm_g, sg_w, sg_b, w_out, norm2_g, w_up, conv_w, conv_b, w_down,
           final_norm_g):
    batch, t_lat, d = x.shape
    t_ctx = ctx.shape[1]
    depth = w_in.shape[0]
    n_lat, n_ctx = batch * t_lat, batch * t_ctx
    assert t_lat & (t_lat - 1) == 0 and t_ctx & (t_ctx - 1) == 0

    tm = min(1024, n_ctx)
    tf = tm // 2
    assert t_lat % tm == 0 and n_ctx % tm == 0

    def tile_mod(rows):
        return lambda i: jnp.minimum(i // (t_lat // rows), batch)

    def seq_of_tile(i):
        return jnp.where(i < n_lat // tf, t_lat, t_ctx)

    def tile_rope(i):
        return jnp.where(i < n_lat // tm, i % (t_lat // tm), t_lat // tm)

    xs = jnp.concatenate([x.reshape(n_lat, d), ctx.reshape(n_ctx, d)], axis=0)
    cin = jnp.concatenate([c, c_ctx[None, :], jnp.zeros((8 - batch - 1, d), F32)], axis=0)
    mods = ada_table(cin, w_ada, b_ada)[:, :batch + 1, :]
    mods = mods.reshape(depth, (batch + 1) * N_MOD, 1, d)
    cos2, sin2 = _rope_tables(t_lat, tm)
    lbs = _lower_bounds(hg_lower_bounds)
    qw = ATTN_HEADS * HEAD_DIM
    kvw = ATTN_KV_HEADS * HEAD_DIM
    hg_col0 = (qw + 2 * kvw) // (HG_HEADS * HG_D)
    sg_col0 = hg_col0 + 5

    for l in range(depth):
        mod = mods[l]
        p = in_proj(xs, norm1_g[l], mod, w_in[l].astype(BF16), tile_mod=tile_mod(tm), tm=tm)
        q, k, v = qk_prep(p, cos2, sin2, q_norm_g[l], k_norm_g[l], tile_rope=tile_rope, tm=tm)
        attn = attention(q, k, v, batch=batch, t_lat=t_lat, t_ctx=t_ctx)
        o_f = hgrn_scan(p, lbs[0, l], batch=batch, t_lat=t_lat, t_ctx=t_ctx, col0=hg_col0,
                        reverse=False)
        hg = hgrn_scan(p, lbs[1, l], batch=batch, t_lat=t_lat, t_ctx=t_ctx, col0=hg_col0,
                       reverse=True, fwd_out=o_f, gain=hg_norm_g[l])
        bias_full = jnp.repeat(sg_b[l].T, SG_DIM, axis=1)
        sg = spatial_gate(p, sg_norm_g[l], sg_w[l].astype(BF16), bias_full, col0=sg_col0, tm=tm)
        xs = out_proj(attn, hg, sg, w_out[l].astype(BF16), xs, mod, tile_mod=tile_mod(tm), tm=tm)
        xs = conv_ffn(xs, norm2_g[l], mod, w_up[l].astype(BF16), conv_w[l], conv_b[l],
                      w_down[l].astype(BF16), tile_mod=tile_mod(tf), seq_of_tile=seq_of_tile, tm=tf)

    return final_norm(xs, final_norm_g, rows=n_lat).reshape(batch, t_lat, d)
```

```python
import functools

import jax
import jax.numpy as jnp
from jax import lax
from jax.experimental import pallas as pl
from jax.experimental.pallas import tpu as pltpu

F32 = jnp.float32
BF16 = jnp.bfloat16

EPS = 1e-6
F_MIN = 1e-30
N_MOD = 6
HEAD_DIM = 128
ATTN_HEADS = 8
ATTN_KV_HEADS = 2
ATTN_GROUP = ATTN_HEADS // ATTN_KV_HEADS
ROPE_THETA = 10000.0
GRID_W = 64
HG_HEADS = 4
HG_D = 128
SG_GROUPS = 4
SG_DIM = 128
SG_CHUNK = 128
CONV_W = 3
LANE = 128
HG_CHUNK = 128
VMEM_LIMIT = 60 * 1024 * 1024


def _cparams(sem):
    return pltpu.CompilerParams(dimension_semantics=sem, vmem_limit_bytes=VMEM_LIMIT)


def _dot(a, b):
    return jnp.dot(a, b, preferred_element_type=F32)


def _dot_nt(a, b):
    return lax.dot_general(a, b, (((1,), (1,)), ((), ())), preferred_element_type=F32)


def _rms(x, g):
    return x * lax.rsqrt(jnp.mean(x * x, axis=-1, keepdims=True) + EPS) * g


def _silu(x):
    return x * jax.nn.sigmoid(x)


def _ada_kernel(c_ref, w_ref, b_ref, o_ref):
    s = _silu(c_ref[...]).astype(BF16)
    o_ref[...] = _dot(s, w_ref[...].astype(BF16)) + b_ref[...]


def ada_table(cin, w_ada, b_ada, tn=1024):
    depth, d, n = w_ada.shape
    assert n % tn == 0
    return pl.pallas_call(
        _ada_kernel,
        out_shape=jax.ShapeDtypeStruct((depth, 8, n), F32),
        grid=(depth, n // tn),
        in_specs=[
            pl.BlockSpec((8, d), lambda l, j: (0, 0)),
            pl.BlockSpec((None, d, tn), lambda l, j: (l, 0, j)),
            pl.BlockSpec((None, 1, tn), lambda l, j: (l, 0, j)),
        ],
        out_specs=pl.BlockSpec((None, 8, tn), lambda l, j: (l, 0, j)),
        compiler_params=_cparams(("parallel", "parallel")),
        name="ada_table",
    )(cin, w_ada, b_ada.reshape(depth, 1, n))


def _norm_mod(x, g, scale, shift):
    r = lax.rsqrt(jnp.mean(x * x, axis=-1, keepdims=True) + EPS)
    return ((x * r) * (g * (1.0 + scale)) + shift).astype(BF16)


def _in_kernel(x_ref, g_ref, sh_ref, sc_ref, w_ref, o_ref, h_ref, *, rows):
    @pl.when(pl.program_id(1) == 0)
    def _():
        for r in range(0, x_ref.shape[0], rows):
            h_ref[r:r + rows, :] = _norm_mod(x_ref[r:r + rows, :], g_ref[...], sc_ref[0], sh_ref[0])

    o_ref[...] = _dot(h_ref[...], w_ref[...])


def in_proj(x, g, mod, w, *, tile_mod, tm, tn=1024):
    n, d = x.shape
    cols = w.shape[1]
    assert n % tm == 0 and cols % tn == 0
    return pl.pallas_call(
        functools.partial(_in_kernel, rows=min(tm, 256)),
        out_shape=jax.ShapeDtypeStruct((n, cols), F32),
        grid=(n // tm, cols // tn),
        in_specs=[
            pl.BlockSpec((tm, d), lambda i, j: (i, 0)),
            pl.BlockSpec((1, d), lambda i, j: (0, 0)),
            pl.BlockSpec((1, 1, d), lambda i, j: (tile_mod(i) * N_MOD + 0, 0, 0)),
            pl.BlockSpec((1, 1, d), lambda i, j: (tile_mod(i) * N_MOD + 1, 0, 0)),
            pl.BlockSpec((d, tn), lambda i, j: (0, j)),
        ],
        out_specs=pl.BlockSpec((tm, tn), lambda i, j: (i, j)),
        scratch_shapes=[pltpu.VMEM((tm, d), BF16)],
        compiler_params=_cparams(("parallel", "arbitrary")),
        name="in_proj",
    )(x, g.reshape(1, d), mod, mod, w)


def _rope(y, cos2, sin2):
    lane = lax.broadcasted_iota(jnp.int32, y.shape, 1)
    swapped = jnp.where((lane & 1) == 0, pltpu.roll(y, LANE - 1, 1), pltpu.roll(y, 1, 1))
    return y * cos2 + swapped * sin2


def _qk_kernel(q_ref, kv_ref, cos_ref, sin_ref, qg_ref, kg_ref, qo_ref, ko_ref, vo_ref):
    cos2, sin2 = cos_ref[...], sin_ref[...]
    scale = HEAD_DIM ** -0.5
    for h in range(ATTN_HEADS):
        sl = slice(h * HEAD_DIM, (h + 1) * HEAD_DIM)
        y = _rms(q_ref[:, sl], qg_ref[...])
        qo_ref[:, sl] = (_rope(y, cos2, sin2) * scale).astype(BF16)
    for h in range(ATTN_KV_HEADS):
        sl = slice(h * HEAD_DIM, (h + 1) * HEAD_DIM)
        y = _rms(kv_ref[:, sl], kg_ref[...])
        ko_ref[:, sl] = _rope(y, cos2, sin2).astype(BF16)
    kvw = ATTN_KV_HEADS * HEAD_DIM
    vo_ref[...] = kv_ref[:, kvw:2 * kvw].astype(BF16)


def qk_prep(p, cos2, sin2, qg, kg, *, tile_rope, tm):
    n = p.shape[0]
    qw = ATTN_HEADS * HEAD_DIM
    kvw = ATTN_KV_HEADS * HEAD_DIM
    return pl.pallas_call(
        _qk_kernel,
        out_shape=(jax.ShapeDtypeStruct((n, qw), BF16),
                   jax.ShapeDtypeStruct((n, kvw), BF16),
                   jax.ShapeDtypeStruct((n, kvw), BF16)),
        grid=(n // tm,),
        in_specs=[
            pl.BlockSpec((tm, qw), lambda i: (i, 0)),
            pl.BlockSpec((tm, 2 * kvw), lambda i: (i, qw // (2 * kvw))),
            pl.BlockSpec((tm, HEAD_DIM), lambda i: (tile_rope(i), 0)),
            pl.BlockSpec((tm, HEAD_DIM), lambda i: (tile_rope(i), 0)),
            pl.BlockSpec((1, HEAD_DIM), lambda i: (0, 0)),
            pl.BlockSpec((1, HEAD_DIM), lambda i: (0, 0)),
        ],
        out_specs=(pl.BlockSpec((tm, qw), lambda i: (i, 0)),
                   pl.BlockSpec((tm, kvw), lambda i: (i, 0)),
                   pl.BlockSpec((tm, kvw), lambda i: (i, 0))),
        compiler_params=_cparams(("parallel",)),
        name="qk_prep",
    )(p, p, cos2, sin2, qg.reshape(1, HEAD_DIM), kg.reshape(1, HEAD_DIM))


def _softmax_pv(q, segs, ck):
    m = l = acc = None
    for k_ref, v_ref in segs:
        n = k_ref.shape[0]
        cs = min(ck, n)
        for c0 in range(0, n, cs):
            s = _dot_nt(q, k_ref[c0:c0 + cs, :])
            m_c = s.max(axis=-1, keepdims=True)
            m_new = m_c if m is None else jnp.maximum(m, m_c)
            p = jnp.exp(s - m_new)
            p_lanes = functools.reduce(jnp.add, [p[:, i:i + LANE] for i in range(0, cs, LANE)])
            pv = _dot(p.astype(BF16), v_ref[c0:c0 + cs, :])
            if m is None:
                l, acc = p_lanes, pv
            else:
                alpha = jnp.exp(m - m_new)
                l, acc = alpha * l + p_lanes, alpha * acc + pv
            m = m_new
    return acc / l.sum(axis=-1, keepdims=True)


def _attn_kernel(q_ref, kl_ref, kc_ref, vl_ref, vc_ref, o_ref, *, n_lat, ck):
    qi = pl.program_id(2)
    tq = q_ref.shape[0]

    def run(segs):
        hs = range(ATTN_GROUP)
        q = jnp.concatenate([q_ref[:, h * HEAD_DIM:(h + 1) * HEAD_DIM] for h in hs], axis=0)
        o = _softmax_pv(q, segs, ck)
        for h in hs:
            o_ref[:, h * HEAD_DIM:(h + 1) * HEAD_DIM] = o[h * tq:(h + 1) * tq].astype(BF16)

    @pl.when(qi < n_lat)
    def _():
        run([(kl_ref, vl_ref), (kc_ref, vc_ref)])

    @pl.when(qi >= n_lat)
    def _():
        run([(kc_ref, vc_ref)])


def attention(q, k, v, *, batch, t_lat, t_ctx, tq=256, ck=512):
    n = q.shape[0]
    n_lat = t_lat // tq
    n_ctx = t_ctx // tq
    gw = ATTN_GROUP * HEAD_DIM
    ctx0 = batch * t_lat

    def qmap(b, j, i):
        return (jnp.where(i < n_lat, b * n_lat + i, ctx0 // tq + b * n_ctx + (i - n_lat)), j)

    return pl.pallas_call(
        functools.partial(_attn_kernel, n_lat=n_lat, ck=ck),
        out_shape=jax.ShapeDtypeStruct((n, ATTN_HEADS * HEAD_DIM), BF16),
        grid=(batch, ATTN_KV_HEADS, n_lat + n_ctx),
        in_specs=[
            pl.BlockSpec((tq, gw), qmap),
            pl.BlockSpec((t_lat, HEAD_DIM), lambda b, j, i: (b, j)),
            pl.BlockSpec((t_ctx, HEAD_DIM), lambda b, j, i: (ctx0 // t_ctx + b, j)),
            pl.BlockSpec((t_lat, HEAD_DIM), lambda b, j, i: (b, j)),
            pl.BlockSpec((t_ctx, HEAD_DIM), lambda b, j, i: (ctx0 // t_ctx + b, j)),
        ],
        out_specs=pl.BlockSpec((tq, gw), qmap),
        compiler_params=_cparams(("parallel", "parallel", "arbitrary")),
        name="attention",
    )(q, k, k, v, v)


def _hgrn_chunk(hq, z, v, lb, st, *, reverse):
    c = hq.shape[0]
    sig = jax.nn.sigmoid(z)
    f = lb + (1.0 - lb) * sig
    logf = jnp.log(jnp.maximum(f, F_MIN))
    kk = (1.0 - lb) * jax.nn.sigmoid(-z)
    q = _silu(hq)

    row = lax.broadcasted_iota(jnp.int32, (c, c), 0)
    col = lax.broadcasted_iota(jnp.int32, (c, c), 1)
    rid = lax.broadcasted_iota(jnp.int32, (c, HG_D), 0)
    later = (row < col) if reverse else (row > col)

    a = jnp.where(row == col, _dot_nt(q.astype(BF16), kk.astype(BF16)), 0.0)
    cum, tot = logf, logf
    d, ld = 1, 0
    while d < c:
        second = (rid & d) != 0
        sel = jnp.logical_not(second) if reverse else second
        e = jnp.exp(jnp.where(sel, cum, tot - cum))
        ad = _dot_nt((q * e).astype(BF16), (kk * e).astype(BF16))
        a = a + jnp.where(jnp.logical_and(((row ^ col) >> ld) == 1, later), ad, 0.0)
        up = pltpu.roll(tot, d, 0)
        dn = pltpu.roll(tot, c - d, 0)
        prev, nxt = (dn, up) if reverse else (up, dn)
        cum = cum + jnp.where(sel, prev, 0.0)
        tot = tot + jnp.where(sel, prev, nxt)
        d, ld = d * 2, ld + 1

    qd = (q * jnp.exp(cum)).astype(BF16)
    kd = (kk * jnp.exp(tot - cum)).astype(BF16)
    vb = v.astype(BF16)
    o = _dot(a.astype(BF16), vb) + _dot_nt(qd, st.astype(BF16))
    st_new = st * jnp.exp(tot[0:1, :]) + _dot(vb.T, kd)
    return o, st_new


def _hgrn_kernel(*refs, reverse, finalize):
    if finalize:
        hq_ref, hf_ref, hi_ref, lb_ref, of_ref, gt_ref, g_ref, o_ref, st_ref = refs
    else:
        hq_ref, hf_ref, hi_ref, lb_ref, o_ref, st_ref = refs

    @pl.when(pl.program_id(1) == 0)
    def _():
        st_ref[...] = jnp.zeros_like(st_ref)

    starts = range(0, hq_ref.shape[0], HG_CHUNK)
    for r in (reversed(starts) if reverse else starts):
        rs = slice(r, r + HG_CHUNK)
        for h in range(HG_HEADS):
            sl = slice(h * HG_D, (h + 1) * HG_D)
            o, st_new = _hgrn_chunk(hq_ref[rs, sl], hf_ref[rs, sl], hi_ref[rs, sl], lb_ref[:, sl],
                                    st_ref[h], reverse=reverse)
            st_ref[h] = st_new
            if finalize:
                y = _rms(o + of_ref[rs, sl], g_ref[...])
                o_ref[rs, sl] = (y * _silu(gt_ref[rs, sl])).astype(o_ref.dtype)
            else:
                o_ref[rs, sl] = o


def hgrn_scan(p, lb, *, batch, t_lat, t_ctx, col0, reverse, fwd_out=None, gain=None,
              chunks_per_step=2):
    n = p.shape[0]
    c = HG_CHUNK * chunks_per_step
    assert t_lat % c == 0 and t_ctx % c == 0
    w = HG_HEADS * HG_D
    n_lat, n_ctx = t_lat // c, t_ctx // c
    ctx0 = batch * n_lat

    def rows(b, s):
        if reverse:
            return jnp.where(s < n_ctx, ctx0 + b * n_ctx + (n_ctx - 1 - s),
                             b * n_lat + (n_lat - 1 - (s - n_ctx)))
        return jnp.where(s < n_ctx, ctx0 + b * n_ctx + s, b * n_lat + (s - n_ctx))

    def spec(cb):
        return pl.BlockSpec((c, w), lambda b, s: (rows(b, s), cb))

    finalize = fwd_out is not None
    in_specs = [spec(col0), spec(col0 + (2 if reverse else 1)), spec(col0 + 3),
                pl.BlockSpec((1, w), lambda b, s: (0, 0))]
    args = [p, p, p, lb.reshape(1, w)]
    if finalize:
        in_specs += [spec(0), spec(col0 + 4), pl.BlockSpec((1, HG_D), lambda b, s: (0, 0))]
        args += [fwd_out, p, gain.reshape(1, HG_D)]
    return pl.pallas_call(
        functools.partial(_hgrn_kernel, reverse=reverse, finalize=finalize),
        out_shape=jax.ShapeDtypeStruct((n, w), BF16 if finalize else F32),
        grid=(batch, n_lat + n_ctx),
        in_specs=in_specs,
        out_specs=spec(0),
        scratch_shapes=[pltpu.VMEM((HG_HEADS, HG_D, HG_D), F32)],
        compiler_params=_cparams(("parallel", "arbitrary")),
        name="hgrn_bwd" if reverse else "hgrn_fwd",
    )(*args)


def _sg_kernel(u_ref, v_ref, g_ref, w_ref, b_ref, o_ref):
    for r in range(0, u_ref.shape[0], SG_CHUNK):
        rs = slice(r, r + SG_CHUNK)
        for g in range(SG_GROUPS):
            sl = slice(g * SG_DIM, (g + 1) * SG_DIM)
            vn = _rms(jax.nn.gelu(v_ref[rs, sl]), g_ref[:, sl])
            mixed = _dot(w_ref[g], vn.astype(BF16)) + b_ref[:, sl]
            o_ref[rs, sl] = (jax.nn.gelu(u_ref[rs, sl]) * mixed).astype(BF16)


def spatial_gate(p, g, w, bias_full, *, col0, tm):
    n = p.shape[0]
    sw = SG_GROUPS * SG_DIM
    return pl.pallas_call(
        _sg_kernel,
        out_shape=jax.ShapeDtypeStruct((n, sw), BF16),
        grid=(n // tm,),
        in_specs=[
            pl.BlockSpec((tm, sw), lambda i: (i, col0)),
            pl.BlockSpec((tm, sw), lambda i: (i, col0 + 1)),
            pl.BlockSpec((1, sw), lambda i: (0, 0)),
            pl.BlockSpec((SG_GROUPS, SG_CHUNK, SG_CHUNK), lambda i: (0, 0, 0)),
            pl.BlockSpec((SG_CHUNK, sw), lambda i: (0, 0)),
        ],
        out_specs=pl.BlockSpec((tm, sw), lambda i: (i, 0)),
        compiler_params=_cparams(("parallel",)),
        name="spatial_gate",
    )(p, p, g.reshape(1, sw), w, bias_full)


def _out_kernel(a_ref, h_ref, s_ref, wa_ref, wh_ref, ws_ref, x_ref, gate_ref, o_ref):
    acc = _dot(a_ref[...], wa_ref[...]) + _dot(h_ref[...], wh_ref[...]) + _dot(s_ref[...], ws_ref[...])
    o_ref[...] = x_ref[...] + gate_ref[0] * acc


def out_proj(attn, hg, sg, w, x, mod, *, tile_mod, tm, tn=1024):
    n, d = x.shape
    wa, wh, ws = attn.shape[1], hg.shape[1], sg.shape[1]
    assert wa % wh == 0 and wh == ws and n % tm == 0 and d % tn == 0
    return pl.pallas_call(
        _out_kernel,
        out_shape=jax.ShapeDtypeStruct((n, d), F32),
        grid=(n // tm, d // tn),
        in_specs=[
            pl.BlockSpec((tm, wa), lambda i, j: (i, 0)),
            pl.BlockSpec((tm, wh), lambda i, j: (i, 0)),
            pl.BlockSpec((tm, ws), lambda i, j: (i, 0)),
            pl.BlockSpec((wa, tn), lambda i, j: (0, j)),
            pl.BlockSpec((wh, tn), lambda i, j: (wa // wh, j)),
            pl.BlockSpec((ws, tn), lambda i, j: (wa // wh + 1, j)),
            pl.BlockSpec((tm, tn), lambda i, j: (i, j)),
            pl.BlockSpec((1, 1, tn), lambda i, j: (tile_mod(i) * N_MOD + 2, 0, j)),
        ],
        out_specs=pl.BlockSpec((tm, tn), lambda i, j: (i, j)),
        compiler_params=_cparams(("parallel", "parallel")),
        name="out_proj",
    )(attn, hg, sg, w, w, w, x, mod)


def _ffn_kernel(x_ref, xp_ref, xn_ref, g_ref, sh_ref, sc_ref, gate_ref,
                wg_ref, wv_ref, cwg_ref, cwv_ref, cbg_ref, cbv_ref, wd_ref,
                o_ref, h_ref, *, seq_of_tile, rows, chunk):
    i, j = pl.program_id(0), pl.program_id(1)
    tm = x_ref.shape[0]
    halo = xp_ref.shape[0]

    def norm_mod(x):
        return _norm_mod(x, g_ref[...], sc_ref[0], sh_ref[0])

    @pl.when(j == 0)
    def _():
        for r in range(0, tm, rows):
            h_ref[r:r + rows, :] = norm_mod(x_ref[r:r + rows, :])
        h_ref[tm:tm + halo, :] = norm_mod(xp_ref[...])
        h_ref[tm + halo:tm + 2 * halo, :] = norm_mod(xn_ref[...])
        o_ref[...] = jnp.zeros_like(o_ref)

    rid = lax.broadcasted_iota(jnp.int32, (tm, 1), 0)
    pos = (i * tm + rid) & (seq_of_tile(i) - 1)
    first, last = pos == 0, pos == seq_of_tile(i) - 1

    def conv(w_ref, cw_ref, cb_ref, cs):
        u = _dot(h_ref[0:tm, :], w_ref[:, cs])
        uh = _dot(h_ref[tm:tm + 2 * halo, :], w_ref[:, cs])
        before = jnp.where(rid == 0, uh[halo - 1:halo, :], pltpu.roll(u, 1, 0))
        after = jnp.where(rid == tm - 1, uh[halo:halo + 1, :], pltpu.roll(u, tm - 1, 0))
        before = jnp.where(first, 0.0, before)
        after = jnp.where(last, 0.0, after)
        return (cb_ref[:, cs] + before * cw_ref[0:1, cs] + u * cw_ref[1:2, cs]
                + after * cw_ref[2:3, cs])

    part = None
    for c0 in range(0, wd_ref.shape[0], chunk):
        cs = slice(c0, c0 + chunk)
        act = (_silu(conv(wg_ref, cwg_ref, cbg_ref, cs)) * conv(wv_ref, cwv_ref, cbv_ref, cs))
        p = _dot(act.astype(BF16), wd_ref[cs, :])
        part = p if part is None else part + p
    o_ref[...] += part

    @pl.when(j == pl.num_programs(1) - 1)
    def _():
        o_ref[...] = x_ref[...] + gate_ref[0] * o_ref[...]


def conv_ffn(x, g, mod, w_up, conv_w, conv_b, w_down, *, tile_mod, seq_of_tile, tm, tf=512):
    n, d = x.shape
    dff = w_down.shape[0]
    assert n % tm == 0 and dff % tf == 0
    nj = dff // tf
    halo = 8
    hb = tm // halo
    nhb = n // halo
    cb = conv_b.reshape(1, 2 * dff)
    return pl.pallas_call(
        functools.partial(_ffn_kernel, seq_of_tile=seq_of_tile, rows=min(tm, 256), chunk=256),
        out_shape=jax.ShapeDtypeStruct((n, d), F32),
        grid=(n // tm, nj),
        in_specs=[
            pl.BlockSpec((tm, d), lambda i, j: (i, 0)),
            pl.BlockSpec((halo, d), lambda i, j: (jnp.maximum(i * hb - 1, 0), 0)),
            pl.BlockSpec((halo, d), lambda i, j: (jnp.minimum((i + 1) * hb, nhb - 1), 0)),
            pl.BlockSpec((1, d), lambda i, j: (0, 0)),
            pl.BlockSpec((1, 1, d), lambda i, j: (tile_mod(i) * N_MOD + 3, 0, 0)),
            pl.BlockSpec((1, 1, d), lambda i, j: (tile_mod(i) * N_MOD + 4, 0, 0)),
            pl.BlockSpec((1, 1, d), lambda i, j: (tile_mod(i) * N_MOD + 5, 0, 0)),
            pl.BlockSpec((d, tf), lambda i, j: (0, j)),
            pl.BlockSpec((d, tf), lambda i, j: (0, nj + j)),
            pl.BlockSpec((CONV_W, tf), lambda i, j: (0, j)),
            pl.BlockSpec((CONV_W, tf), lambda i, j: (0, nj + j)),
            pl.BlockSpec((1, tf), lambda i, j: (0, j)),
            pl.BlockSpec((1, tf), lambda i, j: (0, nj + j)),
            pl.BlockSpec((tf, d), lambda i, j: (j, 0)),
        ],
        out_specs=pl.BlockSpec((tm, d), lambda i, j: (i, 0)),
        scratch_shapes=[pltpu.VMEM((tm + 2 * halo, d), BF16)],
        compiler_params=_cparams(("parallel", "arbitrary")),
        name="conv_ffn",
    )(x, x, x, g.reshape(1, d), mod, mod, mod, w_up, w_up, conv_w, conv_w, cb, cb, w_down)


def _norm_kernel(x_ref, g_ref, o_ref):
    o_ref[...] = _rms(x_ref[...], g_ref[...])


def final_norm(x, g, *, rows, tm=256):
    d = x.shape[1]
    return pl.pallas_call(
        _norm_kernel,
        out_shape=jax.ShapeDtypeStruct((rows, d), F32),
        grid=(rows // tm,),
        in_specs=[pl.BlockSpec((tm, d), lambda i: (i, 0)), pl.BlockSpec((1, d), lambda i: (0, 0))],
        out_specs=pl.BlockSpec((tm, d), lambda i: (i, 0)),
        compiler_params=_cparams(("parallel",)),
        name="final_norm",
    )(x, g.reshape(1, d))


def _rope_tables(t_lat, tm):
    rows = t_lat // GRID_W
    row = jnp.repeat(jnp.arange(rows, dtype=F32), GRID_W)
    col = jnp.tile(jnp.arange(GRID_W, dtype=F32), rows)
    n_freq = HEAD_DIM // 4
    inv = ROPE_THETA ** (-jnp.arange(n_freq, dtype=F32) / n_freq)
    ang = jnp.concatenate([row[:, None] * inv, col[:, None] * inv], axis=-1)
    cos2 = jnp.repeat(jnp.cos(ang), 2, axis=-1)
    sin2 = jnp.repeat(jnp.sin(ang), 2, axis=-1) * jnp.tile(jnp.array([-1.0, 1.0], F32), HEAD_DIM // 2)
    cos2 = jnp.concatenate([cos2, jnp.ones((tm, HEAD_DIM), F32)], axis=0)
    sin2 = jnp.concatenate([sin2, jnp.zeros((tm, HEAD_DIM), F32)], axis=0)
    return cos2, sin2


def _lower_bounds(lb_param):
    p = jax.nn.softmax(lb_param.astype(F32), axis=1)
    return jnp.cumsum(p, axis=1) - p[:, :1]


def kernel(x, c, ctx, c_ctx, w_ada, b_ada, norm1_g, w_in, q_norm_g, k_norm_g, hg_lower_bounds,
           hg_norm_g, sg_norm_g, sg_w, sg_b, w_out, norm2_g, w_up, conv_w, conv_b, w_down,
           final_norm_g):
    batch, t_lat, d = x.shape
    t_ctx = ctx.shape[1]
    depth = w_in.shape[0]
    n_lat, n_ctx = batch * t_lat, batch * t_ctx
    assert t_lat & (t_lat - 1) == 0 and t_ctx & (t_ctx - 1) == 0

    tm = min(1024, n_ctx)
    tf = tm
    assert t_lat % tm == 0 and n_ctx % tm == 0

    def tile_mod(rows):
        return lambda i: jnp.minimum(i // (t_lat // rows), batch)

    def seq_of_tile(i):
        return jnp.where(i < n_lat // tf, t_lat, t_ctx)

    def tile_rope(i):
        return jnp.where(i < n_lat // tm, i % (t_lat // tm), t_lat // tm)

    xs = jnp.concatenate([x.reshape(n_lat, d), ctx.reshape(n_ctx, d)], axis=0)
    cin = jnp.concatenate([c, c_ctx[None, :], jnp.zeros((8 - batch - 1, d), F32)], axis=0)
    mods = ada_table(cin, w_ada, b_ada)[:, :batch + 1, :]
    mods = mods.reshape(depth, (batch + 1) * N_MOD, 1, d)
    cos2, sin2 = _rope_tables(t_lat, tm)
    lbs = _lower_bounds(hg_lower_bounds)
    qw = ATTN_HEADS * HEAD_DIM
    kvw = ATTN_KV_HEADS * HEAD_DIM
    hg_col0 = (qw + 2 * kvw) // (HG_HEADS * HG_D)
    sg_col0 = hg_col0 + 5

    for l in range(depth):
        mod = mods[l]
        p = in_proj(xs, norm1_g[l], mod, w_in[l].astype(BF16), tile_mod=tile_mod(tm), tm=tm)
        q, k, v = qk_prep(p, cos2, sin2, q_norm_g[l], k_norm_g[l], tile_rope=tile_rope, tm=tm)
        attn = attention(q, k, v, batch=batch, t_lat=t_lat, t_ctx=t_ctx)
        o_f = hgrn_scan(p, lbs[0, l], batch=batch, t_lat=t_lat, t_ctx=t_ctx, col0=hg_col0,
                        reverse=False)
        hg = hgrn_scan(p, lbs[1, l], batch=batch, t_lat=t_lat, t_ctx=t_ctx, col0=hg_col0,
                       reverse=True, fwd_out=o_f, gain=hg_norm_g[l])
        bias_full = jnp.repeat(sg_b[l].T, SG_DIM, axis=1)
        sg = spatial_gate(p, sg_norm_g[l], sg_w[l].astype(BF16), bias_full, col0=sg_col0, tm=tm)
        xs = out_proj(attn, hg, sg, w_out[l].astype(BF16), xs, mod, tile_mod=tile_mod(tm), tm=tm)
        xs = conv_ffn(xs, norm2_g[l], mod, w_up[l].astype(BF16), conv_w[l], conv_b[l],
                      w_down[l].astype(BF16), tile_mod=tile_mod(tf), seq_of_tile=seq_of_tile, tm=tf)

    return final_norm(xs, final_norm_g, rows=n_lat).reshape(batch, t_lat, d)
```

```python
import functools

import jax
import jax.numpy as jnp
from jax import lax
from jax.experimental import pallas as pl
from jax.experimental.pallas import tpu as pltpu

F32 = jnp.float32
BF16 = jnp.bfloat16

EPS = 1e-6
F_MIN = 1e-30
N_MOD = 6
HEAD_DIM = 128
ATTN_HEADS = 8
ATTN_KV_HEADS = 2
ATTN_GROUP = ATTN_HEADS // ATTN_KV_HEADS
ROPE_THETA = 10000.0
GRID_W = 64
HG_HEADS = 4
HG_D = 128
SG_GROUPS = 4
SG_DIM = 128
SG_CHUNK = 128
CONV_W = 3
LANE = 128
HG_CHUNK = 128
VMEM_LIMIT = 62 * 1024 * 1024


def _cparams(sem):
    return pltpu.CompilerParams(dimension_semantics=sem, vmem_limit_bytes=VMEM_LIMIT)


def _dot(a, b):
    return jnp.dot(a, b, preferred_element_type=F32)


def _dot_nt(a, b):
    return lax.dot_general(a, b, (((1,), (1,)), ((), ())), preferred_element_type=F32)


def _rms(x, g):
    return x * lax.rsqrt(jnp.mean(x * x, axis=-1, keepdims=True) + EPS) * g


def _silu(x):
    return x * jax.nn.sigmoid(x)


def _ada_kernel(c_ref, w_ref, b_ref, o_ref):
    s = _silu(c_ref[...]).astype(BF16)
    o_ref[...] = _dot(s, w_ref[...].astype(BF16)) + b_ref[...]


def ada_table(cin, w_ada, b_ada, tn=1024):
    depth, d, n = w_ada.shape
    assert n % tn == 0
    return pl.pallas_call(
        _ada_kernel,
        out_shape=jax.ShapeDtypeStruct((depth, 8, n), F32),
        grid=(depth, n // tn),
        in_specs=[
            pl.BlockSpec((8, d), lambda l, j: (0, 0)),
            pl.BlockSpec((None, d, tn), lambda l, j: (l, 0, j)),
            pl.BlockSpec((None, 1, tn), lambda l, j: (l, 0, j)),
        ],
        out_specs=pl.BlockSpec((None, 8, tn), lambda l, j: (l, 0, j)),
        compiler_params=_cparams(("parallel", "parallel")),
        name="ada_table",
    )(cin, w_ada, b_ada.reshape(depth, 1, n))


def _norm_mod(x, g, scale, shift):
    r = lax.rsqrt(jnp.mean(x * x, axis=-1, keepdims=True) + EPS)
    return ((x * r) * (g * (1.0 + scale)) + shift).astype(BF16)


def _in_kernel(x_ref, g_ref, sh_ref, sc_ref, w_ref, o_ref, h_ref, *, rows):
    @pl.when(pl.program_id(1) == 0)
    def _():
        for r in range(0, x_ref.shape[0], rows):
            h_ref[r:r + rows, :] = _norm_mod(x_ref[r:r + rows, :], g_ref[...], sc_ref[0], sh_ref[0])

    o_ref[...] = _dot(h_ref[...], w_ref[...])


def in_proj(x, g, mod, w, *, tile_mod, tm, tn=1024):
    n, d = x.shape
    cols = w.shape[1]
    assert n % tm == 0 and cols % tn == 0
    return pl.pallas_call(
        functools.partial(_in_kernel, rows=min(tm, 256)),
        out_shape=jax.ShapeDtypeStruct((n, cols), F32),
        grid=(n // tm, cols // tn),
        in_specs=[
            pl.BlockSpec((tm, d), lambda i, j: (i, 0)),
            pl.BlockSpec((1, d), lambda i, j: (0, 0)),
            pl.BlockSpec((1, 1, d), lambda i, j: (tile_mod(i) * N_MOD + 0, 0, 0)),
            pl.BlockSpec((1, 1, d), lambda i, j: (tile_mod(i) * N_MOD + 1, 0, 0)),
            pl.BlockSpec((d, tn), lambda i, j: (0, j)),
        ],
        out_specs=pl.BlockSpec((tm, tn), lambda i, j: (i, j)),
        scratch_shapes=[pltpu.VMEM((tm, d), BF16)],
        compiler_params=_cparams(("parallel", "arbitrary")),
        name="in_proj",
    )(x, g.reshape(1, d), mod, mod, w)


def _rope(y, cos2, sin2):
    lane = lax.broadcasted_iota(jnp.int32, y.shape, 1)
    swapped = jnp.where((lane & 1) == 0, pltpu.roll(y, LANE - 1, 1), pltpu.roll(y, 1, 1))
    return y * cos2 + swapped * sin2


def _qk_kernel(q_ref, kv_ref, cos_ref, sin_ref, qg_ref, kg_ref, qo_ref, ko_ref, vo_ref):
    cos2, sin2 = cos_ref[...], sin_ref[...]
    scale = HEAD_DIM ** -0.5
    for h in range(ATTN_HEADS):
        sl = slice(h * HEAD_DIM, (h + 1) * HEAD_DIM)
        y = _rms(q_ref[:, sl], qg_ref[...])
        qo_ref[:, sl] = (_rope(y, cos2, sin2) * scale).astype(BF16)
    for h in range(ATTN_KV_HEADS):
        sl = slice(h * HEAD_DIM, (h + 1) * HEAD_DIM)
        y = _rms(kv_ref[:, sl], kg_ref[...])
        ko_ref[:, sl] = _rope(y, cos2, sin2).astype(BF16)
    kvw = ATTN_KV_HEADS * HEAD_DIM
    vo_ref[...] = kv_ref[:, kvw:2 * kvw].astype(BF16)


def qk_prep(p, cos2, sin2, qg, kg, *, tile_rope, tm):
    n = p.shape[0]
    qw = ATTN_HEADS * HEAD_DIM
    kvw = ATTN_KV_HEADS * HEAD_DIM
    return pl.pallas_call(
        _qk_kernel,
        out_shape=(jax.ShapeDtypeStruct((n, qw), BF16),
                   jax.ShapeDtypeStruct((n, kvw), BF16),
                   jax.ShapeDtypeStruct((n, kvw), BF16)),
        grid=(n // tm,),
        in_specs=[
            pl.BlockSpec((tm, qw), lambda i: (i, 0)),
            pl.BlockSpec((tm, 2 * kvw), lambda i: (i, qw // (2 * kvw))),
            pl.BlockSpec((tm, HEAD_DIM), lambda i: (tile_rope(i), 0)),
            pl.BlockSpec((tm, HEAD_DIM), lambda i: (tile_rope(i), 0)),
            pl.BlockSpec((1, HEAD_DIM), lambda i: (0, 0)),
            pl.BlockSpec((1, HEAD_DIM), lambda i: (0, 0)),
        ],
        out_specs=(pl.BlockSpec((tm, qw), lambda i: (i, 0)),
                   pl.BlockSpec((tm, kvw), lambda i: (i, 0)),
                   pl.BlockSpec((tm, kvw), lambda i: (i, 0))),
        compiler_params=_cparams(("parallel",)),
        name="qk_prep",
    )(p, p, cos2, sin2, qg.reshape(1, HEAD_DIM), kg.reshape(1, HEAD_DIM))


def _softmax_pv(q, segs, ck):
    m = l = acc = None
    for k_ref, v_ref in segs:
        n = k_ref.shape[0]
        cs = min(ck, n)
        for c0 in range(0, n, cs):
            s = _dot_nt(q, k_ref[c0:c0 + cs, :])
            m_c = s.max(axis=-1, keepdims=True)
            m_new = m_c if m is None else jnp.maximum(m, m_c)
            p = jnp.exp(s - m_new)
            p_lanes = functools.reduce(jnp.add, [p[:, i:i + LANE] for i in range(0, cs, LANE)])
            pv = _dot(p.astype(BF16), v_ref[c0:c0 + cs, :])
            if m is None:
                l, acc = p_lanes, pv
            else:
                alpha = jnp.exp(m - m_new)
                l, acc = alpha * l + p_lanes, alpha * acc + pv
            m = m_new
    return acc / l.sum(axis=-1, keepdims=True)


def _attn_kernel(q_ref, kl_ref, kc_ref, vl_ref, vc_ref, o_ref, *, n_lat, ck):
    qi = pl.program_id(2)
    tq = q_ref.shape[0]

    def run(segs):
        hs = range(ATTN_GROUP)
        q = jnp.concatenate([q_ref[:, h * HEAD_DIM:(h + 1) * HEAD_DIM] for h in hs], axis=0)
        o = _softmax_pv(q, segs, ck)
        for h in hs:
            o_ref[:, h * HEAD_DIM:(h + 1) * HEAD_DIM] = o[h * tq:(h + 1) * tq].astype(BF16)

    @pl.when(qi < n_lat)
    def _():
        run([(kl_ref, vl_ref), (kc_ref, vc_ref)])

    @pl.when(qi >= n_lat)
    def _():
        run([(kc_ref, vc_ref)])


def attention(q, k, v, *, batch, t_lat, t_ctx, tq=256, ck=512):
    n = q.shape[0]
    n_lat = t_lat // tq
    n_ctx = t_ctx // tq
    gw = ATTN_GROUP * HEAD_DIM
    ctx0 = batch * t_lat

    def qmap(b, j, i):
        return (jnp.where(i < n_lat, b * n_lat + i, ctx0 // tq + b * n_ctx + (i - n_lat)), j)

    return pl.pallas_call(
        functools.partial(_attn_kernel, n_lat=n_lat, ck=ck),
        out_shape=jax.ShapeDtypeStruct((n, ATTN_HEADS * HEAD_DIM), BF16),
        grid=(batch, ATTN_KV_HEADS, n_lat + n_ctx),
        in_specs=[
            pl.BlockSpec((tq, gw), qmap),
            pl.BlockSpec((t_lat, HEAD_DIM), lambda b, j, i: (b, j)),
            pl.BlockSpec((t_ctx, HEAD_DIM), lambda b, j, i: (ctx0 // t_ctx + b, j)),
            pl.BlockSpec((t_lat, HEAD_DIM), lambda b, j, i: (b, j)),
            pl.BlockSpec((t_ctx, HEAD_DIM), lambda b, j, i: (ctx0 // t_ctx + b, j)),
        ],
        out_specs=pl.BlockSpec((tq, gw), qmap),
        compiler_params=_cparams(("parallel", "parallel", "arbitrary")),
        name="attention",
    )(q, k, k, v, v)


def _hgrn_chunk(hq, z, v, lb, st, *, reverse):
    c = hq.shape[0]
    sig = jax.nn.sigmoid(z)
    f = lb + (1.0 - lb) * sig
    logf = jnp.log(jnp.maximum(f, F_MIN))
    kk = (1.0 - lb) * jax.nn.sigmoid(-z)
    q = _silu(hq)

    row = lax.broadcasted_iota(jnp.int32, (c, c), 0)
    col = lax.broadcasted_iota(jnp.int32, (c, c), 1)
    rid = lax.broadcasted_iota(jnp.int32, (c, HG_D), 0)
    later = (row < col) if reverse else (row > col)

    a = jnp.where(row == col, _dot_nt(q.astype(BF16), kk.astype(BF16)), 0.0)
    cum, tot = logf, logf
    d, ld = 1, 0
    while d < c:
        second = (rid & d) != 0
        sel = jnp.logical_not(second) if reverse else second
        e = jnp.exp(jnp.where(sel, cum, tot - cum))
        ad = _dot_nt((q * e).astype(BF16), (kk * e).astype(BF16))
        a = a + jnp.where(jnp.logical_and(((row ^ col) >> ld) == 1, later), ad, 0.0)
        up = pltpu.roll(tot, d, 0)
        dn = pltpu.roll(tot, c - d, 0)
        prev, nxt = (dn, up) if reverse else (up, dn)
        cum = cum + jnp.where(sel, prev, 0.0)
        tot = tot + jnp.where(sel, prev, nxt)
        d, ld = d * 2, ld + 1

    qd = (q * jnp.exp(cum)).astype(BF16)
    kd = (kk * jnp.exp(tot - cum)).astype(BF16)
    vb = v.astype(BF16)
    o = _dot(a.astype(BF16), vb) + _dot_nt(qd, st.astype(BF16))
    st_new = st * jnp.exp(tot[0:1, :]) + _dot(vb.T, kd)
    return o, st_new


def _hgrn_kernel(*refs, reverse, finalize):
    if finalize:
        hq_ref, hf_ref, hi_ref, lb_ref, of_ref, gt_ref, g_ref, o_ref, st_ref = refs
    else:
        hq_ref, hf_ref, hi_ref, lb_ref, o_ref, st_ref = refs

    @pl.when(pl.program_id(1) == 0)
    def _():
        st_ref[...] = jnp.zeros_like(st_ref)

    starts = range(0, hq_ref.shape[0], HG_CHUNK)
    for r in (reversed(starts) if reverse else starts):
        rs = slice(r, r + HG_CHUNK)
        for h in range(HG_HEADS):
            sl = slice(h * HG_D, (h + 1) * HG_D)
            o, st_new = _hgrn_chunk(hq_ref[rs, sl], hf_ref[rs, sl], hi_ref[rs, sl], lb_ref[:, sl],
                                    st_ref[h], reverse=reverse)
            st_ref[h] = st_new
            if finalize:
                y = _rms(o + of_ref[rs, sl], g_ref[...])
                o_ref[rs, sl] = (y * _silu(gt_ref[rs, sl])).astype(o_ref.dtype)
            else:
                o_ref[rs, sl] = o


def hgrn_scan(p, lb, *, batch, t_lat, t_ctx, col0, reverse, fwd_out=None, gain=None,
              chunks_per_step=2):
    n = p.shape[0]
    c = HG_CHUNK * chunks_per_step
    assert t_lat % c == 0 and t_ctx % c == 0
    w = HG_HEADS * HG_D
    n_lat, n_ctx = t_lat // c, t_ctx // c
    ctx0 = batch * n_lat

    def rows(b, s):
        if reverse:
            return jnp.where(s < n_ctx, ctx0 + b * n_ctx + (n_ctx - 1 - s),
                             b * n_lat + (n_lat - 1 - (s - n_ctx)))
        return jnp.where(s < n_ctx, ctx0 + b * n_ctx + s, b * n_lat + (s - n_ctx))

    def spec(cb):
        return pl.BlockSpec((c, w), lambda b, s: (rows(b, s), cb))

    finalize = fwd_out is not None
    in_specs = [spec(col0), spec(col0 + (2 if reverse else 1)), spec(col0 + 3),
                pl.BlockSpec((1, w), lambda b, s: (0, 0))]
    args = [p, p, p, lb.reshape(1, w)]
    if finalize:
        in_specs += [spec(0), spec(col0 + 4), pl.BlockSpec((1, HG_D), lambda b, s: (0, 0))]
        args += [fwd_out, p, gain.reshape(1, HG_D)]
    return pl.pallas_call(
        functools.partial(_hgrn_kernel, reverse=reverse, finalize=finalize),
        out_shape=jax.ShapeDtypeStruct((n, w), BF16 if finalize else F32),
        grid=(batch, n_lat + n_ctx),
        in_specs=in_specs,
        out_specs=spec(0),
        scratch_shapes=[pltpu.VMEM((HG_HEADS, HG_D, HG_D), F32)],
        compiler_params=_cparams(("parallel", "arbitrary")),
        name="hgrn_bwd" if reverse else "hgrn_fwd",
    )(*args)


def _sg_kernel(u_ref, v_ref, g_ref, w_ref, b_ref, o_ref):
    for r in range(0, u_ref.shape[0], SG_CHUNK):
        rs = slice(r, r + SG_CHUNK)
        for g in range(SG_GROUPS):
            sl = slice(g * SG_DIM, (g + 1) * SG_DIM)
            vn = _rms(jax.nn.gelu(v_ref[rs, sl]), g_ref[:, sl])
            mixed = _dot(w_ref[g], vn.astype(BF16)) + b_ref[:, sl]
            o_ref[rs, sl] = (jax.nn.gelu(u_ref[rs, sl]) * mixed).astype(BF16)


def spatial_gate(p, g, w, bias_full, *, col0, tm):
    n = p.shape[0]
    sw = SG_GROUPS * SG_DIM
    return pl.pallas_call(
        _sg_kernel,
        out_shape=jax.ShapeDtypeStruct((n, sw), BF16),
        grid=(n // tm,),
        in_specs=[
            pl.BlockSpec((tm, sw), lambda i: (i, col0)),
            pl.BlockSpec((tm, sw), lambda i: (i, col0 + 1)),
            pl.BlockSpec((1, sw), lambda i: (0, 0)),
            pl.BlockSpec((SG_GROUPS, SG_CHUNK, SG_CHUNK), lambda i: (0, 0, 0)),
            pl.BlockSpec((SG_CHUNK, sw), lambda i: (0, 0)),
        ],
        out_specs=pl.BlockSpec((tm, sw), lambda i: (i, 0)),
        compiler_params=_cparams(("parallel",)),
        name="spatial_gate",
    )(p, p, g.reshape(1, sw), w, bias_full)


def _out_kernel(a_ref, h_ref, s_ref, wa_ref, wh_ref, ws_ref, x_ref, gate_ref, o_ref):
    acc = _dot(a_ref[...], wa_ref[...]) + _dot(h_ref[...], wh_ref[...]) + _dot(s_ref[...], ws_ref[...])
    o_ref[...] = x_ref[...] + gate_ref[0] * acc


def out_proj(attn, hg, sg, w, x, mod, *, tile_mod, tm, tn=1024):
    n, d = x.shape
    wa, wh, ws = attn.shape[1], hg.shape[1], sg.shape[1]
    assert wa % wh == 0 and wh == ws and n % tm == 0 and d % tn == 0
    return pl.pallas_call(
        _out_kernel,
        out_shape=jax.ShapeDtypeStruct((n, d), F32),
        grid=(n // tm, d // tn),
        in_specs=[
            pl.BlockSpec((tm, wa), lambda i, j: (i, 0)),
            pl.BlockSpec((tm, wh), lambda i, j: (i, 0)),
            pl.BlockSpec((tm, ws), lambda i, j: (i, 0)),
            pl.BlockSpec((wa, tn), lambda i, j: (0, j)),
            pl.BlockSpec((wh, tn), lambda i, j: (wa // wh, j)),
            pl.BlockSpec((ws, tn), lambda i, j: (wa // wh + 1, j)),
            pl.BlockSpec((tm, tn), lambda i, j: (i, j)),
            pl.BlockSpec((1, 1, tn), lambda i, j: (tile_mod(i) * N_MOD + 2, 0, j)),
        ],
        out_specs=pl.BlockSpec((tm, tn), lambda i, j: (i, j)),
        compiler_params=_cparams(("parallel", "parallel")),
        name="out_proj",
    )(attn, hg, sg, w, w, w, x, mod)


def _ffn_kernel(x_ref, xp_ref, xn_ref, g_ref, sh_ref, sc_ref, gate_ref,
                wg_ref, wv_ref, cwg_ref, cwv_ref, cbg_ref, cbv_ref, wd_ref,
                o_ref, h_ref, u_ref, *, seq_of_tile, rows, chunk):
    i, j = pl.program_id(0), pl.program_id(1)
    tm = x_ref.shape[0]
    halo = xp_ref.shape[0]

    def norm_mod(x):
        return _norm_mod(x, g_ref[...], sc_ref[0], sh_ref[0])

    @pl.when(j == 0)
    def _():
        for r in range(0, tm, rows):
            h_ref[r:r + rows, :] = norm_mod(x_ref[r:r + rows, :])
        h_ref[tm:tm + halo, :] = norm_mod(xp_ref[...])
        h_ref[tm + halo:tm + 2 * halo, :] = norm_mod(xn_ref[...])
        o_ref[...] = jnp.zeros_like(o_ref)

    rid = lax.broadcasted_iota(jnp.int32, (tm, 1), 0)
    pos = (i * tm + rid) & (seq_of_tile(i) - 1)
    first, last = pos == 0, pos == seq_of_tile(i) - 1

    chunks = [slice(c0, c0 + chunk) for c0 in range(0, wd_ref.shape[0], chunk)]

    for c, cs in enumerate(chunks):
        u_ref[c, 0] = _dot(h_ref[...], wg_ref[:, cs])
        u_ref[c, 1] = _dot(h_ref[...], wv_ref[:, cs])

    def conv(c, k, cw_ref, cb_ref, cs):
        u = u_ref[c, k, 0:tm, :]
        before = jnp.where(rid == 0, u_ref[c, k, tm + halo - 1:tm + halo, :], pltpu.roll(u, 1, 0))
        after = jnp.where(rid == tm - 1, u_ref[c, k, tm + halo:tm + halo + 1, :],
                          pltpu.roll(u, tm - 1, 0))
        before = jnp.where(first, 0.0, before)
        after = jnp.where(last, 0.0, after)
        return (cb_ref[:, cs] + before * cw_ref[0:1, cs] + u * cw_ref[1:2, cs]
                + after * cw_ref[2:3, cs])

    for c, cs in enumerate(chunks):
        act = _silu(conv(c, 0, cwg_ref, cbg_ref, cs)) * conv(c, 1, cwv_ref, cbv_ref, cs)
        o_ref[...] += _dot(act.astype(BF16), wd_ref[cs, :])

    @pl.when(j == pl.num_programs(1) - 1)
    def _():
        o_ref[...] = x_ref[...] + gate_ref[0] * o_ref[...]


def conv_ffn(x, g, mod, w_up, conv_w, conv_b, w_down, *, tile_mod, seq_of_tile, tm, tf=512):
    n, d = x.shape
    dff = w_down.shape[0]
    assert n % tm == 0 and dff % tf == 0
    nj = dff // tf
    halo = 8
    chunk = min(256, tf)
    hb = tm // halo
    nhb = n // halo
    cb = conv_b.reshape(1, 2 * dff)
    return pl.pallas_call(
        functools.partial(_ffn_kernel, seq_of_tile=seq_of_tile, rows=min(tm, 256), chunk=chunk),
        out_shape=jax.ShapeDtypeStruct((n, d), F32),
        grid=(n // tm, nj),
        in_specs=[
            pl.BlockSpec((tm, d), lambda i, j: (i, 0)),
            pl.BlockSpec((halo, d), lambda i, j: (jnp.maximum(i * hb - 1, 0), 0)),
            pl.BlockSpec((halo, d), lambda i, j: (jnp.minimum((i + 1) * hb, nhb - 1), 0)),
            pl.BlockSpec((1, d), lambda i, j: (0, 0)),
            pl.BlockSpec((1, 1, d), lambda i, j: (tile_mod(i) * N_MOD + 3, 0, 0)),
            pl.BlockSpec((1, 1, d), lambda i, j: (tile_mod(i) * N_MOD + 4, 0, 0)),
            pl.BlockSpec((1, 1, d), lambda i, j: (tile_mod(i) * N_MOD + 5, 0, 0)),
            pl.BlockSpec((d, tf), lambda i, j: (0, j)),
            pl.BlockSpec((d, tf), lambda i, j: (0, nj + j)),
            pl.BlockSpec((CONV_W, tf), lambda i, j: (0, j)),
            pl.BlockSpec((CONV_W, tf), lambda i, j: (0, nj + j)),
            pl.BlockSpec((1, tf), lambda i, j: (0, j)),
            pl.BlockSpec((1, tf), lambda i, j: (0, nj + j)),
            pl.BlockSpec((tf, d), lambda i, j: (j, 0)),
        ],
        out_specs=pl.BlockSpec((tm, d), lambda i, j: (i, 0)),
        scratch_shapes=[pltpu.VMEM((tm + 2 * halo, d), BF16),
                        pltpu.VMEM((tf // chunk, 2, tm + 2 * halo, chunk), F32)],
        compiler_params=_cparams(("parallel", "arbitrary")),
        name="conv_ffn",
    )(x, x, x, g.reshape(1, d), mod, mod, mod, w_up, w_up, conv_w, conv_w, cb, cb, w_down)


def _norm_kernel(x_ref, g_ref, o_ref):
    o_ref[...] = _rms(x_ref[...], g_ref[...])


def final_norm(x, g, *, rows, tm=256):
    d = x.shape[1]
    return pl.pallas_call(
        _norm_kernel,
        out_shape=jax.ShapeDtypeStruct((rows, d), F32),
        grid=(rows // tm,),
        in_specs=[pl.BlockSpec((tm, d), lambda i: (i, 0)), pl.BlockSpec((1, d), lambda i: (0, 0))],
        out_specs=pl.BlockSpec((tm, d), lambda i: (i, 0)),
        compiler_params=_cparams(("parallel",)),
        name="final_norm",
    )(x, g.reshape(1, d))


def _rope_tables(t_lat, tm):
    rows = t_lat // GRID_W
    row = jnp.repeat(jnp.arange(rows, dtype=F32), GRID_W)
    col = jnp.tile(jnp.arange(GRID_W, dtype=F32), rows)
    n_freq = HEAD_DIM // 4
    inv = ROPE_THETA ** (-jnp.arange(n_freq, dtype=F32) / n_freq)
    ang = jnp.concatenate([row[:, None] * inv, col[:, None] * inv], axis=-1)
    cos2 = jnp.repeat(jnp.cos(ang), 2, axis=-1)
    sin2 = jnp.repeat(jnp.sin(ang), 2, axis=-1) * jnp.tile(jnp.array([-1.0, 1.0], F32), HEAD_DIM // 2)
    cos2 = jnp.concatenate([cos2, jnp.ones((tm, HEAD_DIM), F32)], axis=0)
    sin2 = jnp.concatenate([sin2, jnp.zeros((tm, HEAD_DIM), F32)], axis=0)
    return cos2, sin2


def _lower_bounds(lb_param):
    p = jax.nn.softmax(lb_param.astype(F32), axis=1)
    return jnp.cumsum(p, axis=1) - p[:, :1]


def kernel(x, c, ctx, c_ctx, w_ada, b_ada, norm1_g, w_in, q_norm_g, k_norm_g, hg_lower_bounds,
           hg_norm_g, sg_norm_g, sg_w, sg_b, w_out, norm2_g, w_up, conv_w, conv_b, w_down,
           final_norm_g):
    batch, t_lat, d = x.shape
    t_ctx = ctx.shape[1]
    depth = w_in.shape[0]
    n_lat, n_ctx = batch * t_lat, batch * t_ctx
    assert t_lat & (t_lat - 1) == 0 and t_ctx & (t_ctx - 1) == 0

    tm = min(1024, n_ctx)
    tf = tm
    assert t_lat % tm == 0 and n_ctx % tm == 0

    def tile_mod(rows):
        return lambda i: jnp.minimum(i // (t_lat // rows), batch)

    def seq_of_tile(i):
        return jnp.where(i < n_lat // tf, t_lat, t_ctx)

    def tile_rope(i):
        return jnp.where(i < n_lat // tm, i % (t_lat // tm), t_lat // tm)

    xs = jnp.concatenate([x.reshape(n_lat, d), ctx.reshape(n_ctx, d)], axis=0)
    cin = jnp.concatenate([c, c_ctx[None, :], jnp.zeros((8 - batch - 1, d), F32)], axis=0)
    mods = ada_table(cin, w_ada, b_ada)[:, :batch + 1, :]
    mods = mods.reshape(depth, (batch + 1) * N_MOD, 1, d)
    cos2, sin2 = _rope_tables(t_lat, tm)
    lbs = _lower_bounds(hg_lower_bounds)
    qw = ATTN_HEADS * HEAD_DIM
    kvw = ATTN_KV_HEADS * HEAD_DIM
    hg_col0 = (qw + 2 * kvw) // (HG_HEADS * HG_D)
    sg_col0 = hg_col0 + 5

    for l in range(depth):
        mod = mods[l]
        p = in_proj(xs, norm1_g[l], mod, w_in[l].astype(BF16), tile_mod=tile_mod(tm), tm=tm)
        q, k, v = qk_prep(p, cos2, sin2, q_norm_g[l], k_norm_g[l], tile_rope=tile_rope, tm=tm)
        attn = attention(q, k, v, batch=batch, t_lat=t_lat, t_ctx=t_ctx)
        o_f = hgrn_scan(p, lbs[0, l], batch=batch, t_lat=t_lat, t_ctx=t_ctx, col0=hg_col0,
                        reverse=False)
        hg = hgrn_scan(p, lbs[1, l], batch=batch, t_lat=t_lat, t_ctx=t_ctx, col0=hg_col0,
                       reverse=True, fwd_out=o_f, gain=hg_norm_g[l])
        bias_full = jnp.repeat(sg_b[l].T, SG_DIM, axis=1)
        sg = spatial_gate(p, sg_norm_g[l], sg_w[l].astype(BF16), bias_full, col0=sg_col0, tm=tm)
        xs = out_proj(attn, hg, sg, w_out[l].astype(BF16), xs, mod, tile_mod=tile_mod(tm), tm=tm)
        xs = conv_ffn(xs, norm2_g[l], mod, w_up[l].astype(BF16), conv_w[l], conv_b[l],
                      w_down[l].astype(BF16), tile_mod=tile_mod(tf), seq_of_tile=seq_of_tile, tm=tf)

    return final_norm(xs, final_norm_g, rows=n_lat).reshape(batch, t_lat, d)
```

```python
import functools

import jax
import jax.numpy as jnp
from jax import lax
from jax.experimental import pallas as pl
from jax.experimental.pallas import tpu as pltpu

F32 = jnp.float32
BF16 = jnp.bfloat16

EPS = 1e-6
F_MIN = 1e-30
N_MOD = 6
HEAD_DIM = 128
ATTN_HEADS = 8
ATTN_KV_HEADS = 2
ATTN_GROUP = ATTN_HEADS // ATTN_KV_HEADS
ROPE_THETA = 10000.0
GRID_W = 64
HG_HEADS = 4
HG_D = 128
SG_GROUPS = 4
SG_DIM = 128
SG_CHUNK = 128
CONV_W = 3
LANE = 128
SUBLANES = 8
LOG2_E = 1.4426950408889634
HG_CHUNK = 128
VMEM_LIMIT = 62 * 1024 * 1024


def _cparams(sem):
    return pltpu.CompilerParams(dimension_semantics=sem, vmem_limit_bytes=VMEM_LIMIT)


def _dot(a, b):
    return jnp.dot(a, b, preferred_element_type=F32)


def _dot_nt(a, b):
    return lax.dot_general(a, b, (((1,), (1,)), ((), ())), preferred_element_type=F32)


def _rms(x, g):
    return x * lax.rsqrt(jnp.mean(x * x, axis=-1, keepdims=True) + EPS) * g


def _silu(x):
    return x * jax.nn.sigmoid(x)


def _ada_kernel(c_ref, w_ref, b_ref, o_ref):
    s = _silu(c_ref[...]).astype(BF16)
    o_ref[...] = _dot(s, w_ref[...].astype(BF16)) + b_ref[...]


def ada_table(cin, w_ada, b_ada, tn=1024):
    depth, d, n = w_ada.shape
    assert n % tn == 0
    return pl.pallas_call(
        _ada_kernel,
        out_shape=jax.ShapeDtypeStruct((depth, 8, n), F32),
        grid=(depth, n // tn),
        in_specs=[
            pl.BlockSpec((8, d), lambda l, j: (0, 0)),
            pl.BlockSpec((None, d, tn), lambda l, j: (l, 0, j)),
            pl.BlockSpec((None, 1, tn), lambda l, j: (l, 0, j)),
        ],
        out_specs=pl.BlockSpec((None, 8, tn), lambda l, j: (l, 0, j)),
        compiler_params=_cparams(("parallel", "parallel")),
        name="ada_table",
    )(cin, w_ada, b_ada.reshape(depth, 1, n))


def _norm_mod(x, g, scale, shift):
    r = lax.rsqrt(jnp.mean(x * x, axis=-1, keepdims=True) + EPS)
    return ((x * r) * (g * (1.0 + scale)) + shift).astype(BF16)


def _in_kernel(x_ref, g_ref, sh_ref, sc_ref, w_ref, o_ref, h_ref, *, rows):
    @pl.when(pl.program_id(1) == 0)
    def _():
        for r in range(0, x_ref.shape[0], rows):
            h_ref[r:r + rows, :] = _norm_mod(x_ref[r:r + rows, :], g_ref[...], sc_ref[0], sh_ref[0])

    o_ref[...] = _dot(h_ref[...], w_ref[...])


def in_proj(x, g, mod, w, layer, *, tile_mod, tm, tn=1024):
    n, d = x.shape
    cols = w.shape[2]
    assert n % tm == 0 and cols % tn == 0
    return pl.pallas_call(
        functools.partial(_in_kernel, rows=min(tm, 256)),
        out_shape=jax.ShapeDtypeStruct((n, cols), F32),
        grid=(n // tm, cols // tn),
        in_specs=[
            pl.BlockSpec((tm, d), lambda i, j: (i, 0)),
            pl.BlockSpec((1, d), lambda i, j: (0, 0)),
            pl.BlockSpec((1, 1, d), lambda i, j: (tile_mod(i) * N_MOD + 0, 0, 0)),
            pl.BlockSpec((1, 1, d), lambda i, j: (tile_mod(i) * N_MOD + 1, 0, 0)),
            pl.BlockSpec((None, d, tn), lambda i, j: (layer, 0, j)),
        ],
        out_specs=pl.BlockSpec((tm, tn), lambda i, j: (i, j)),
        scratch_shapes=[pltpu.VMEM((tm, d), BF16)],
        compiler_params=_cparams(("parallel", "arbitrary")),
        name="in_proj",
    )(x, g.reshape(1, d), mod, mod, w)


def _rope(y, cos2, sin2):
    lane = lax.broadcasted_iota(jnp.int32, y.shape, 1)
    swapped = jnp.where((lane & 1) == 0, pltpu.roll(y, LANE - 1, 1), pltpu.roll(y, 1, 1))
    return y * cos2 + swapped * sin2


def _qk_kernel(q_ref, kv_ref, cos_ref, sin_ref, qg_ref, kg_ref, qo_ref, ko_ref, vo_ref):
    cos2, sin2 = cos_ref[...], sin_ref[...]
    scale = HEAD_DIM ** -0.5 * LOG2_E
    for h in range(ATTN_HEADS):
        sl = slice(h * HEAD_DIM, (h + 1) * HEAD_DIM)
        y = _rms(q_ref[:, sl], qg_ref[...])
        qo_ref[:, sl] = (_rope(y, cos2, sin2) * scale).astype(BF16)
    for h in range(ATTN_KV_HEADS):
        sl = slice(h * HEAD_DIM, (h + 1) * HEAD_DIM)
        y = _rms(kv_ref[:, sl], kg_ref[...])
        ko_ref[:, sl] = _rope(y, cos2, sin2).astype(BF16)
    kvw = ATTN_KV_HEADS * HEAD_DIM
    vo_ref[...] = kv_ref[:, kvw:2 * kvw].astype(BF16)


def qk_prep(p, cos2, sin2, qg, kg, *, tile_rope, tm):
    n = p.shape[0]
    qw = ATTN_HEADS * HEAD_DIM
    kvw = ATTN_KV_HEADS * HEAD_DIM
    return pl.pallas_call(
        _qk_kernel,
        out_shape=(jax.ShapeDtypeStruct((n, qw), BF16),
                   jax.ShapeDtypeStruct((n, kvw), BF16),
                   jax.ShapeDtypeStruct((n, kvw), BF16)),
        grid=(n // tm,),
        in_specs=[
            pl.BlockSpec((tm, qw), lambda i: (i, 0)),
            pl.BlockSpec((tm, 2 * kvw), lambda i: (i, qw // (2 * kvw))),
            pl.BlockSpec((tm, HEAD_DIM), lambda i: (tile_rope(i), 0)),
            pl.BlockSpec((tm, HEAD_DIM), lambda i: (tile_rope(i), 0)),
            pl.BlockSpec((1, HEAD_DIM), lambda i: (0, 0)),
            pl.BlockSpec((1, HEAD_DIM), lambda i: (0, 0)),
        ],
        out_specs=(pl.BlockSpec((tm, qw), lambda i: (i, 0)),
                   pl.BlockSpec((tm, kvw), lambda i: (i, 0)),
                   pl.BlockSpec((tm, kvw), lambda i: (i, 0))),
        compiler_params=_cparams(("parallel",)),
        name="qk_prep",
    )(p, p, cos2, sin2, qg.reshape(1, HEAD_DIM), kg.reshape(1, HEAD_DIM))


def _softmax_pv(q, segs, ck):
    m = l = acc = None
    for k_ref, v_ref in segs:
        n = k_ref.shape[0]
        cs = min(ck, n)
        for c0 in range(0, n, cs):
            s = _dot_nt(q, k_ref[c0:c0 + cs, :])
            m_c = s.max(axis=-1, keepdims=True)
            m_new = m_c if m is None else jnp.maximum(m, m_c)
            p = jnp.exp2(s - m_new)
            p_lanes = functools.reduce(jnp.add, [p[:, i:i + LANE] for i in range(0, cs, LANE)])
            pv = _dot(p.astype(BF16), v_ref[c0:c0 + cs, :])
            if m is None:
                l, acc = p_lanes, pv
            else:
                alpha = jnp.exp2(m - m_new)
                l, acc = alpha * l + p_lanes, alpha * acc + pv
            m = m_new
    return acc / l.sum(axis=-1, keepdims=True)


def _attn_kernel(q_ref, kl_ref, kc_ref, vl_ref, vc_ref, o_ref, *, n_lat, ck):
    qi = pl.program_id(2)
    tq = q_ref.shape[0]

    def run(segs):
        hs = range(ATTN_GROUP)
        q = jnp.concatenate([q_ref[:, h * HEAD_DIM:(h + 1) * HEAD_DIM] for h in hs], axis=0)
        o = _softmax_pv(q, segs, ck)
        for h in hs:
            o_ref[:, h * HEAD_DIM:(h + 1) * HEAD_DIM] = o[h * tq:(h + 1) * tq].astype(BF16)

    @pl.when(qi < n_lat)
    def _():
        run([(kl_ref, vl_ref), (kc_ref, vc_ref)])

    @pl.when(qi >= n_lat)
    def _():
        run([(kc_ref, vc_ref)])


def attention(q, k, v, *, batch, t_lat, t_ctx, tq=256, ck=512):
    n = q.shape[0]
    n_lat = t_lat // tq
    n_ctx = t_ctx // tq
    gw = ATTN_GROUP * HEAD_DIM
    ctx0 = batch * t_lat

    def qmap(b, j, i):
        return (jnp.where(i < n_lat, b * n_lat + i, ctx0 // tq + b * n_ctx + (i - n_lat)), j)

    return pl.pallas_call(
        functools.partial(_attn_kernel, n_lat=n_lat, ck=ck),
        out_shape=jax.ShapeDtypeStruct((n, ATTN_HEADS * HEAD_DIM), BF16),
        grid=(batch, ATTN_KV_HEADS, n_lat + n_ctx),
        in_specs=[
            pl.BlockSpec((tq, gw), qmap),
            pl.BlockSpec((t_lat, HEAD_DIM), lambda b, j, i: (b, j)),
            pl.BlockSpec((t_ctx, HEAD_DIM), lambda b, j, i: (ctx0 // t_ctx + b, j)),
            pl.BlockSpec((t_lat, HEAD_DIM), lambda b, j, i: (b, j)),
            pl.BlockSpec((t_ctx, HEAD_DIM), lambda b, j, i: (ctx0 // t_ctx + b, j)),
        ],
        out_specs=pl.BlockSpec((tq, gw), qmap),
        compiler_params=_cparams(("parallel", "parallel", "arbitrary")),
        name="attention",
    )(q, k, k, v, v)


def _hgrn_chunk(hq, z, v, lb, st, *, reverse):
    c = hq.shape[0]
    sig = jax.nn.sigmoid(z)
    f = lb + (1.0 - lb) * sig
    logf = jnp.log(jnp.maximum(f, F_MIN)) * (1.0 / jnp.log(2.0))
    kk = (1.0 - lb) * jax.nn.sigmoid(-z)
    q = _silu(hq)

    row = lax.broadcasted_iota(jnp.int32, (c, c), 0)
    col = lax.broadcasted_iota(jnp.int32, (c, c), 1)
    rid = lax.broadcasted_iota(jnp.int32, (c, HG_D), 0)
    later = (row < col) if reverse else (row > col)

    a = jnp.where(row == col, _dot_nt(q.astype(BF16), kk.astype(BF16)), 0.0)
    cum, tot = logf, logf
    d, ld = 1, 0
    while d < c:
        mask = jnp.logical_and(((row ^ col) >> ld) == 1, later)
        if d < SUBLANES:
            second = (rid & d) != 0
            sel = jnp.logical_not(second) if reverse else second
            x = jnp.where(sel, q, kk) * jnp.exp2(jnp.where(sel, cum, tot - cum))
            t3 = tot.reshape(c // SUBLANES, SUBLANES, HG_D)
            up = pltpu.roll(t3, d, 1).reshape(c, HG_D)
            partner = up if 2 * d == SUBLANES else jnp.where(
                second, up, pltpu.roll(t3, SUBLANES - d, 1).reshape(c, HG_D))
            cum = cum + jnp.where(sel, partner, 0.0)
            tot = tot + partner
        else:
            def split(y):
                y4 = y.reshape(c // (2 * d), 2, d, HG_D)
                return (y4[:, 1], y4[:, 0]) if reverse else (y4[:, 0], y4[:, 1])

            def join(early, late):
                pair = [late, early] if reverse else [early, late]
                return jnp.stack(pair, axis=1).reshape(c, HG_D)

            (cum_e, cum_l), (tot_e, tot_l) = split(cum), split(tot)
            x = join(split(kk)[0] * jnp.exp2(tot_e - cum_e), split(q)[1] * jnp.exp2(cum_l))
            both = tot_e + tot_l
            cum, tot = join(cum_e, cum_l + tot_e), join(both, both)
        xb = x.astype(BF16)
        a = jnp.where(mask, _dot_nt(xb, xb), a)
        d, ld = d * 2, ld + 1

    qd = (q * jnp.exp2(cum)).astype(BF16)
    kd = (kk * jnp.exp2(tot - cum)).astype(BF16)
    vb = v.astype(BF16)
    o = _dot(a.astype(BF16), vb) + _dot_nt(qd, st.astype(BF16))
    st_new = st * jnp.exp2(tot[0:1, :]) + _dot(vb.T, kd)
    return o, st_new


def _hgrn_kernel(*refs, reverse, finalize):
    if finalize:
        hq_ref, hf_ref, hi_ref, lb_ref, of_ref, gt_ref, g_ref, o_ref, st_ref = refs
    else:
        hq_ref, hf_ref, hi_ref, lb_ref, o_ref, st_ref = refs

    @pl.when(pl.program_id(1) == 0)
    def _():
        st_ref[...] = jnp.zeros_like(st_ref)

    starts = range(0, hq_ref.shape[0], HG_CHUNK)
    for r in (reversed(starts) if reverse else starts):
        rs = slice(r, r + HG_CHUNK)
        for h in range(HG_HEADS):
            sl = slice(h * HG_D, (h + 1) * HG_D)
            o, st_new = _hgrn_chunk(hq_ref[rs, sl], hf_ref[rs, sl], hi_ref[rs, sl], lb_ref[:, sl],
                                    st_ref[h], reverse=reverse)
            st_ref[h] = st_new
            if finalize:
                y = _rms(o + of_ref[rs, sl], g_ref[...])
                o_ref[rs, sl] = (y * _silu(gt_ref[rs, sl])).astype(o_ref.dtype)
            else:
                o_ref[rs, sl] = o


def hgrn_scan(p, lb, *, batch, t_lat, t_ctx, col0, reverse, fwd_out=None, gain=None,
              chunks_per_step=2):
    n = p.shape[0]
    c = HG_CHUNK * chunks_per_step
    assert t_lat % c == 0 and t_ctx % c == 0
    w = HG_HEADS * HG_D
    n_lat, n_ctx = t_lat // c, t_ctx // c
    ctx0 = batch * n_lat

    def rows(b, s):
        if reverse:
            return jnp.where(s < n_ctx, ctx0 + b * n_ctx + (n_ctx - 1 - s),
                             b * n_lat + (n_lat - 1 - (s - n_ctx)))
        return jnp.where(s < n_ctx, ctx0 + b * n_ctx + s, b * n_lat + (s - n_ctx))

    def spec(cb):
        return pl.BlockSpec((c, w), lambda b, s: (rows(b, s), cb))

    finalize = fwd_out is not None
    in_specs = [spec(col0), spec(col0 + (2 if reverse else 1)), spec(col0 + 3),
                pl.BlockSpec((1, w), lambda b, s: (0, 0))]
    args = [p, p, p, lb.reshape(1, w)]
    if finalize:
        in_specs += [spec(0), spec(col0 + 4), pl.BlockSpec((1, HG_D), lambda b, s: (0, 0))]
        args += [fwd_out, p, gain.reshape(1, HG_D)]
    return pl.pallas_call(
        functools.partial(_hgrn_kernel, reverse=reverse, finalize=finalize),
        out_shape=jax.ShapeDtypeStruct((n, w), BF16 if finalize else F32),
        grid=(batch, n_lat + n_ctx),
        in_specs=in_specs,
        out_specs=spec(0),
        scratch_shapes=[pltpu.VMEM((HG_HEADS, HG_D, HG_D), F32)],
        compiler_params=_cparams(("parallel", "arbitrary")),
        name="hgrn_bwd" if reverse else "hgrn_fwd",
    )(*args)


def _sg_kernel(u_ref, v_ref, g_ref, w_ref, b_ref, o_ref):
    for r in range(0, u_ref.shape[0], SG_CHUNK):
        rs = slice(r, r + SG_CHUNK)
        for g in range(SG_GROUPS):
            sl = slice(g * SG_DIM, (g + 1) * SG_DIM)
            vn = _rms(jax.nn.gelu(v_ref[rs, sl]), g_ref[:, sl])
            mixed = _dot(w_ref[g], vn.astype(BF16)) + b_ref[:, sl]
            o_ref[rs, sl] = (jax.nn.gelu(u_ref[rs, sl]) * mixed).astype(BF16)


def spatial_gate(p, g, w, bias_full, *, col0, tm):
    n = p.shape[0]
    sw = SG_GROUPS * SG_DIM
    return pl.pallas_call(
        _sg_kernel,
        out_shape=jax.ShapeDtypeStruct((n, sw), BF16),
        grid=(n // tm,),
        in_specs=[
            pl.BlockSpec((tm, sw), lambda i: (i, col0)),
            pl.BlockSpec((tm, sw), lambda i: (i, col0 + 1)),
            pl.BlockSpec((1, sw), lambda i: (0, 0)),
            pl.BlockSpec((SG_GROUPS, SG_CHUNK, SG_CHUNK), lambda i: (0, 0, 0)),
            pl.BlockSpec((SG_CHUNK, sw), lambda i: (0, 0)),
        ],
        out_specs=pl.BlockSpec((tm, sw), lambda i: (i, 0)),
        compiler_params=_cparams(("parallel",)),
        name="spatial_gate",
    )(p, p, g.reshape(1, sw), w, bias_full)


def _out_kernel(a_ref, h_ref, s_ref, wa_ref, wh_ref, ws_ref, x_ref, gate_ref, o_ref):
    acc = _dot(a_ref[...], wa_ref[...]) + _dot(h_ref[...], wh_ref[...]) + _dot(s_ref[...], ws_ref[...])
    o_ref[...] = x_ref[...] + gate_ref[0] * acc


def out_proj(attn, hg, sg, w, layer, x, mod, *, tile_mod, tm, tn=1024):
    n, d = x.shape
    wa, wh, ws = attn.shape[1], hg.shape[1], sg.shape[1]
    assert wa % wh == 0 and wh == ws and n % tm == 0 and d % tn == 0
    return pl.pallas_call(
        _out_kernel,
        out_shape=jax.ShapeDtypeStruct((n, d), F32),
        grid=(n // tm, d // tn),
        in_specs=[
            pl.BlockSpec((tm, wa), lambda i, j: (i, 0)),
            pl.BlockSpec((tm, wh), lambda i, j: (i, 0)),
            pl.BlockSpec((tm, ws), lambda i, j: (i, 0)),
            pl.BlockSpec((None, wa, tn), lambda i, j: (layer, 0, j)),
            pl.BlockSpec((None, wh, tn), lambda i, j: (layer, wa // wh, j)),
            pl.BlockSpec((None, ws, tn), lambda i, j: (layer, wa // wh + 1, j)),
            pl.BlockSpec((tm, tn), lambda i, j: (i, j)),
            pl.BlockSpec((1, 1, tn), lambda i, j: (tile_mod(i) * N_MOD + 2, 0, j)),
        ],
        out_specs=pl.BlockSpec((tm, tn), lambda i, j: (i, j)),
        compiler_params=_cparams(("parallel", "parallel")),
        name="out_proj",
    )(attn, hg, sg, w, w, w, x, mod)


def _ffn_kernel(x_ref, xp_ref, xn_ref, g_ref, sh_ref, sc_ref, gate_ref,
                wg_ref, wv_ref, cwg_ref, cwv_ref, cbg_ref, cbv_ref, wd_ref,
                o_ref, h_ref, u_ref, *, seq_of_tile, rows, chunk):
    i, j = pl.program_id(0), pl.program_id(1)
    tm = x_ref.shape[0]
    halo = xp_ref.shape[0]

    def norm_mod(x):
        return _norm_mod(x, g_ref[...], sc_ref[0], sh_ref[0])

    @pl.when(j == 0)
    def _():
        for r in range(0, tm, rows):
            h_ref[r:r + rows, :] = norm_mod(x_ref[r:r + rows, :])
        h_ref[tm:tm + halo, :] = norm_mod(xp_ref[...])
        h_ref[tm + halo:tm + 2 * halo, :] = norm_mod(xn_ref[...])
        o_ref[...] = jnp.zeros_like(o_ref)

    rid = lax.broadcasted_iota(jnp.int32, (tm, 1), 0)
    pos = (i * tm + rid) & (seq_of_tile(i) - 1)
    first, last = pos == 0, pos == seq_of_tile(i) - 1

    chunks = [slice(c0, c0 + chunk) for c0 in range(0, wd_ref.shape[0], chunk)]

    for c, cs in enumerate(chunks):
        u_ref[c, 0] = _dot(h_ref[...], wg_ref[:, cs])
        u_ref[c, 1] = _dot(h_ref[...], wv_ref[:, cs])

    def conv(c, k, cw_ref, cb_ref, cs):
        u = u_ref[c, k, 0:tm, :]
        before = jnp.where(rid == 0, u_ref[c, k, tm + halo - 1:tm + halo, :], pltpu.roll(u, 1, 0))
        after = jnp.where(rid == tm - 1, u_ref[c, k, tm + halo:tm + halo + 1, :],
                          pltpu.roll(u, tm - 1, 0))
        before = jnp.where(first, 0.0, before)
        after = jnp.where(last, 0.0, after)
        return (cb_ref[:, cs] + before * cw_ref[0:1, cs] + u * cw_ref[1:2, cs]
                + after * cw_ref[2:3, cs])

    for c, cs in enumerate(chunks):
        act = _silu(conv(c, 0, cwg_ref, cbg_ref, cs)) * conv(c, 1, cwv_ref, cbv_ref, cs)
        o_ref[...] += _dot(act.astype(BF16), wd_ref[cs, :])

    @pl.when(j == pl.num_programs(1) - 1)
    def _():
        o_ref[...] = x_ref[...] + gate_ref[0] * o_ref[...]


def conv_ffn(x, g, mod, w_up, conv_w, conv_b, w_down, layer, *, tile_mod, seq_of_tile, tm, tf=512):
    n, d = x.shape
    dff = w_down.shape[1]
    assert n % tm == 0 and dff % tf == 0
    nj = dff // tf
    halo = 8
    chunk = min(256, tf)
    hb = tm // halo
    nhb = n // halo
    cb = conv_b.reshape(1, 2 * dff)
    return pl.pallas_call(
        functools.partial(_ffn_kernel, seq_of_tile=seq_of_tile, rows=min(tm, 256), chunk=chunk),
        out_shape=jax.ShapeDtypeStruct((n, d), F32),
        grid=(n // tm, nj),
        in_specs=[
            pl.BlockSpec((tm, d), lambda i, j: (i, 0)),
            pl.BlockSpec((halo, d), lambda i, j: (jnp.maximum(i * hb - 1, 0), 0)),
            pl.BlockSpec((halo, d), lambda i, j: (jnp.minimum((i + 1) * hb, nhb - 1), 0)),
            pl.BlockSpec((1, d), lambda i, j: (0, 0)),
            pl.BlockSpec((1, 1, d), lambda i, j: (tile_mod(i) * N_MOD + 3, 0, 0)),
            pl.BlockSpec((1, 1, d), lambda i, j: (tile_mod(i) * N_MOD + 4, 0, 0)),
            pl.BlockSpec((1, 1, d), lambda i, j: (tile_mod(i) * N_MOD + 5, 0, 0)),
            pl.BlockSpec((None, d, tf), lambda i, j: (layer, 0, j)),
            pl.BlockSpec((None, d, tf), lambda i, j: (layer, 0, nj + j)),
            pl.BlockSpec((CONV_W, tf), lambda i, j: (0, j)),
            pl.BlockSpec((CONV_W, tf), lambda i, j: (0, nj + j)),
            pl.BlockSpec((1, tf), lambda i, j: (0, j)),
            pl.BlockSpec((1, tf), lambda i, j: (0, nj + j)),
            pl.BlockSpec((None, tf, d), lambda i, j: (layer, j, 0)),
        ],
        out_specs=pl.BlockSpec((tm, d), lambda i, j: (i, 0)),
        scratch_shapes=[pltpu.VMEM((tm + 2 * halo, d), BF16),
                        pltpu.VMEM((tf // chunk, 2, tm + 2 * halo, chunk), F32)],
        compiler_params=_cparams(("parallel", "arbitrary")),
        name="conv_ffn",
    )(x, x, x, g.reshape(1, d), mod, mod, mod, w_up, w_up, conv_w, conv_w, cb, cb, w_down)


def _norm_kernel(x_ref, g_ref, o_ref):
    o_ref[...] = _rms(x_ref[...], g_ref[...])


def final_norm(x, g, *, rows, tm=256):
    d = x.shape[1]
    return pl.pallas_call(
        _norm_kernel,
        out_shape=jax.ShapeDtypeStruct((rows, d), F32),
        grid=(rows // tm,),
        in_specs=[pl.BlockSpec((tm, d), lambda i: (i, 0)), pl.BlockSpec((1, d), lambda i: (0, 0))],
        out_specs=pl.BlockSpec((tm, d), lambda i: (i, 0)),
        compiler_params=_cparams(("parallel",)),
        name="final_norm",
    )(x, g.reshape(1, d))


def _rope_tables(t_lat, tm):
    rows = t_lat // GRID_W
    row = jnp.repeat(jnp.arange(rows, dtype=F32), GRID_W)
    col = jnp.tile(jnp.arange(GRID_W, dtype=F32), rows)
    n_freq = HEAD_DIM // 4
    inv = ROPE_THETA ** (-jnp.arange(n_freq, dtype=F32) / n_freq)
    ang = jnp.concatenate([row[:, None] * inv, col[:, None] * inv], axis=-1)
    cos2 = jnp.repeat(jnp.cos(ang), 2, axis=-1)
    sin2 = jnp.repeat(jnp.sin(ang), 2, axis=-1) * jnp.tile(jnp.array([-1.0, 1.0], F32), HEAD_DIM // 2)
    cos2 = jnp.concatenate([cos2, jnp.ones((tm, HEAD_DIM), F32)], axis=0)
    sin2 = jnp.concatenate([sin2, jnp.zeros((tm, HEAD_DIM), F32)], axis=0)
    return cos2, sin2


def _lower_bounds(lb_param):
    p = jax.nn.softmax(lb_param.astype(F32), axis=1)
    return jnp.cumsum(p, axis=1) - p[:, :1]


def kernel(x, c, ctx, c_ctx, w_ada, b_ada, norm1_g, w_in, q_norm_g, k_norm_g, hg_lower_bounds,
           hg_norm_g, sg_norm_g, sg_w, sg_b, w_out, norm2_g, w_up, conv_w, conv_b, w_down,
           final_norm_g):
    batch, t_lat, d = x.shape
    t_ctx = ctx.shape[1]
    depth = w_in.shape[0]
    n_lat, n_ctx = batch * t_lat, batch * t_ctx
    assert t_lat & (t_lat - 1) == 0 and t_ctx & (t_ctx - 1) == 0

    tm = min(1024, n_ctx)
    tf = tm
    assert t_lat % tm == 0 and n_ctx % tm == 0

    def tile_mod(rows):
        return lambda i: jnp.minimum(i // (t_lat // rows), batch)

    def seq_of_tile(i):
        return jnp.where(i < n_lat // tf, t_lat, t_ctx)

    def tile_rope(i):
        return jnp.where(i < n_lat // tm, i % (t_lat // tm), t_lat // tm)

    xs = jnp.concatenate([x.reshape(n_lat, d), ctx.reshape(n_ctx, d)], axis=0)
    cin = jnp.concatenate([c, c_ctx[None, :], jnp.zeros((8 - batch - 1, d), F32)], axis=0)
    mods = ada_table(cin, w_ada, b_ada)[:, :batch + 1, :]
    mods = mods.reshape(depth, (batch + 1) * N_MOD, 1, d)
    cos2, sin2 = _rope_tables(t_lat, tm)
    lbs = _lower_bounds(hg_lower_bounds)
    qw = ATTN_HEADS * HEAD_DIM
    kvw = ATTN_KV_HEADS * HEAD_DIM
    hg_col0 = (qw + 2 * kvw) // (HG_HEADS * HG_D)
    sg_col0 = hg_col0 + 5

    w_in, w_out, w_up, w_down = (w.astype(BF16) for w in (w_in, w_out, w_up, w_down))

    for l in range(depth):
        mod = mods[l]
        p = in_proj(xs, norm1_g[l], mod, w_in, l, tile_mod=tile_mod(tm), tm=tm)
        q, k, v = qk_prep(p, cos2, sin2, q_norm_g[l], k_norm_g[l], tile_rope=tile_rope, tm=tm)
        attn = attention(q, k, v, batch=batch, t_lat=t_lat, t_ctx=t_ctx)
        o_f = hgrn_scan(p, lbs[0, l], batch=batch, t_lat=t_lat, t_ctx=t_ctx, col0=hg_col0,
                        reverse=False)
        hg = hgrn_scan(p, lbs[1, l], batch=batch, t_lat=t_lat, t_ctx=t_ctx, col0=hg_col0,
                       reverse=True, fwd_out=o_f, gain=hg_norm_g[l])
        bias_full = jnp.repeat(sg_b[l].T, SG_DIM, axis=1)
        sg = spatial_gate(p, sg_norm_g[l], sg_w[l].astype(BF16), bias_full, col0=sg_col0, tm=tm)
        xs = out_proj(attn, hg, sg, w_out, l, xs, mod, tile_mod=tile_mod(tm), tm=tm)
        xs = conv_ffn(xs, norm2_g[l], mod, w_up, conv_w[l], conv_b[l], w_down, l,
                      tile_mod=tile_mod(tf), seq_of_tile=seq_of_tile, tm=tf)

    return final_norm(xs, final_norm_g, rows=n_lat).reshape(batch, t_lat, d)
```

```python
import functools

import jax
import jax.numpy as jnp
from jax import lax
from jax.experimental import pallas as pl
from jax.experimental.pallas import tpu as pltpu

F32 = jnp.float32
BF16 = jnp.bfloat16

EPS = 1e-6
F_MIN = 1e-30
N_MOD = 6
HEAD_DIM = 128
ATTN_HEADS = 8
ATTN_KV_HEADS = 2
ATTN_GROUP = ATTN_HEADS // ATTN_KV_HEADS
ROPE_THETA = 10000.0
GRID_W = 64
HG_HEADS = 4
HG_D = 128
SG_GROUPS = 4
SG_DIM = 128
SG_CHUNK = 128
CONV_W = 3
LANE = 128
SUBLANES = 8
LOG2_E = 1.4426950408889634
HG_CHUNK = 128
VMEM_LIMIT = 62 * 1024 * 1024


def _cparams(sem):
    return pltpu.CompilerParams(dimension_semantics=sem, vmem_limit_bytes=VMEM_LIMIT)


def _dot(a, b):
    return jnp.dot(a, b, preferred_element_type=F32)


def _dot_nt(a, b):
    return lax.dot_general(a, b, (((1,), (1,)), ((), ())), preferred_element_type=F32)


def _rms(x, g):
    return x * lax.rsqrt(jnp.mean(x * x, axis=-1, keepdims=True) + EPS) * g


def _silu(x):
    return x * jax.nn.sigmoid(x)


def _ada_kernel(c_ref, w_ref, b_ref, o_ref):
    s = _silu(c_ref[...]).astype(BF16)
    o_ref[...] = _dot(s, w_ref[...].astype(BF16)) + b_ref[...]


def ada_table(cin, w_ada, b_ada, tn=1024):
    depth, d, n = w_ada.shape
    assert n % tn == 0
    return pl.pallas_call(
        _ada_kernel,
        out_shape=jax.ShapeDtypeStruct((depth, 8, n), F32),
        grid=(depth, n // tn),
        in_specs=[
            pl.BlockSpec((8, d), lambda l, j: (0, 0)),
            pl.BlockSpec((None, d, tn), lambda l, j: (l, 0, j)),
            pl.BlockSpec((None, 1, tn), lambda l, j: (l, 0, j)),
        ],
        out_specs=pl.BlockSpec((None, 8, tn), lambda l, j: (l, 0, j)),
        compiler_params=_cparams(("parallel", "parallel")),
        name="ada_table",
    )(cin, w_ada, b_ada.reshape(depth, 1, n))


def _norm_mod(x, g, scale, shift):
    r = lax.rsqrt(jnp.mean(x * x, axis=-1, keepdims=True) + EPS)
    return ((x * r) * (g * (1.0 + scale)) + shift).astype(BF16)


def _in_kernel(x_ref, g_ref, sh_ref, sc_ref, w_ref, o_ref, h_ref, *, rows):
    @pl.when(pl.program_id(1) == 0)
    def _():
        for r in range(0, x_ref.shape[0], rows):
            h_ref[r:r + rows, :] = _norm_mod(x_ref[r:r + rows, :], g_ref[...], sc_ref[0], sh_ref[0])

    o_ref[...] = _dot(h_ref[...], w_ref[...])


def in_proj(x, g, mod, w, layer, *, tile_mod, tm, tn=1024):
    n, d = x.shape
    cols = w.shape[2]
    assert n % tm == 0 and cols % tn == 0
    return pl.pallas_call(
        functools.partial(_in_kernel, rows=min(tm, 256)),
        out_shape=jax.ShapeDtypeStruct((n, cols), F32),
        grid=(n // tm, cols // tn),
        in_specs=[
            pl.BlockSpec((tm, d), lambda i, j: (i, 0)),
            pl.BlockSpec((1, d), lambda i, j: (0, 0)),
            pl.BlockSpec((1, 1, d), lambda i, j: (tile_mod(i) * N_MOD + 0, 0, 0)),
            pl.BlockSpec((1, 1, d), lambda i, j: (tile_mod(i) * N_MOD + 1, 0, 0)),
            pl.BlockSpec((None, d, tn), lambda i, j: (layer, 0, j)),
        ],
        out_specs=pl.BlockSpec((tm, tn), lambda i, j: (i, j)),
        scratch_shapes=[pltpu.VMEM((tm, d), BF16)],
        compiler_params=_cparams(("parallel", "arbitrary")),
        name="in_proj",
    )(x, g.reshape(1, d), mod, mod, w)


def _rope(y, cos2, sin2):
    lane = lax.broadcasted_iota(jnp.int32, y.shape, 1)
    swapped = jnp.where((lane & 1) == 0, pltpu.roll(y, LANE - 1, 1), pltpu.roll(y, 1, 1))
    return y * cos2 + swapped * sin2


def _qk_kernel(q_ref, kv_ref, cos_ref, sin_ref, qg_ref, kg_ref, qo_ref, ko_ref, vo_ref):
    cos2, sin2 = cos_ref[...], sin_ref[...]
    scale = HEAD_DIM ** -0.5 * LOG2_E
    for h in range(ATTN_HEADS):
        sl = slice(h * HEAD_DIM, (h + 1) * HEAD_DIM)
        y = _rms(q_ref[:, sl], qg_ref[...])
        qo_ref[:, sl] = (_rope(y, cos2, sin2) * scale).astype(BF16)
    for h in range(ATTN_KV_HEADS):
        sl = slice(h * HEAD_DIM, (h + 1) * HEAD_DIM)
        y = _rms(kv_ref[:, sl], kg_ref[...])
        ko_ref[:, sl] = _rope(y, cos2, sin2).astype(BF16)
    kvw = ATTN_KV_HEADS * HEAD_DIM
    vo_ref[...] = kv_ref[:, kvw:2 * kvw].astype(BF16)


def qk_prep(p, cos2, sin2, qg, kg, *, tile_rope, tm):
    n = p.shape[0]
    qw = ATTN_HEADS * HEAD_DIM
    kvw = ATTN_KV_HEADS * HEAD_DIM
    return pl.pallas_call(
        _qk_kernel,
        out_shape=(jax.ShapeDtypeStruct((n, qw), BF16),
                   jax.ShapeDtypeStruct((n, kvw), BF16),
                   jax.ShapeDtypeStruct((n, kvw), BF16)),
        grid=(n // tm,),
        in_specs=[
            pl.BlockSpec((tm, qw), lambda i: (i, 0)),
            pl.BlockSpec((tm, 2 * kvw), lambda i: (i, qw // (2 * kvw))),
            pl.BlockSpec((tm, HEAD_DIM), lambda i: (tile_rope(i), 0)),
            pl.BlockSpec((tm, HEAD_DIM), lambda i: (tile_rope(i), 0)),
            pl.BlockSpec((1, HEAD_DIM), lambda i: (0, 0)),
            pl.BlockSpec((1, HEAD_DIM), lambda i: (0, 0)),
        ],
        out_specs=(pl.BlockSpec((tm, qw), lambda i: (i, 0)),
                   pl.BlockSpec((tm, kvw), lambda i: (i, 0)),
                   pl.BlockSpec((tm, kvw), lambda i: (i, 0))),
        compiler_params=_cparams(("parallel",)),
        name="qk_prep",
    )(p, p, cos2, sin2, qg.reshape(1, HEAD_DIM), kg.reshape(1, HEAD_DIM))


def _softmax_pv(q, segs, ck):
    m = l = acc = None
    for k_ref, v_ref in segs:
        n = k_ref.shape[0]
        cs = min(ck, n)
        for c0 in range(0, n, cs):
            s = _dot_nt(q, k_ref[c0:c0 + cs, :])
            m_c = s.max(axis=-1, keepdims=True)
            m_new = m_c if m is None else jnp.maximum(m, m_c)
            p = jnp.exp2(s - m_new)
            p_lanes = functools.reduce(jnp.add, [p[:, i:i + LANE] for i in range(0, cs, LANE)])
            pv = _dot(p.astype(BF16), v_ref[c0:c0 + cs, :])
            if m is None:
                l, acc = p_lanes, pv
            else:
                alpha = jnp.exp2(m - m_new)
                l, acc = alpha * l + p_lanes, alpha * acc + pv
            m = m_new
    return acc / l.sum(axis=-1, keepdims=True)


def _attn_kernel(q_ref, kl_ref, kc_ref, vl_ref, vc_ref, o_ref, *, n_lat, ck):
    qi = pl.program_id(2)
    tq = q_ref.shape[0]

    def run(segs):
        hs = range(ATTN_GROUP)
        q = jnp.concatenate([q_ref[:, h * HEAD_DIM:(h + 1) * HEAD_DIM] for h in hs], axis=0)
        o = _softmax_pv(q, segs, ck)
        for h in hs:
            o_ref[:, h * HEAD_DIM:(h + 1) * HEAD_DIM] = o[h * tq:(h + 1) * tq].astype(BF16)

    @pl.when(qi < n_lat)
    def _():
        run([(kl_ref, vl_ref), (kc_ref, vc_ref)])

    @pl.when(qi >= n_lat)
    def _():
        run([(kc_ref, vc_ref)])


def attention(q, k, v, *, batch, t_lat, t_ctx, ctx_queries=True, tq=256, ck=512):
    n = q.shape[0]
    n_lat = t_lat // tq
    n_ctx = t_ctx // tq if ctx_queries else 0
    gw = ATTN_GROUP * HEAD_DIM
    ctx0 = batch * t_lat

    def qmap(b, j, i):
        return (jnp.where(i < n_lat, b * n_lat + i, ctx0 // tq + b * (t_ctx // tq) + (i - n_lat)), j)

    return pl.pallas_call(
        functools.partial(_attn_kernel, n_lat=n_lat, ck=ck),
        out_shape=jax.ShapeDtypeStruct((n, ATTN_HEADS * HEAD_DIM), BF16),
        grid=(batch, ATTN_KV_HEADS, n_lat + n_ctx),
        in_specs=[
            pl.BlockSpec((tq, gw), qmap),
            pl.BlockSpec((t_lat, HEAD_DIM), lambda b, j, i: (b, j)),
            pl.BlockSpec((t_ctx, HEAD_DIM), lambda b, j, i: (ctx0 // t_ctx + b, j)),
            pl.BlockSpec((t_lat, HEAD_DIM), lambda b, j, i: (b, j)),
            pl.BlockSpec((t_ctx, HEAD_DIM), lambda b, j, i: (ctx0 // t_ctx + b, j)),
        ],
        out_specs=pl.BlockSpec((tq, gw), qmap),
        compiler_params=_cparams(("parallel", "parallel", "arbitrary")),
        name="attention",
    )(q, k, k, v, v)


def _hgrn_chunk(hq, z, v, lb, st, *, reverse):
    c = hq.shape[0]
    sig = jax.nn.sigmoid(z)
    f = lb + (1.0 - lb) * sig
    logf = jnp.log(jnp.maximum(f, F_MIN)) * (1.0 / jnp.log(2.0))
    kk = (1.0 - lb) * jax.nn.sigmoid(-z)
    q = _silu(hq)

    row = lax.broadcasted_iota(jnp.int32, (c, c), 0)
    col = lax.broadcasted_iota(jnp.int32, (c, c), 1)
    rid = lax.broadcasted_iota(jnp.int32, (c, HG_D), 0)
    later = (row < col) if reverse else (row > col)

    a = jnp.where(row == col, _dot_nt(q.astype(BF16), kk.astype(BF16)), 0.0)
    cum, tot = logf, logf
    d, ld = 1, 0
    while d < c:
        mask = jnp.logical_and(((row ^ col) >> ld) == 1, later)
        if d < SUBLANES:
            second = (rid & d) != 0
            sel = jnp.logical_not(second) if reverse else second
            x = jnp.where(sel, q, kk) * jnp.exp2(jnp.where(sel, cum, tot - cum))
            t3 = tot.reshape(c // SUBLANES, SUBLANES, HG_D)
            up = pltpu.roll(t3, d, 1).reshape(c, HG_D)
            partner = up if 2 * d == SUBLANES else jnp.where(
                second, up, pltpu.roll(t3, SUBLANES - d, 1).reshape(c, HG_D))
            cum = cum + jnp.where(sel, partner, 0.0)
            tot = tot + partner
        else:
            def split(y):
                y4 = y.reshape(c // (2 * d), 2, d, HG_D)
                return (y4[:, 1], y4[:, 0]) if reverse else (y4[:, 0], y4[:, 1])

            def join(early, late):
                pair = [late, early] if reverse else [early, late]
                return jnp.stack(pair, axis=1).reshape(c, HG_D)

            (cum_e, cum_l), (tot_e, tot_l) = split(cum), split(tot)
            x = join(split(kk)[0] * jnp.exp2(tot_e - cum_e), split(q)[1] * jnp.exp2(cum_l))
            both = tot_e + tot_l
            cum, tot = join(cum_e, cum_l + tot_e), join(both, both)
        xb = x.astype(BF16)
        a = jnp.where(mask, _dot_nt(xb, xb), a)
        d, ld = d * 2, ld + 1

    qd = (q * jnp.exp2(cum)).astype(BF16)
    kd = (kk * jnp.exp2(tot - cum)).astype(BF16)
    vb = v.astype(BF16)
    o = _dot(a.astype(BF16), vb) + _dot_nt(qd, st.astype(BF16))
    st_new = st * jnp.exp2(tot[0:1, :]) + _dot(vb.T, kd)
    return o, st_new


def _hgrn_kernel(*refs, reverse, finalize):
    if finalize:
        hq_ref, hf_ref, hi_ref, lb_ref, of_ref, gt_ref, g_ref, o_ref, st_ref = refs
    else:
        hq_ref, hf_ref, hi_ref, lb_ref, o_ref, st_ref = refs

    @pl.when(pl.program_id(1) == 0)
    def _():
        st_ref[...] = jnp.zeros_like(st_ref)

    starts = range(0, hq_ref.shape[0], HG_CHUNK)
    for r in (reversed(starts) if reverse else starts):
        rs = slice(r, r + HG_CHUNK)
        for h in range(HG_HEADS):
            sl = slice(h * HG_D, (h + 1) * HG_D)
            o, st_new = _hgrn_chunk(hq_ref[rs, sl], hf_ref[rs, sl], hi_ref[rs, sl], lb_ref[:, sl],
                                    st_ref[h], reverse=reverse)
            st_ref[h] = st_new
            if finalize:
                y = _rms(o + of_ref[rs, sl], g_ref[...])
                o_ref[rs, sl] = (y * _silu(gt_ref[rs, sl])).astype(o_ref.dtype)
            else:
                o_ref[rs, sl] = o


def hgrn_scan(p, lb, *, batch, t_lat, t_ctx, col0, reverse, fwd_out=None, gain=None,
              chunks_per_step=2):
    n = p.shape[0]
    c = HG_CHUNK * chunks_per_step
    assert t_lat % c == 0 and t_ctx % c == 0
    w = HG_HEADS * HG_D
    n_lat, n_ctx = t_lat // c, t_ctx // c
    ctx0 = batch * n_lat

    def rows(b, s):
        if reverse:
            return jnp.where(s < n_ctx, ctx0 + b * n_ctx + (n_ctx - 1 - s),
                             b * n_lat + (n_lat - 1 - (s - n_ctx)))
        return jnp.where(s < n_ctx, ctx0 + b * n_ctx + s, b * n_lat + (s - n_ctx))

    def spec(cb):
        return pl.BlockSpec((c, w), lambda b, s: (rows(b, s), cb))

    finalize = fwd_out is not None
    in_specs = [spec(col0), spec(col0 + (2 if reverse else 1)), spec(col0 + 3),
                pl.BlockSpec((1, w), lambda b, s: (0, 0))]
    args = [p, p, p, lb.reshape(1, w)]
    if finalize:
        in_specs += [spec(0), spec(col0 + 4), pl.BlockSpec((1, HG_D), lambda b, s: (0, 0))]
        args += [fwd_out, p, gain.reshape(1, HG_D)]
    return pl.pallas_call(
        functools.partial(_hgrn_kernel, reverse=reverse, finalize=finalize),
        out_shape=jax.ShapeDtypeStruct((n, w), BF16 if finalize else F32),
        grid=(batch, n_lat + n_ctx),
        in_specs=in_specs,
        out_specs=spec(0),
        scratch_shapes=[pltpu.VMEM((HG_HEADS, HG_D, HG_D), F32)],
        compiler_params=_cparams(("parallel", "arbitrary")),
        name="hgrn_bwd" if reverse else "hgrn_fwd",
    )(*args)


def _sg_kernel(u_ref, v_ref, g_ref, w_ref, b_ref, o_ref):
    for r in range(0, u_ref.shape[0], SG_CHUNK):
        rs = slice(r, r + SG_CHUNK)
        for g in range(SG_GROUPS):
            sl = slice(g * SG_DIM, (g + 1) * SG_DIM)
            vn = _rms(jax.nn.gelu(v_ref[rs, sl]), g_ref[:, sl])
            mixed = _dot(w_ref[g], vn.astype(BF16)) + b_ref[:, sl]
            o_ref[rs, sl] = (jax.nn.gelu(u_ref[rs, sl]) * mixed).astype(BF16)


def spatial_gate(p, g, w, bias_full, *, col0, n_rows, tm):
    sw = SG_GROUPS * SG_DIM
    assert n_rows % tm == 0
    return pl.pallas_call(
        _sg_kernel,
        out_shape=jax.ShapeDtypeStruct((n_rows, sw), BF16),
        grid=(n_rows // tm,),
        in_specs=[
            pl.BlockSpec((tm, sw), lambda i: (i, col0)),
            pl.BlockSpec((tm, sw), lambda i: (i, col0 + 1)),
            pl.BlockSpec((1, sw), lambda i: (0, 0)),
            pl.BlockSpec((SG_GROUPS, SG_CHUNK, SG_CHUNK), lambda i: (0, 0, 0)),
            pl.BlockSpec((SG_CHUNK, sw), lambda i: (0, 0)),
        ],
        out_specs=pl.BlockSpec((tm, sw), lambda i: (i, 0)),
        compiler_params=_cparams(("parallel",)),
        name="spatial_gate",
    )(p, p, g.reshape(1, sw), w, bias_full)


def _out_kernel(a_ref, h_ref, s_ref, wa_ref, wh_ref, ws_ref, x_ref, gate_ref, o_ref):
    acc = _dot(a_ref[...], wa_ref[...]) + _dot(h_ref[...], wh_ref[...]) + _dot(s_ref[...], ws_ref[...])
    o_ref[...] = x_ref[...] + gate_ref[0] * acc


def out_proj(attn, hg, sg, w, layer, x, mod, *, tile_mod, n_rows, tm, tn=1024):
    n, d = n_rows, x.shape[1]
    wa, wh, ws = attn.shape[1], hg.shape[1], sg.shape[1]
    assert wa % wh == 0 and wh == ws and n % tm == 0 and d % tn == 0
    return pl.pallas_call(
        _out_kernel,
        out_shape=jax.ShapeDtypeStruct((n, d), F32),
        grid=(n // tm, d // tn),
        in_specs=[
            pl.BlockSpec((tm, wa), lambda i, j: (i, 0)),
            pl.BlockSpec((tm, wh), lambda i, j: (i, 0)),
            pl.BlockSpec((tm, ws), lambda i, j: (i, 0)),
            pl.BlockSpec((None, wa, tn), lambda i, j: (layer, 0, j)),
            pl.BlockSpec((None, wh, tn), lambda i, j: (layer, wa // wh, j)),
            pl.BlockSpec((None, ws, tn), lambda i, j: (layer, wa // wh + 1, j)),
            pl.BlockSpec((tm, tn), lambda i, j: (i, j)),
            pl.BlockSpec((1, 1, tn), lambda i, j: (tile_mod(i) * N_MOD + 2, 0, j)),
        ],
        out_specs=pl.BlockSpec((tm, tn), lambda i, j: (i, j)),
        compiler_params=_cparams(("parallel", "parallel")),
        name="out_proj",
    )(attn, hg, sg, w, w, w, x, mod)


def _ffn_kernel(*refs, tile_seq, n_long, has_short, final, rows, chunk):
    (x_ref, xp_ref, xn_ref, g_ref, sh_ref, sc_ref, gate_ref,
     wg_ref, wv_ref, cwg_ref, cwv_ref, cbg_ref, cbv_ref, wd_ref) = refs[:14]
    gf_ref = refs[14] if final else None
    o_ref, h_ref, u_ref = refs[-3:]
    i, j = pl.program_id(0), pl.program_id(1)
    tm = x_ref.shape[0]
    halo = xp_ref.shape[0]

    def norm_mod(x):
        return _norm_mod(x, g_ref[...], sc_ref[0], sh_ref[0])

    @pl.when(j == 0)
    def _():
        for r in range(0, tm, rows):
            h_ref[r:r + rows, :] = norm_mod(x_ref[r:r + rows, :])
        h_ref[tm:tm + halo, :] = norm_mod(xp_ref[...])
        h_ref[tm + halo:tm + 2 * halo, :] = norm_mod(xn_ref[...])
        o_ref[...] = jnp.zeros_like(o_ref)

    chunks = [slice(c0, c0 + chunk) for c0 in range(0, wd_ref.shape[0], chunk)]
    rid = lax.broadcasted_iota(jnp.int32, (tm, 1), 0)

    def step(interior):
        seq = tile_seq(i)
        tile_starts = ((i * tm) & (seq - 1)) == 0
        tile_ends = (((i + 1) * tm) & (seq - 1)) == 0
        if interior:
            pos = (i * tm + rid) & (seq - 1)
            first, last = pos == 0, pos == seq - 1

        for c, cs in enumerate(chunks):
            u_ref[c, 0] = _dot(h_ref[...], wg_ref[:, cs])
            u_ref[c, 1] = _dot(h_ref[...], wv_ref[:, cs])

        def conv(c, k, cw_ref, cb_ref, cs):
            u = u_ref[c, k, 0:tm, :]
            prev_row = jnp.where(tile_starts, 0.0, u_ref[c, k, tm + halo - 1:tm + halo, :])
            next_row = jnp.where(tile_ends, 0.0, u_ref[c, k, tm + halo:tm + halo + 1, :])
            before = jnp.where(rid == 0, prev_row, pltpu.roll(u, 1, 0))
            after = jnp.where(rid == tm - 1, next_row, pltpu.roll(u, tm - 1, 0))
            if interior:
                before = jnp.where(first, 0.0, before)
                after = jnp.where(last, 0.0, after)
            return (cb_ref[:, cs] + before * cw_ref[0:1, cs] + u * cw_ref[1:2, cs]
                    + after * cw_ref[2:3, cs])

        for c, cs in enumerate(chunks):
            act = _silu(conv(c, 0, cwg_ref, cbg_ref, cs)) * conv(c, 1, cwv_ref, cbv_ref, cs)
            o_ref[...] += _dot(act.astype(BF16), wd_ref[cs, :])

    if has_short:
        pl.when(i < n_long)(functools.partial(step, False))
        pl.when(i >= n_long)(functools.partial(step, True))
    else:
        step(False)

    @pl.when(j == pl.num_programs(1) - 1)
    def _():
        def finish(r, carry):
            rs = pl.ds(pl.multiple_of(r * rows, rows), rows)
            y = x_ref[rs, :] + gate_ref[0] * o_ref[rs, :]
            o_ref[rs, :] = _rms(y, gf_ref[...]) if final else y
            return carry

        lax.fori_loop(0, tm // rows, finish, 0)


def conv_ffn(x, g, mod, w_up, conv_w, conv_b, w_down, layer, *, tile_mod, t_lat, t_ctx, n_lat,
             n_rows, tm, tf=512, final_gain=None):
    n, d = x.shape
    dff = w_down.shape[1]
    assert n_rows % tm == 0 and n_lat % tm == 0 and t_lat % tm == 0 and dff % tf == 0
    assert n_rows == n_lat or tm % t_ctx == 0 or t_ctx % tm == 0
    nj = dff // tf
    halo = 8
    chunk = min(256, tf)
    hb = tm // halo
    nhb = n // halo
    n_long = n_lat // tm
    cb = conv_b.reshape(1, 2 * dff)
    final = final_gain is not None
    in_specs = [
        pl.BlockSpec((tm, d), lambda i, j: (i, 0)),
        pl.BlockSpec((halo, d), lambda i, j: (jnp.maximum(i * hb - 1, 0), 0)),
        pl.BlockSpec((halo, d), lambda i, j: (jnp.minimum((i + 1) * hb, nhb - 1), 0)),
        pl.BlockSpec((1, d), lambda i, j: (0, 0)),
        pl.BlockSpec((1, 1, d), lambda i, j: (tile_mod(i) * N_MOD + 3, 0, 0)),
        pl.BlockSpec((1, 1, d), lambda i, j: (tile_mod(i) * N_MOD + 4, 0, 0)),
        pl.BlockSpec((1, 1, d), lambda i, j: (tile_mod(i) * N_MOD + 5, 0, 0)),
        pl.BlockSpec((None, d, tf), lambda i, j: (layer, 0, j)),
        pl.BlockSpec((None, d, tf), lambda i, j: (layer, 0, nj + j)),
        pl.BlockSpec((CONV_W, tf), lambda i, j: (0, j)),
        pl.BlockSpec((CONV_W, tf), lambda i, j: (0, nj + j)),
        pl.BlockSpec((1, tf), lambda i, j: (0, j)),
        pl.BlockSpec((1, tf), lambda i, j: (0, nj + j)),
        pl.BlockSpec((None, tf, d), lambda i, j: (layer, j, 0)),
    ]
    args = [x, x, x, g.reshape(1, d), mod, mod, mod, w_up, w_up, conv_w, conv_w, cb, cb, w_down]
    if final:
        in_specs.append(pl.BlockSpec((1, d), lambda i, j: (0, 0)))
        args.append(final_gain.reshape(1, d))
    return pl.pallas_call(
        functools.partial(
            _ffn_kernel, tile_seq=lambda i: jnp.where(i < n_long, t_lat, t_ctx), n_long=n_long,
            has_short=n_rows > n_lat and t_ctx < tm, final=final, rows=min(tm, 256), chunk=chunk),
        out_shape=jax.ShapeDtypeStruct((n_rows, d), F32),
        grid=(n_rows // tm, nj),
        in_specs=in_specs,
        out_specs=pl.BlockSpec((tm, d), lambda i, j: (i, 0)),
        scratch_shapes=[pltpu.VMEM((tm + 2 * halo, d), BF16),
                        pltpu.VMEM((tf // chunk, 2, tm + 2 * halo, chunk), F32)],
        compiler_params=_cparams(("parallel", "arbitrary")),
        name="conv_ffn",
    )(*args)


def _rope_tables(t_lat, tm):
    rows = t_lat // GRID_W
    row = jnp.repeat(jnp.arange(rows, dtype=F32), GRID_W)
    col = jnp.tile(jnp.arange(GRID_W, dtype=F32), rows)
    n_freq = HEAD_DIM // 4
    inv = ROPE_THETA ** (-jnp.arange(n_freq, dtype=F32) / n_freq)
    ang = jnp.concatenate([row[:, None] * inv, col[:, None] * inv], axis=-1)
    cos2 = jnp.repeat(jnp.cos(ang), 2, axis=-1)
    sin2 = jnp.repeat(jnp.sin(ang), 2, axis=-1) * jnp.tile(jnp.array([-1.0, 1.0], F32), HEAD_DIM // 2)
    cos2 = jnp.concatenate([cos2, jnp.ones((tm, HEAD_DIM), F32)], axis=0)
    sin2 = jnp.concatenate([sin2, jnp.zeros((tm, HEAD_DIM), F32)], axis=0)
    return cos2, sin2


def _lower_bounds(lb_param):
    p = jax.nn.softmax(lb_param.astype(F32), axis=1)
    return jnp.cumsum(p, axis=1) - p[:, :1]


def kernel(x, c, ctx, c_ctx, w_ada, b_ada, norm1_g, w_in, q_norm_g, k_norm_g, hg_lower_bounds,
           hg_norm_g, sg_norm_g, sg_w, sg_b, w_out, norm2_g, w_up, conv_w, conv_b, w_down,
           final_norm_g):
    batch, t_lat, d = x.shape
    t_ctx = ctx.shape[1]
    depth = w_in.shape[0]
    n_lat, n_ctx = batch * t_lat, batch * t_ctx
    assert t_lat & (t_lat - 1) == 0 and t_ctx & (t_ctx - 1) == 0

    tm = min(1024, n_ctx)
    assert t_lat % tm == 0 and n_ctx % tm == 0

    def tile_mod(i):
        return jnp.minimum(i // (t_lat // tm), batch)

    def tile_rope(i):
        return jnp.where(i < n_lat // tm, i % (t_lat // tm), t_lat // tm)

    xs = jnp.concatenate([x.reshape(n_lat, d), ctx.reshape(n_ctx, d)], axis=0)
    cin = jnp.concatenate([c, c_ctx[None, :], jnp.zeros((8 - batch - 1, d), F32)], axis=0)
    mods = ada_table(cin, w_ada, b_ada)[:, :batch + 1, :]
    mods = mods.reshape(depth, (batch + 1) * N_MOD, 1, d)
    cos2, sin2 = _rope_tables(t_lat, tm)
    lbs = _lower_bounds(hg_lower_bounds)
    qw = ATTN_HEADS * HEAD_DIM
    kvw = ATTN_KV_HEADS * HEAD_DIM
    hg_col0 = (qw + 2 * kvw) // (HG_HEADS * HG_D)
    sg_col0 = hg_col0 + 5

    w_in, w_out, w_up, w_down = (w.astype(BF16) for w in (w_in, w_out, w_up, w_down))

    for l in range(depth):
        mod = mods[l]
        last = l == depth - 1
        n_rows = n_lat if last else n_lat + n_ctx
        p = in_proj(xs, norm1_g[l], mod, w_in, l, tile_mod=tile_mod, tm=tm)
        q, k, v = qk_prep(p, cos2, sin2, q_norm_g[l], k_norm_g[l], tile_rope=tile_rope, tm=tm)
        attn = attention(q, k, v, batch=batch, t_lat=t_lat, t_ctx=t_ctx, ctx_queries=not last)
        o_f = hgrn_scan(p, lbs[0, l], batch=batch, t_lat=t_lat, t_ctx=t_ctx, col0=hg_col0,
                        reverse=False)
        hg = hgrn_scan(p, lbs[1, l], batch=batch, t_lat=t_lat, t_ctx=t_ctx, col0=hg_col0,
                       reverse=True, fwd_out=o_f, gain=hg_norm_g[l])
        bias_full = jnp.repeat(sg_b[l].T, SG_DIM, axis=1)
        sg = spatial_gate(p, sg_norm_g[l], sg_w[l].astype(BF16), bias_full, col0=sg_col0,
                          n_rows=n_rows, tm=tm)
        xs = out_proj(attn, hg, sg, w_out, l, xs, mod, tile_mod=tile_mod, n_rows=n_rows, tm=tm)
        xs = conv_ffn(xs, norm2_g[l], mod, w_up, conv_w[l], conv_b[l], w_down, l,
                      tile_mod=tile_mod, t_lat=t_lat, t_ctx=t_ctx, n_lat=n_lat, n_rows=n_rows,
                      tm=tm, final_gain=final_norm_g if last else None)

    return xs.reshape(batch, t_lat, d)
```

```python
import functools

import jax
import jax.numpy as jnp
from jax import lax
from jax.experimental import pallas as pl
from jax.experimental.pallas import tpu as pltpu

F32 = jnp.float32
BF16 = jnp.bfloat16

EPS = 1e-6
F_MIN = 1e-30
N_MOD = 6
HEAD_DIM = 128
ATTN_HEADS = 8
ATTN_KV_HEADS = 2
ATTN_GROUP = ATTN_HEADS // ATTN_KV_HEADS
ROPE_THETA = 10000.0
GRID_W = 64
HG_HEADS = 4
HG_D = 128
SG_GROUPS = 4
SG_DIM = 128
SG_CHUNK = 128
CONV_W = 3
LANE = 128
SUBLANES = 8
LOG2_E = 1.4426950408889634
HG_CHUNK = 128
VMEM_LIMIT = 62 * 1024 * 1024


def _cparams(sem):
    return pltpu.CompilerParams(dimension_semantics=sem, vmem_limit_bytes=VMEM_LIMIT)


def _dot(a, b):
    return jnp.dot(a, b, preferred_element_type=F32)


def _dot_nt(a, b):
    return lax.dot_general(a, b, (((1,), (1,)), ((), ())), preferred_element_type=F32)


def _rms(x, g):
    return x * lax.rsqrt(jnp.mean(x * x, axis=-1, keepdims=True) + EPS) * g


def _silu(x):
    return x * jax.nn.sigmoid(x)


def _ada_kernel(c_ref, w_ref, b_ref, o_ref):
    s = _silu(c_ref[...]).astype(BF16)
    o_ref[...] = _dot(s, w_ref[...].astype(BF16)) + b_ref[...]


def ada_table(cin, w_ada, b_ada, tn=1024):
    depth, d, n = w_ada.shape
    assert n % tn == 0
    return pl.pallas_call(
        _ada_kernel,
        out_shape=jax.ShapeDtypeStruct((depth, 8, n), F32),
        grid=(depth, n // tn),
        in_specs=[
            pl.BlockSpec((8, d), lambda l, j: (0, 0)),
            pl.BlockSpec((None, d, tn), lambda l, j: (l, 0, j)),
            pl.BlockSpec((None, 1, tn), lambda l, j: (l, 0, j)),
        ],
        out_specs=pl.BlockSpec((None, 8, tn), lambda l, j: (l, 0, j)),
        compiler_params=_cparams(("parallel", "parallel")),
        name="ada_table",
    )(cin, w_ada, b_ada.reshape(depth, 1, n))


def _norm_mod(x, g, scale, shift):
    r = lax.rsqrt(jnp.mean(x * x, axis=-1, keepdims=True) + EPS)
    return ((x * r) * (g * (1.0 + scale)) + shift).astype(BF16)


def _in_kernel(x_ref, g_ref, sh_ref, sc_ref, w_ref, o_ref, h_ref, *, rows):
    @pl.when(pl.program_id(1) == 0)
    def _():
        for r in range(0, x_ref.shape[0], rows):
            h_ref[r:r + rows, :] = _norm_mod(x_ref[r:r + rows, :], g_ref[...], sc_ref[0], sh_ref[0])

    o_ref[...] = _dot(h_ref[...], w_ref[...])


def in_proj(x, g, mod, w, layer, *, tile_mod, tm, tn=1024):
    n, d = x.shape
    cols = w.shape[2]
    assert n % tm == 0 and cols % tn == 0
    return pl.pallas_call(
        functools.partial(_in_kernel, rows=min(tm, 256)),
        out_shape=jax.ShapeDtypeStruct((n, cols), F32),
        grid=(n // tm, cols // tn),
        in_specs=[
            pl.BlockSpec((tm, d), lambda i, j: (i, 0)),
            pl.BlockSpec((1, d), lambda i, j: (0, 0)),
            pl.BlockSpec((1, 1, d), lambda i, j: (tile_mod(i) * N_MOD + 0, 0, 0)),
            pl.BlockSpec((1, 1, d), lambda i, j: (tile_mod(i) * N_MOD + 1, 0, 0)),
            pl.BlockSpec((None, d, tn), lambda i, j: (layer, 0, j)),
        ],
        out_specs=pl.BlockSpec((tm, tn), lambda i, j: (i, j)),
        scratch_shapes=[pltpu.VMEM((tm, d), BF16)],
        compiler_params=_cparams(("parallel", "arbitrary")),
        name="in_proj",
    )(x, g.reshape(1, d), mod, mod, w)


def _rope(y, cos2, sin2):
    lane = lax.broadcasted_iota(jnp.int32, y.shape, 1)
    swapped = jnp.where((lane & 1) == 0, pltpu.roll(y, LANE - 1, 1), pltpu.roll(y, 1, 1))
    return y * cos2 + swapped * sin2


def _qk_kernel(q_ref, kv_ref, cos_ref, sin_ref, qg_ref, kg_ref, qo_ref, ko_ref, vo_ref):
    cos2, sin2 = cos_ref[...], sin_ref[...]
    scale = HEAD_DIM ** -0.5 * LOG2_E
    for h in range(ATTN_HEADS):
        sl = slice(h * HEAD_DIM, (h + 1) * HEAD_DIM)
        y = _rms(q_ref[:, sl], qg_ref[...])
        qo_ref[:, sl] = (_rope(y, cos2, sin2) * scale).astype(BF16)
    for h in range(ATTN_KV_HEADS):
        sl = slice(h * HEAD_DIM, (h + 1) * HEAD_DIM)
        y = _rms(kv_ref[:, sl], kg_ref[...])
        ko_ref[:, sl] = _rope(y, cos2, sin2).astype(BF16)
    kvw = ATTN_KV_HEADS * HEAD_DIM
    vo_ref[...] = kv_ref[:, kvw:2 * kvw].T.astype(BF16)


def qk_prep(p, cos2, sin2, qg, kg, *, tile_rope, tm):
    n = p.shape[0]
    qw = ATTN_HEADS * HEAD_DIM
    kvw = ATTN_KV_HEADS * HEAD_DIM
    return pl.pallas_call(
        _qk_kernel,
        out_shape=(jax.ShapeDtypeStruct((n, qw), BF16),
                   jax.ShapeDtypeStruct((n, kvw), BF16),
                   jax.ShapeDtypeStruct((kvw, n), BF16)),
        grid=(n // tm,),
        in_specs=[
            pl.BlockSpec((tm, qw), lambda i: (i, 0)),
            pl.BlockSpec((tm, 2 * kvw), lambda i: (i, qw // (2 * kvw))),
            pl.BlockSpec((tm, HEAD_DIM), lambda i: (tile_rope(i), 0)),
            pl.BlockSpec((tm, HEAD_DIM), lambda i: (tile_rope(i), 0)),
            pl.BlockSpec((1, HEAD_DIM), lambda i: (0, 0)),
            pl.BlockSpec((1, HEAD_DIM), lambda i: (0, 0)),
        ],
        out_specs=(pl.BlockSpec((tm, qw), lambda i: (i, 0)),
                   pl.BlockSpec((tm, kvw), lambda i: (i, 0)),
                   pl.BlockSpec((kvw, tm), lambda i: (0, i))),
        compiler_params=_cparams(("parallel",)),
        name="qk_prep",
    )(p, p, cos2, sin2, qg.reshape(1, HEAD_DIM), kg.reshape(1, HEAD_DIM))


def _softmax_pv(q, segs, ck, s_ref):
    mq = q.shape[0]
    chunks = [(k_ref, vt_ref, c0, min(ck, k_ref.shape[0]))
              for k_ref, vt_ref in segs for c0 in range(0, k_ref.shape[0], min(ck, k_ref.shape[0]))]

    def scores(i):
        k_ref, _, c0, cs = chunks[i]
        s_ref[i % 2, 0:cs, :] = _dot_nt(k_ref[c0:c0 + cs, :], q)

    scores(0)
    m = l = acc = None
    for i, (_, vt_ref, c0, cs) in enumerate(chunks):
        if i + 1 < len(chunks):
            scores(i + 1)
        s = s_ref[i % 2, 0:cs, :]
        m_c = s.reshape(cs // SUBLANES, SUBLANES, mq).max(axis=0).max(axis=0, keepdims=True)
        m_new = m_c if m is None else jnp.maximum(m, m_c)
        p = jnp.exp2(s - m_new)
        p_rows = p.reshape(cs // SUBLANES, SUBLANES, mq).sum(axis=0)
        pv = _dot(vt_ref[:, c0:c0 + cs], p.astype(BF16))
        if m is None:
            l, acc = p_rows, pv
        else:
            alpha = jnp.exp2(m - m_new)
            l, acc = alpha * l + p_rows, alpha * acc + pv
        m = m_new
    return (acc / l.sum(axis=0, keepdims=True)).T


def _attn_kernel(q_ref, kl_ref, kc_ref, vl_ref, vc_ref, o_ref, s_ref, *, n_lat, ck):
    qi = pl.program_id(2)
    tq = q_ref.shape[0]

    def run(segs):
        hs = range(ATTN_GROUP)
        q = jnp.concatenate([q_ref[:, h * HEAD_DIM:(h + 1) * HEAD_DIM] for h in hs], axis=0)
        o = _softmax_pv(q, segs, ck, s_ref)
        for h in hs:
            o_ref[:, h * HEAD_DIM:(h + 1) * HEAD_DIM] = o[h * tq:(h + 1) * tq].astype(BF16)

    @pl.when(qi < n_lat)
    def _():
        run([(kl_ref, vl_ref), (kc_ref, vc_ref)])

    @pl.when(qi >= n_lat)
    def _():
        run([(kc_ref, vc_ref)])


def attention(q, k, vt, *, batch, t_lat, t_ctx, ctx_queries=True, tq=256, ck=512):
    n = q.shape[0]
    n_lat = t_lat // tq
    n_ctx = t_ctx // tq if ctx_queries else 0
    gw = ATTN_GROUP * HEAD_DIM
    ctx0 = batch * t_lat

    def qmap(b, j, i):
        return (jnp.where(i < n_lat, b * n_lat + i, ctx0 // tq + b * (t_ctx // tq) + (i - n_lat)), j)

    return pl.pallas_call(
        functools.partial(_attn_kernel, n_lat=n_lat, ck=ck),
        out_shape=jax.ShapeDtypeStruct((n, ATTN_HEADS * HEAD_DIM), BF16),
        grid=(batch, ATTN_KV_HEADS, n_lat + n_ctx),
        in_specs=[
            pl.BlockSpec((tq, gw), qmap),
            pl.BlockSpec((t_lat, HEAD_DIM), lambda b, j, i: (b, j)),
            pl.BlockSpec((t_ctx, HEAD_DIM), lambda b, j, i: (ctx0 // t_ctx + b, j)),
            pl.BlockSpec((HEAD_DIM, t_lat), lambda b, j, i: (j, b)),
            pl.BlockSpec((HEAD_DIM, t_ctx), lambda b, j, i: (j, ctx0 // t_ctx + b)),
        ],
        out_specs=pl.BlockSpec((tq, gw), qmap),
        scratch_shapes=[pltpu.VMEM((2, ck, ATTN_GROUP * tq), F32)],
        compiler_params=_cparams(("parallel", "parallel", "arbitrary")),
        name="attention",
    )(q, k, k, vt, vt)


def _hgrn_chunk(hq, z, v, lb, st, *, reverse):
    c = hq.shape[0]
    sig = jax.nn.sigmoid(z)
    f = lb + (1.0 - lb) * sig
    logf = jnp.log(jnp.maximum(f, F_MIN)) * (1.0 / jnp.log(2.0))
    kk = (1.0 - lb) * (1.0 - sig)
    q = _silu(hq)

    row = lax.broadcasted_iota(jnp.int32, (c, c), 0)
    col = lax.broadcasted_iota(jnp.int32, (c, c), 1)
    rid = lax.broadcasted_iota(jnp.int32, (c, HG_D), 0)
    later = (row < col) if reverse else (row > col)

    a = jnp.where(row == col, _dot_nt(q.astype(BF16), kk.astype(BF16)), 0.0)
    cum, tot = logf, logf
    d, ld = 1, 0
    while d < c:
        mask = jnp.logical_and(((row ^ col) >> ld) == 1, later)
        if d < SUBLANES:
            second = (rid & d) != 0
            sel = jnp.logical_not(second) if reverse else second
            x = jnp.where(sel, q, kk) * jnp.exp2(jnp.where(sel, cum, tot - cum))
            t3 = tot.reshape(c // SUBLANES, SUBLANES, HG_D)
            up = pltpu.roll(t3, d, 1).reshape(c, HG_D)
            partner = up if 2 * d == SUBLANES else jnp.where(
                second, up, pltpu.roll(t3, SUBLANES - d, 1).reshape(c, HG_D))
            cum = cum + jnp.where(sel, partner, 0.0)
            tot = tot + partner
        else:
            def split(y):
                y4 = y.reshape(c // (2 * d), 2, d, HG_D)
                return (y4[:, 1], y4[:, 0]) if reverse else (y4[:, 0], y4[:, 1])

            def join(early, late):
                pair = [late, early] if reverse else [early, late]
                return jnp.stack(pair, axis=1).reshape(c, HG_D)

            (cum_e, cum_l), (tot_e, tot_l) = split(cum), split(tot)
            x = join(split(kk)[0] * jnp.exp2(tot_e - cum_e), split(q)[1] * jnp.exp2(cum_l))
            both = tot_e + tot_l
            cum, tot = join(cum_e, cum_l + tot_e), join(both, both)
        xb = x.astype(BF16)
        a = jnp.where(mask, _dot_nt(xb, xb), a)
        d, ld = d * 2, ld + 1

    qd = (q * jnp.exp2(cum)).astype(BF16)
    kd = (kk * jnp.exp2(tot - cum)).astype(BF16)
    vb = v.astype(BF16)
    o = _dot(a.astype(BF16), vb) + _dot_nt(qd, st.astype(BF16))
    st_new = st * jnp.exp2(tot[0:1, :]) + _dot(vb.T, kd)
    return o, st_new


def _hgrn_kernel(*refs, reverse, finalize):
    if finalize:
        hq_ref, hf_ref, hi_ref, lb_ref, of_ref, gt_ref, g_ref, o_ref, st_ref = refs
    else:
        hq_ref, hf_ref, hi_ref, lb_ref, o_ref, st_ref = refs

    @pl.when(pl.program_id(1) == 0)
    def _():
        st_ref[...] = jnp.zeros_like(st_ref)

    starts = range(0, hq_ref.shape[0], HG_CHUNK)
    for r in (reversed(starts) if reverse else starts):
        rs = slice(r, r + HG_CHUNK)
        for h in range(HG_HEADS):
            sl = slice(h * HG_D, (h + 1) * HG_D)
            o, st_new = _hgrn_chunk(hq_ref[rs, sl], hf_ref[rs, sl], hi_ref[rs, sl], lb_ref[:, sl],
                                    st_ref[h], reverse=reverse)
            st_ref[h] = st_new
            if finalize:
                y = _rms(o + of_ref[rs, sl], g_ref[...])
                o_ref[rs, sl] = (y * _silu(gt_ref[rs, sl])).astype(o_ref.dtype)
            else:
                o_ref[rs, sl] = o


def hgrn_scan(p, lb, *, batch, t_lat, t_ctx, col0, reverse, fwd_out=None, gain=None,
              chunks_per_step=2):
    n = p.shape[0]
    c = HG_CHUNK * chunks_per_step
    assert t_lat % c == 0 and t_ctx % c == 0
    w = HG_HEADS * HG_D
    n_lat, n_ctx = t_lat // c, t_ctx // c
    ctx0 = batch * n_lat

    def rows(b, s):
        if reverse:
            return jnp.where(s < n_ctx, ctx0 + b * n_ctx + (n_ctx - 1 - s),
                             b * n_lat + (n_lat - 1 - (s - n_ctx)))
        return jnp.where(s < n_ctx, ctx0 + b * n_ctx + s, b * n_lat + (s - n_ctx))

    def spec(cb):
        return pl.BlockSpec((c, w), lambda b, s: (rows(b, s), cb))

    finalize = fwd_out is not None
    in_specs = [spec(col0), spec(col0 + (2 if reverse else 1)), spec(col0 + 3),
                pl.BlockSpec((1, w), lambda b, s: (0, 0))]
    args = [p, p, p, lb.reshape(1, w)]
    if finalize:
        in_specs += [spec(0), spec(col0 + 4), pl.BlockSpec((1, HG_D), lambda b, s: (0, 0))]
        args += [fwd_out, p, gain.reshape(1, HG_D)]
    return pl.pallas_call(
        functools.partial(_hgrn_kernel, reverse=reverse, finalize=finalize),
        out_shape=jax.ShapeDtypeStruct((n, w), BF16 if finalize else F32),
        grid=(batch, n_lat + n_ctx),
        in_specs=in_specs,
        out_specs=spec(0),
        scratch_shapes=[pltpu.VMEM((HG_HEADS, HG_D, HG_D), F32)],
        compiler_params=_cparams(("parallel", "arbitrary")),
        name="hgrn_bwd" if reverse else "hgrn_fwd",
    )(*args)


def _sg_kernel(u_ref, v_ref, g_ref, w_ref, b_ref, o_ref):
    for r in range(0, u_ref.shape[0], SG_CHUNK):
        rs = slice(r, r + SG_CHUNK)
        for g in range(SG_GROUPS):
            sl = slice(g * SG_DIM, (g + 1) * SG_DIM)
            vn = _rms(jax.nn.gelu(v_ref[rs, sl]), g_ref[:, sl])
            mixed = _dot(w_ref[g], vn.astype(BF16)) + b_ref[:, sl]
            o_ref[rs, sl] = (jax.nn.gelu(u_ref[rs, sl]) * mixed).astype(BF16)


def spatial_gate(p, g, w, bias_full, *, col0, n_rows, tm):
    sw = SG_GROUPS * SG_DIM
    assert n_rows % tm == 0
    return pl.pallas_call(
        _sg_kernel,
        out_shape=jax.ShapeDtypeStruct((n_rows, sw), BF16),
        grid=(n_rows // tm,),
        in_specs=[
            pl.BlockSpec((tm, sw), lambda i: (i, col0)),
            pl.BlockSpec((tm, sw), lambda i: (i, col0 + 1)),
            pl.BlockSpec((1, sw), lambda i: (0, 0)),
            pl.BlockSpec((SG_GROUPS, SG_CHUNK, SG_CHUNK), lambda i: (0, 0, 0)),
            pl.BlockSpec((SG_CHUNK, sw), lambda i: (0, 0)),
        ],
        out_specs=pl.BlockSpec((tm, sw), lambda i: (i, 0)),
        compiler_params=_cparams(("parallel",)),
        name="spatial_gate",
    )(p, p, g.reshape(1, sw), w, bias_full)


def _out_kernel(a_ref, h_ref, s_ref, wa_ref, wh_ref, ws_ref, x_ref, gate_ref, o_ref):
    acc = _dot(a_ref[...], wa_ref[...]) + _dot(h_ref[...], wh_ref[...]) + _dot(s_ref[...], ws_ref[...])
    o_ref[...] = x_ref[...] + gate_ref[0] * acc


def out_proj(attn, hg, sg, w, layer, x, mod, *, tile_mod, n_rows, tm, tn=1024):
    n, d = n_rows, x.shape[1]
    wa, wh, ws = attn.shape[1], hg.shape[1], sg.shape[1]
    assert wa % wh == 0 and wh == ws and n % tm == 0 and d % tn == 0
    return pl.pallas_call(
        _out_kernel,
        out_shape=jax.ShapeDtypeStruct((n, d), F32),
        grid=(n // tm, d // tn),
        in_specs=[
            pl.BlockSpec((tm, wa), lambda i, j: (i, 0)),
            pl.BlockSpec((tm, wh), lambda i, j: (i, 0)),
            pl.BlockSpec((tm, ws), lambda i, j: (i, 0)),
            pl.BlockSpec((None, wa, tn), lambda i, j: (layer, 0, j)),
            pl.BlockSpec((None, wh, tn), lambda i, j: (layer, wa // wh, j)),
            pl.BlockSpec((None, ws, tn), lambda i, j: (layer, wa // wh + 1, j)),
            pl.BlockSpec((tm, tn), lambda i, j: (i, j)),
            pl.BlockSpec((1, 1, tn), lambda i, j: (tile_mod(i) * N_MOD + 2, 0, j)),
        ],
        out_specs=pl.BlockSpec((tm, tn), lambda i, j: (i, j)),
        compiler_params=_cparams(("parallel", "parallel")),
        name="out_proj",
    )(attn, hg, sg, w, w, w, x, mod)


def _ffn_kernel(*refs, tile_seq, n_long, has_short, final, rows, chunk):
    (x_ref, xp_ref, xn_ref, g_ref, sh_ref, sc_ref, gate_ref,
     wg_ref, wv_ref, cwg_ref, cwv_ref, cbg_ref, cbv_ref, wd_ref) = refs[:14]
    gf_ref = refs[14] if final else None
    o_ref, h_ref, u_ref = refs[-3:]
    i, j = pl.program_id(0), pl.program_id(1)
    tm = x_ref.shape[0]
    halo = xp_ref.shape[0]

    def norm_mod(x):
        return _norm_mod(x, g_ref[...], sc_ref[0], sh_ref[0])

    @pl.when(j == 0)
    def _():
        for r in range(0, tm, rows):
            h_ref[r:r + rows, :] = norm_mod(x_ref[r:r + rows, :])
        h_ref[tm:tm + halo, :] = norm_mod(xp_ref[...])
        h_ref[tm + halo:tm + 2 * halo, :] = norm_mod(xn_ref[...])
        o_ref[...] = jnp.zeros_like(o_ref)

    chunks = [slice(c0, c0 + chunk) for c0 in range(0, wd_ref.shape[0], chunk)]
    rid = lax.broadcasted_iota(jnp.int32, (tm, 1), 0)

    def step(interior):
        seq = tile_seq(i)
        tile_starts = ((i * tm) & (seq - 1)) == 0
        tile_ends = (((i + 1) * tm) & (seq - 1)) == 0
        if interior:
            pos = (i * tm + rid) & (seq - 1)
            first, last = pos == 0, pos == seq - 1

        for c, cs in enumerate(chunks):
            u_ref[c, 0] = _dot(h_ref[...], wg_ref[:, cs])
            u_ref[c, 1] = _dot(h_ref[...], wv_ref[:, cs])

        def conv(c, k, cw_ref, cb_ref, cs):
            u = u_ref[c, k, 0:tm, :]
            prev_row = jnp.where(tile_starts, 0.0, u_ref[c, k, tm + halo - 1:tm + halo, :])
            next_row = jnp.where(tile_ends, 0.0, u_ref[c, k, tm + halo:tm + halo + 1, :])
            before = jnp.where(rid == 0, prev_row, pltpu.roll(u, 1, 0))
            after = jnp.where(rid == tm - 1, next_row, pltpu.roll(u, tm - 1, 0))
            if interior:
                before = jnp.where(first, 0.0, before)
                after = jnp.where(last, 0.0, after)
            return (cb_ref[:, cs] + before * cw_ref[0:1, cs] + u * cw_ref[1:2, cs]
                    + after * cw_ref[2:3, cs])

        for c, cs in enumerate(chunks):
            act = _silu(conv(c, 0, cwg_ref, cbg_ref, cs)) * conv(c, 1, cwv_ref, cbv_ref, cs)
            o_ref[...] += _dot(act.astype(BF16), wd_ref[cs, :])

    if has_short:
        pl.when(i < n_long)(functools.partial(step, False))
        pl.when(i >= n_long)(functools.partial(step, True))
    else:
        step(False)

    @pl.when(j == pl.num_programs(1) - 1)
    def _():
        def finish(r, carry):
            rs = pl.ds(pl.multiple_of(r * rows, rows), rows)
            y = x_ref[rs, :] + gate_ref[0] * o_ref[rs, :]
            o_ref[rs, :] = _rms(y, gf_ref[...]) if final else y
            return carry

        lax.fori_loop(0, tm // rows, finish, 0)


def conv_ffn(x, g, mod, w_up, conv_w, conv_b, w_down, layer, *, tile_mod, t_lat, t_ctx, n_lat,
             n_rows, tm, tf=512, final_gain=None):
    n, d = x.shape
    dff = w_down.shape[1]
    assert n_rows % tm == 0 and n_lat % tm == 0 and t_lat % tm == 0 and dff % tf == 0
    assert n_rows == n_lat or tm % t_ctx == 0 or t_ctx % tm == 0
    nj = dff // tf
    halo = 8
    chunk = min(256, tf)
    hb = tm // halo
    nhb = n // halo
    n_long = n_lat // tm
    cb = conv_b.reshape(1, 2 * dff)
    final = final_gain is not None
    in_specs = [
        pl.BlockSpec((tm, d), lambda i, j: (i, 0)),
        pl.BlockSpec((halo, d), lambda i, j: (jnp.maximum(i * hb - 1, 0), 0)),
        pl.BlockSpec((halo, d), lambda i, j: (jnp.minimum((i + 1) * hb, nhb - 1), 0)),
        pl.BlockSpec((1, d), lambda i, j: (0, 0)),
        pl.BlockSpec((1, 1, d), lambda i, j: (tile_mod(i) * N_MOD + 3, 0, 0)),
        pl.BlockSpec((1, 1, d), lambda i, j: (tile_mod(i) * N_MOD + 4, 0, 0)),
        pl.BlockSpec((1, 1, d), lambda i, j: (tile_mod(i) * N_MOD + 5, 0, 0)),
        pl.BlockSpec((None, d, tf), lambda i, j: (layer, 0, j)),
        pl.BlockSpec((None, d, tf), lambda i, j: (layer, 0, nj + j)),
        pl.BlockSpec((CONV_W, tf), lambda i, j: (0, j)),
        pl.BlockSpec((CONV_W, tf), lambda i, j: (0, nj + j)),
        pl.BlockSpec((1, tf), lambda i, j: (0, j)),
        pl.BlockSpec((1, tf), lambda i, j: (0, nj + j)),
        pl.BlockSpec((None, tf, d), lambda i, j: (layer, j, 0)),
    ]
    args = [x, x, x, g.reshape(1, d), mod, mod, mod, w_up, w_up, conv_w, conv_w, cb, cb, w_down]
    if final:
        in_specs.append(pl.BlockSpec((1, d), lambda i, j: (0, 0)))
        args.append(final_gain.reshape(1, d))
    return pl.pallas_call(
        functools.partial(
            _ffn_kernel, tile_seq=lambda i: jnp.where(i < n_long, t_lat, t_ctx), n_long=n_long,
            has_short=n_rows > n_lat and t_ctx < tm, final=final, rows=min(tm, 256), chunk=chunk),
        out_shape=jax.ShapeDtypeStruct((n_rows, d), F32),
        grid=(n_rows // tm, nj),
        in_specs=in_specs,
        out_specs=pl.BlockSpec((tm, d), lambda i, j: (i, 0)),
        scratch_shapes=[pltpu.VMEM((tm + 2 * halo, d), BF16),
                        pltpu.VMEM((tf // chunk, 2, tm + 2 * halo, chunk), F32)],
        compiler_params=_cparams(("parallel", "arbitrary")),
        name="conv_ffn",
    )(*args)


def _rope_tables(t_lat, tm):
    rows = t_lat // GRID_W
    row = jnp.repeat(jnp.arange(rows, dtype=F32), GRID_W)
    col = jnp.tile(jnp.arange(GRID_W, dtype=F32), rows)
    n_freq = HEAD_DIM // 4
    inv = ROPE_THETA ** (-jnp.arange(n_freq, dtype=F32) / n_freq)
    ang = jnp.concatenate([row[:, None] * inv, col[:, None] * inv], axis=-1)
    cos2 = jnp.repeat(jnp.cos(ang), 2, axis=-1)
    sin2 = jnp.repeat(jnp.sin(ang), 2, axis=-1) * jnp.tile(jnp.array([-1.0, 1.0], F32), HEAD_DIM // 2)
    cos2 = jnp.concatenate([cos2, jnp.ones((tm, HEAD_DIM), F32)], axis=0)
    sin2 = jnp.concatenate([sin2, jnp.zeros((tm, HEAD_DIM), F32)], axis=0)
    return cos2, sin2


def _lower_bounds(lb_param):
    p = jax.nn.softmax(lb_param.astype(F32), axis=1)
    return jnp.cumsum(p, axis=1) - p[:, :1]


def kernel(x, c, ctx, c_ctx, w_ada, b_ada, norm1_g, w_in, q_norm_g, k_norm_g, hg_lower_bounds,
           hg_norm_g, sg_norm_g, sg_w, sg_b, w_out, norm2_g, w_up, conv_w, conv_b, w_down,
           final_norm_g):
    batch, t_lat, d = x.shape
    t_ctx = ctx.shape[1]
    depth = w_in.shape[0]
    n_lat, n_ctx = batch * t_lat, batch * t_ctx
    assert t_lat & (t_lat - 1) == 0 and t_ctx & (t_ctx - 1) == 0

    tm = min(1024, n_ctx)
    assert t_lat % tm == 0 and n_ctx % tm == 0

    def tile_mod(i):
        return jnp.minimum(i // (t_lat // tm), batch)

    def tile_rope(i):
        return jnp.where(i < n_lat // tm, i % (t_lat // tm), t_lat // tm)

    xs = jnp.concatenate([x.reshape(n_lat, d), ctx.reshape(n_ctx, d)], axis=0)
    cin = jnp.concatenate([c, c_ctx[None, :], jnp.zeros((8 - batch - 1, d), F32)], axis=0)
    mods = ada_table(cin, w_ada, b_ada)[:, :batch + 1, :]
    mods = mods.reshape(depth, (batch + 1) * N_MOD, 1, d)
    cos2, sin2 = _rope_tables(t_lat, tm)
    lbs = _lower_bounds(hg_lower_bounds)
    qw = ATTN_HEADS * HEAD_DIM
    kvw = ATTN_KV_HEADS * HEAD_DIM
    hg_col0 = (qw + 2 * kvw) // (HG_HEADS * HG_D)
    sg_col0 = hg_col0 + 5

    w_in, w_out, w_up, w_down = (w.astype(BF16) for w in (w_in, w_out, w_up, w_down))

    for l in range(depth):
        mod = mods[l]
        last = l == depth - 1
        n_rows = n_lat if last else n_lat + n_ctx
        p = in_proj(xs, norm1_g[l], mod, w_in, l, tile_mod=tile_mod, tm=tm)
        q, k, vt = qk_prep(p, cos2, sin2, q_norm_g[l], k_norm_g[l], tile_rope=tile_rope, tm=tm)
        attn = attention(q, k, vt, batch=batch, t_lat=t_lat, t_ctx=t_ctx, ctx_queries=not last)
        o_f = hgrn_scan(p, lbs[0, l], batch=batch, t_lat=t_lat, t_ctx=t_ctx, col0=hg_col0,
                        reverse=False)
        hg = hgrn_scan(p, lbs[1, l], batch=batch, t_lat=t_lat, t_ctx=t_ctx, col0=hg_col0,
                       reverse=True, fwd_out=o_f, gain=hg_norm_g[l])
        bias_full = jnp.repeat(sg_b[l].T, SG_DIM, axis=1)
        sg = spatial_gate(p, sg_norm_g[l], sg_w[l].astype(BF16), bias_full, col0=sg_col0,
                          n_rows=n_rows, tm=tm)
        xs = out_proj(attn, hg, sg, w_out, l, xs, mod, tile_mod=tile_mod, n_rows=n_rows, tm=tm)
        xs = conv_ffn(xs, norm2_g[l], mod, w_up, conv_w[l], conv_b[l], w_down, l,
                      tile_mod=tile_mod, t_lat=t_lat, t_ctx=t_ctx, n_lat=n_lat, n_rows=n_rows,
                      tm=tm, final_gain=final_norm_g if last else None)

    return xs.reshape(batch, t_lat, d)
```

```python
import functools

import jax
import jax.numpy as jnp
from jax import lax
from jax.experimental import pallas as pl
from jax.experimental.pallas import tpu as pltpu

F32 = jnp.float32
BF16 = jnp.bfloat16

EPS = 1e-6
F_MIN = 1e-30
N_MOD = 6
HEAD_DIM = 128
ATTN_HEADS = 8
ATTN_KV_HEADS = 2
ATTN_GROUP = ATTN_HEADS // ATTN_KV_HEADS
ROPE_THETA = 10000.0
GRID_W = 64
HG_HEADS = 4
HG_D = 128
SG_GROUPS = 4
SG_DIM = 128
SG_CHUNK = 128
CONV_W = 3
LANE = 128
SUBLANES = 8
LOG2_E = 1.4426950408889634
HG_CHUNK = 128
VMEM_LIMIT = 62 * 1024 * 1024


def _cparams(sem):
    return pltpu.CompilerParams(dimension_semantics=sem, vmem_limit_bytes=VMEM_LIMIT)


def _dot(a, b):
    return jnp.dot(a, b, preferred_element_type=F32)


def _dot_nt(a, b):
    return lax.dot_general(a, b, (((1,), (1,)), ((), ())), preferred_element_type=F32)


def _rms(x, g):
    return x * lax.rsqrt(jnp.mean(x * x, axis=-1, keepdims=True) + EPS) * g


def _silu(x):
    return x * jax.nn.sigmoid(x)


def _ada_kernel(c_ref, w_ref, b_ref, o_ref):
    s = _silu(c_ref[...]).astype(BF16)
    o_ref[...] = _dot(s, w_ref[...].astype(BF16)) + b_ref[...]


def ada_table(cin, w_ada, b_ada, tn=1024):
    depth, d, n = w_ada.shape
    assert n % tn == 0
    return pl.pallas_call(
        _ada_kernel,
        out_shape=jax.ShapeDtypeStruct((depth, 8, n), F32),
        grid=(depth, n // tn),
        in_specs=[
            pl.BlockSpec((8, d), lambda l, j: (0, 0)),
            pl.BlockSpec((None, d, tn), lambda l, j: (l, 0, j)),
            pl.BlockSpec((None, 1, tn), lambda l, j: (l, 0, j)),
        ],
        out_specs=pl.BlockSpec((None, 8, tn), lambda l, j: (l, 0, j)),
        compiler_params=_cparams(("parallel", "parallel")),
        name="ada_table",
    )(cin, w_ada, b_ada.reshape(depth, 1, n))


def _norm_mod(x, g, scale, shift):
    r = lax.rsqrt(jnp.mean(x * x, axis=-1, keepdims=True) + EPS)
    return ((x * r) * (g * (1.0 + scale)) + shift).astype(BF16)


def _in_kernel(x_ref, g_ref, sh_ref, sc_ref, w_ref, o_ref, h_ref, *, rows):
    @pl.when(pl.program_id(1) == 0)
    def _():
        for r in range(0, x_ref.shape[0], rows):
            h_ref[r:r + rows, :] = _norm_mod(x_ref[r:r + rows, :], g_ref[...], sc_ref[0], sh_ref[0])

    o_ref[...] = _dot(h_ref[...], w_ref[...])


def in_proj(x, g, mod, w, layer, *, tile_mod, tm, tn=1024):
    n, d = x.shape
    cols = w.shape[2]
    assert n % tm == 0 and cols % tn == 0
    return pl.pallas_call(
        functools.partial(_in_kernel, rows=min(tm, 256)),
        out_shape=jax.ShapeDtypeStruct((n, cols), F32),
        grid=(n // tm, cols // tn),
        in_specs=[
            pl.BlockSpec((tm, d), lambda i, j: (i, 0)),
            pl.BlockSpec((1, d), lambda i, j: (0, 0)),
            pl.BlockSpec((1, 1, d), lambda i, j: (tile_mod(i) * N_MOD + 0, 0, 0)),
            pl.BlockSpec((1, 1, d), lambda i, j: (tile_mod(i) * N_MOD + 1, 0, 0)),
            pl.BlockSpec((None, d, tn), lambda i, j: (layer, 0, j)),
        ],
        out_specs=pl.BlockSpec((tm, tn), lambda i, j: (i, j)),
        scratch_shapes=[pltpu.VMEM((tm, d), BF16)],
        compiler_params=_cparams(("parallel", "arbitrary")),
        name="in_proj",
    )(x, g.reshape(1, d), mod, mod, w)


def _rope(y, cos2, sin2):
    lane = lax.broadcasted_iota(jnp.int32, y.shape, 1)
    swapped = jnp.where((lane & 1) == 0, pltpu.roll(y, LANE - 1, 1), pltpu.roll(y, 1, 1))
    return y * cos2 + swapped * sin2


def _qk_kernel(q_ref, kv_ref, cos_ref, sin_ref, qg_ref, kg_ref, qo_ref, ko_ref, vo_ref):
    cos2, sin2 = cos_ref[...], sin_ref[...]
    scale = HEAD_DIM ** -0.5 * LOG2_E
    for h in range(ATTN_HEADS):
        sl = slice(h * HEAD_DIM, (h + 1) * HEAD_DIM)
        y = _rms(q_ref[:, sl], qg_ref[...])
        qo_ref[:, sl] = (_rope(y, cos2, sin2) * scale).astype(BF16)
    for h in range(ATTN_KV_HEADS):
        sl = slice(h * HEAD_DIM, (h + 1) * HEAD_DIM)
        y = _rms(kv_ref[:, sl], kg_ref[...])
        ko_ref[:, sl] = _rope(y, cos2, sin2).astype(BF16)
    kvw = ATTN_KV_HEADS * HEAD_DIM
    vo_ref[...] = kv_ref[:, kvw:2 * kvw].T.astype(BF16)


def qk_prep(p, cos2, sin2, qg, kg, *, tile_rope, tm):
    n = p.shape[0]
    qw = ATTN_HEADS * HEAD_DIM
    kvw = ATTN_KV_HEADS * HEAD_DIM
    return pl.pallas_call(
        _qk_kernel,
        out_shape=(jax.ShapeDtypeStruct((n, qw), BF16),
                   jax.ShapeDtypeStruct((n, kvw), BF16),
                   jax.ShapeDtypeStruct((kvw, n), BF16)),
        grid=(n // tm,),
        in_specs=[
            pl.BlockSpec((tm, qw), lambda i: (i, 0)),
            pl.BlockSpec((tm, 2 * kvw), lambda i: (i, qw // (2 * kvw))),
            pl.BlockSpec((tm, HEAD_DIM), lambda i: (tile_rope(i), 0)),
            pl.BlockSpec((tm, HEAD_DIM), lambda i: (tile_rope(i), 0)),
            pl.BlockSpec((1, HEAD_DIM), lambda i: (0, 0)),
            pl.BlockSpec((1, HEAD_DIM), lambda i: (0, 0)),
        ],
        out_specs=(pl.BlockSpec((tm, qw), lambda i: (i, 0)),
                   pl.BlockSpec((tm, kvw), lambda i: (i, 0)),
                   pl.BlockSpec((kvw, tm), lambda i: (0, i))),
        compiler_params=_cparams(("parallel",)),
        name="qk_prep",
    )(p, p, cos2, sin2, qg.reshape(1, HEAD_DIM), kg.reshape(1, HEAD_DIM))


def _softmax_pv(q, segs, ck, s_ref):
    mq = q.shape[0]
    chunks = [(k_ref, vt_ref, c0, min(ck, k_ref.shape[0]))
              for k_ref, vt_ref in segs for c0 in range(0, k_ref.shape[0], min(ck, k_ref.shape[0]))]

    def scores(i):
        k_ref, _, c0, cs = chunks[i]
        s_ref[i % 2, 0:cs, :] = _dot_nt(k_ref[c0:c0 + cs, :], q)

    scores(0)
    m = l = acc = None
    for i, (_, vt_ref, c0, cs) in enumerate(chunks):
        if i + 1 < len(chunks):
            scores(i + 1)
        s = s_ref[i % 2, 0:cs, :]
        m_c = s.reshape(cs // SUBLANES, SUBLANES, mq).max(axis=0).max(axis=0, keepdims=True)
        m_new = m_c if m is None else jnp.maximum(m, m_c)
        p = jnp.exp2(s - m_new)
        p_rows = p.reshape(cs // SUBLANES, SUBLANES, mq).sum(axis=0)
        pv = _dot(vt_ref[:, c0:c0 + cs], p.astype(BF16))
        if m is None:
            l, acc = p_rows, pv
        else:
            alpha = jnp.exp2(m - m_new)
            l, acc = alpha * l + p_rows, alpha * acc + pv
        m = m_new
    return (acc / l.sum(axis=0, keepdims=True)).T


def _attn_kernel(*refs, n_lat, ck, n_cast):
    q_ref, kl_ref, kc_ref, vl_ref, vc_ref = refs[:5]
    o_ref, s_ref = refs[5 + n_cast], refs[-1]
    qi = pl.program_id(2)
    tq = q_ref.shape[0]


    for src, dst in zip(refs[5:5 + n_cast], refs[6 + n_cast:6 + 2 * n_cast]):
        dst[...] = src[...].astype(dst.dtype)

    def run(segs):
        hs = range(ATTN_GROUP)
        q = jnp.concatenate([q_ref[:, h * HEAD_DIM:(h + 1) * HEAD_DIM] for h in hs], axis=0)
        o = _softmax_pv(q, segs, ck, s_ref)
        for h in hs:
            o_ref[:, h * HEAD_DIM:(h + 1) * HEAD_DIM] = o[h * tq:(h + 1) * tq].astype(BF16)

    @pl.when(qi < n_lat)
    def _():
        run([(kl_ref, vl_ref), (kc_ref, vc_ref)])

    @pl.when(qi >= n_lat)
    def _():
        run([(kc_ref, vc_ref)])


def _cast_rows(total, steps):
    r = next(r for r in range(16, total + 1, 16) if total % r == 0 and total // r <= steps)
    return r


def attention(q, k, vt, weights, layer, *, batch, t_lat, t_ctx, ctx_queries=True, tq=256, ck=512):
    n = q.shape[0]
    n_lat = t_lat // tq
    n_ctx = t_ctx // tq if ctx_queries else 0
    nq = n_lat + n_ctx
    gw = ATTN_GROUP * HEAD_DIM
    ctx0 = batch * t_lat
    steps = batch * ATTN_KV_HEADS * nq

    def qmap(b, j, i):
        return (jnp.where(i < n_lat, b * n_lat + i, ctx0 // tq + b * (t_ctx // tq) + (i - n_lat)), j)

    def cast_specs(w):
        rows = _cast_rows(w.shape[1], steps)
        last = w.shape[1] // rows - 1

        def blk(b, j, i):
            return jnp.minimum((b * ATTN_KV_HEADS + j) * nq + i, last)

        return (pl.BlockSpec((None, rows, w.shape[2]), lambda b, j, i: (layer, blk(b, j, i), 0)),
                pl.BlockSpec((rows, w.shape[2]), lambda b, j, i: (blk(b, j, i), 0)))

    specs = [cast_specs(w) for w in weights]
    return pl.pallas_call(
        functools.partial(_attn_kernel, n_lat=n_lat, ck=ck, n_cast=len(weights)),
        out_shape=(jax.ShapeDtypeStruct((n, ATTN_HEADS * HEAD_DIM), BF16),
                   *[jax.ShapeDtypeStruct(w.shape[1:], BF16) for w in weights]),
        grid=(batch, ATTN_KV_HEADS, nq),
        in_specs=[
            pl.BlockSpec((tq, gw), qmap),
            pl.BlockSpec((t_lat, HEAD_DIM), lambda b, j, i: (b, j)),
            pl.BlockSpec((t_ctx, HEAD_DIM), lambda b, j, i: (ctx0 // t_ctx + b, j)),
            pl.BlockSpec((HEAD_DIM, t_lat), lambda b, j, i: (j, b)),
            pl.BlockSpec((HEAD_DIM, t_ctx), lambda b, j, i: (j, ctx0 // t_ctx + b)),
            *[s[0] for s in specs],
        ],
        out_specs=(pl.BlockSpec((tq, gw), qmap), *[s[1] for s in specs]),
        scratch_shapes=[pltpu.VMEM((2, ck, ATTN_GROUP * tq), F32)],
        compiler_params=_cparams(("arbitrary", "arbitrary", "arbitrary")),
        name="attention",
    )(q, k, k, vt, vt, *weights)


def _hgrn_chunk(hq, z, v, lb, st, *, reverse):
    c = hq.shape[0]
    sig = jax.nn.sigmoid(z)
    f = lb + (1.0 - lb) * sig
    logf = jnp.log(jnp.maximum(f, F_MIN)) * (1.0 / jnp.log(2.0))
    kk = (1.0 - lb) * (1.0 - sig)
    q = _silu(hq)

    row = lax.broadcasted_iota(jnp.int32, (c, c), 0)
    col = lax.broadcasted_iota(jnp.int32, (c, c), 1)
    rid = lax.broadcasted_iota(jnp.int32, (c, HG_D), 0)
    later = (row < col) if reverse else (row > col)

    a = jnp.where(row == col, _dot_nt(q.astype(BF16), kk.astype(BF16)), 0.0)
    cum, tot = logf, logf
    d, ld = 1, 0
    while d < c:
        mask = jnp.logical_and(((row ^ col) >> ld) == 1, later)
        if d < SUBLANES:
            second = (rid & d) != 0
            sel = jnp.logical_not(second) if reverse else second
            x = jnp.where(sel, q, kk) * jnp.exp2(jnp.where(sel, cum, tot - cum))
            t3 = tot.reshape(c // SUBLANES, SUBLANES, HG_D)
            up = pltpu.roll(t3, d, 1).reshape(c, HG_D)
            partner = up if 2 * d == SUBLANES else jnp.where(
                second, up, pltpu.roll(t3, SUBLANES - d, 1).reshape(c, HG_D))
            cum = cum + jnp.where(sel, partner, 0.0)
            tot = tot + partner
        else:
            def split(y):
                y4 = y.reshape(c // (2 * d), 2, d, HG_D)
                return (y4[:, 1], y4[:, 0]) if reverse else (y4[:, 0], y4[:, 1])

            def join(early, late):
                pair = [late, early] if reverse else [early, late]
                return jnp.stack(pair, axis=1).reshape(c, HG_D)

            (cum_e, cum_l), (tot_e, tot_l) = split(cum), split(tot)
            x = join(split(kk)[0] * jnp.exp2(tot_e - cum_e), split(q)[1] * jnp.exp2(cum_l))
            both = tot_e + tot_l
            cum, tot = join(cum_e, cum_l + tot_e), join(both, both)
        xb = x.astype(BF16)
        a = jnp.where(mask, _dot_nt(xb, xb), a)
        d, ld = d * 2, ld + 1

    qd = (q * jnp.exp2(cum)).astype(BF16)
    kd = (kk * jnp.exp2(tot - cum)).astype(BF16)
    vb = v.astype(BF16)
    o = _dot(a.astype(BF16), vb) + _dot_nt(qd, st.astype(BF16))
    st_new = st * jnp.exp2(tot[0:1, :]) + _dot(vb.T, kd)
    return o, st_new


def _hgrn_kernel(*refs, reverse, finalize):
    if finalize:
        hq_ref, hf_ref, hi_ref, lb_ref, of_ref, gt_ref, g_ref, o_ref, st_ref = refs
    else:
        hq_ref, hf_ref, hi_ref, lb_ref, o_ref, st_ref = refs

    @pl.when(pl.program_id(1) == 0)
    def _():
        st_ref[...] = jnp.zeros_like(st_ref)

    starts = range(0, hq_ref.shape[0], HG_CHUNK)
    for r in (reversed(starts) if reverse else starts):
        rs = slice(r, r + HG_CHUNK)
        for h in range(HG_HEADS):
            sl = slice(h * HG_D, (h + 1) * HG_D)
            o, st_new = _hgrn_chunk(hq_ref[rs, sl], hf_ref[rs, sl], hi_ref[rs, sl], lb_ref[:, sl],
                                    st_ref[h], reverse=reverse)
            st_ref[h] = st_new
            if finalize:
                y = _rms(o + of_ref[rs, sl], g_ref[...])
                o_ref[rs, sl] = (y * _silu(gt_ref[rs, sl])).astype(o_ref.dtype)
            else:
                o_ref[rs, sl] = o


def hgrn_scan(p, lb, *, batch, t_lat, t_ctx, col0, reverse, fwd_out=None, gain=None,
              chunks_per_step=2):
    n = p.shape[0]
    c = HG_CHUNK * chunks_per_step
    assert t_lat % c == 0 and t_ctx % c == 0
    w = HG_HEADS * HG_D
    n_lat, n_ctx = t_lat // c, t_ctx // c
    ctx0 = batch * n_lat

    def rows(b, s):
        if reverse:
            return jnp.where(s < n_ctx, ctx0 + b * n_ctx + (n_ctx - 1 - s),
                             b * n_lat + (n_lat - 1 - (s - n_ctx)))
        return jnp.where(s < n_ctx, ctx0 + b * n_ctx + s, b * n_lat + (s - n_ctx))

    def spec(cb):
        return pl.BlockSpec((c, w), lambda b, s: (rows(b, s), cb))

    finalize = fwd_out is not None
    in_specs = [spec(col0), spec(col0 + (2 if reverse else 1)), spec(col0 + 3),
                pl.BlockSpec((1, w), lambda b, s: (0, 0))]
    args = [p, p, p, lb.reshape(1, w)]
    if finalize:
        in_specs += [spec(0), spec(col0 + 4), pl.BlockSpec((1, HG_D), lambda b, s: (0, 0))]
        args += [fwd_out, p, gain.reshape(1, HG_D)]
    return pl.pallas_call(
        functools.partial(_hgrn_kernel, reverse=reverse, finalize=finalize),
        out_shape=jax.ShapeDtypeStruct((n, w), BF16 if finalize else F32),
        grid=(batch, n_lat + n_ctx),
        in_specs=in_specs,
        out_specs=spec(0),
        scratch_shapes=[pltpu.VMEM((HG_HEADS, HG_D, HG_D), F32)],
        compiler_params=_cparams(("parallel", "arbitrary")),
        name="hgrn_bwd" if reverse else "hgrn_fwd",
    )(*args)


def _sg_kernel(u_ref, v_ref, g_ref, w_ref, b_ref, o_ref):
    for r in range(0, u_ref.shape[0], SG_CHUNK):
        rs = slice(r, r + SG_CHUNK)
        for g in range(SG_GROUPS):
            sl = slice(g * SG_DIM, (g + 1) * SG_DIM)
            vn = _rms(jax.nn.gelu(v_ref[rs, sl]), g_ref[:, sl])
            mixed = _dot(w_ref[g], vn.astype(BF16)) + b_ref[:, sl]
            o_ref[rs, sl] = (jax.nn.gelu(u_ref[rs, sl]) * mixed).astype(BF16)


def spatial_gate(p, g, w, bias_full, *, col0, n_rows, tm):
    sw = SG_GROUPS * SG_DIM
    assert n_rows % tm == 0
    return pl.pallas_call(
        _sg_kernel,
        out_shape=jax.ShapeDtypeStruct((n_rows, sw), BF16),
        grid=(n_rows // tm,),
        in_specs=[
            pl.BlockSpec((tm, sw), lambda i: (i, col0)),
            pl.BlockSpec((tm, sw), lambda i: (i, col0 + 1)),
            pl.BlockSpec((1, sw), lambda i: (0, 0)),
            pl.BlockSpec((SG_GROUPS, SG_CHUNK, SG_CHUNK), lambda i: (0, 0, 0)),
            pl.BlockSpec((SG_CHUNK, sw), lambda i: (0, 0)),
        ],
        out_specs=pl.BlockSpec((tm, sw), lambda i: (i, 0)),
        compiler_params=_cparams(("parallel",)),
        name="spatial_gate",
    )(p, p, g.reshape(1, sw), w, bias_full)


def _out_kernel(a_ref, h_ref, s_ref, wa_ref, wh_ref, ws_ref, x_ref, gate_ref, o_ref):
    acc = _dot(a_ref[...], wa_ref[...]) + _dot(h_ref[...], wh_ref[...]) + _dot(s_ref[...], ws_ref[...])
    o_ref[...] = x_ref[...] + gate_ref[0] * acc


def out_proj(attn, hg, sg, w, layer, x, mod, *, tile_mod, n_rows, tm, tn=1024):
    n, d = n_rows, x.shape[1]
    wa, wh, ws = attn.shape[1], hg.shape[1], sg.shape[1]
    assert wa % wh == 0 and wh == ws and n % tm == 0 and d % tn == 0
    return pl.pallas_call(
        _out_kernel,
        out_shape=jax.ShapeDtypeStruct((n, d), F32),
        grid=(n // tm, d // tn),
        in_specs=[
            pl.BlockSpec((tm, wa), lambda i, j: (i, 0)),
            pl.BlockSpec((tm, wh), lambda i, j: (i, 0)),
            pl.BlockSpec((tm, ws), lambda i, j: (i, 0)),
            pl.BlockSpec((None, wa, tn), lambda i, j: (layer, 0, j)),
            pl.BlockSpec((None, wh, tn), lambda i, j: (layer, wa // wh, j)),
            pl.BlockSpec((None, ws, tn), lambda i, j: (layer, wa // wh + 1, j)),
            pl.BlockSpec((tm, tn), lambda i, j: (i, j)),
            pl.BlockSpec((1, 1, tn), lambda i, j: (tile_mod(i) * N_MOD + 2, 0, j)),
        ],
        out_specs=pl.BlockSpec((tm, tn), lambda i, j: (i, j)),
        compiler_params=_cparams(("parallel", "parallel")),
        name="out_proj",
    )(attn, hg, sg, w, w, w, x, mod)


def _ffn_kernel(*refs, tile_seq, n_long, has_short, final, rows, chunk):
    (x_ref, xp_ref, xn_ref, g_ref, sh_ref, sc_ref, gate_ref,
     wg_ref, wv_ref, cwg_ref, cwv_ref, cbg_ref, cbv_ref, wd_ref) = refs[:14]
    gf_ref = refs[14] if final else None
    o_ref, h_ref, u_ref = refs[-3:]
    i, j = pl.program_id(0), pl.program_id(1)
    tm = x_ref.shape[0]
    halo = xp_ref.shape[0]

    def norm_mod(x):
        return _norm_mod(x, g_ref[...], sc_ref[0], sh_ref[0])

    @pl.when(j == 0)
    def _():
        for r in range(0, tm, rows):
            h_ref[r:r + rows, :] = norm_mod(x_ref[r:r + rows, :])
        h_ref[tm:tm + halo, :] = norm_mod(xp_ref[...])
        h_ref[tm + halo:tm + 2 * halo, :] = norm_mod(xn_ref[...])
        o_ref[...] = jnp.zeros_like(o_ref)

    chunks = [slice(c0, c0 + chunk) for c0 in range(0, wd_ref.shape[0], chunk)]
    rid = lax.broadcasted_iota(jnp.int32, (tm, 1), 0)

    def step(interior):
        seq = tile_seq(i)
        tile_starts = ((i * tm) & (seq - 1)) == 0
        tile_ends = (((i + 1) * tm) & (seq - 1)) == 0
        if interior:
            pos = (i * tm + rid) & (seq - 1)
            first, last = pos == 0, pos == seq - 1

        for c, cs in enumerate(chunks):
            u_ref[c, 0] = _dot(h_ref[...], wg_ref[:, cs])
            u_ref[c, 1] = _dot(h_ref[...], wv_ref[:, cs])

        def conv(c, k, cw_ref, cb_ref, cs):
            u = u_ref[c, k, 0:tm, :]
            prev_row = jnp.where(tile_starts, 0.0, u_ref[c, k, tm + halo - 1:tm + halo, :])
            next_row = jnp.where(tile_ends, 0.0, u_ref[c, k, tm + halo:tm + halo + 1, :])
            before = jnp.where(rid == 0, prev_row, pltpu.roll(u, 1, 0))
            after = jnp.where(rid == tm - 1, next_row, pltpu.roll(u, tm - 1, 0))
            if interior:
                before = jnp.where(first, 0.0, before)
                after = jnp.where(last, 0.0, after)
            return (cb_ref[:, cs] + before * cw_ref[0:1, cs] + u * cw_ref[1:2, cs]
                    + after * cw_ref[2:3, cs])

        for c, cs in enumerate(chunks):
            act = _silu(conv(c, 0, cwg_ref, cbg_ref, cs)) * conv(c, 1, cwv_ref, cbv_ref, cs)
            o_ref[...] += _dot(act.astype(BF16), wd_ref[cs, :])

    if has_short:
        pl.when(i < n_long)(functools.partial(step, False))
        pl.when(i >= n_long)(functools.partial(step, True))
    else:
        step(False)

    @pl.when(j == pl.num_programs(1) - 1)
    def _():
        def finish(r, carry):
            rs = pl.ds(pl.multiple_of(r * rows, rows), rows)
            y = x_ref[rs, :] + gate_ref[0] * o_ref[rs, :]
            o_ref[rs, :] = _rms(y, gf_ref[...]) if final else y
            return carry

        lax.fori_loop(0, tm // rows, finish, 0)


def conv_ffn(x, g, mod, w_up, conv_w, conv_b, w_down, *, tile_mod, t_lat, t_ctx, n_lat,
             n_rows, tm, tf=512, final_gain=None):
    n, d = x.shape
    dff = w_down.shape[0]
    assert n_rows % tm == 0 and n_lat % tm == 0 and t_lat % tm == 0 and dff % tf == 0
    assert n_rows == n_lat or tm % t_ctx == 0 or t_ctx % tm == 0
    nj = dff // tf
    halo = 8
    chunk = min(256, tf)
    hb = tm // halo
    nhb = n // halo
    n_long = n_lat // tm
    cb = conv_b.reshape(1, 2 * dff)
    final = final_gain is not None
    in_specs = [
        pl.BlockSpec((tm, d), lambda i, j: (i, 0)),
        pl.BlockSpec((halo, d), lambda i, j: (jnp.maximum(i * hb - 1, 0), 0)),
        pl.BlockSpec((halo, d), lambda i, j: (jnp.minimum((i + 1) * hb, nhb - 1), 0)),
        pl.BlockSpec((1, d), lambda i, j: (0, 0)),
        pl.BlockSpec((1, 1, d), lambda i, j: (tile_mod(i) * N_MOD + 3, 0, 0)),
        pl.BlockSpec((1, 1, d), lambda i, j: (tile_mod(i) * N_MOD + 4, 0, 0)),
        pl.BlockSpec((1, 1, d), lambda i, j: (tile_mod(i) * N_MOD + 5, 0, 0)),
        pl.BlockSpec((d, tf), lambda i, j: (0, j)),
        pl.BlockSpec((d, tf), lambda i, j: (0, nj + j)),
        pl.BlockSpec((CONV_W, tf), lambda i, j: (0, j)),
        pl.BlockSpec((CONV_W, tf), lambda i, j: (0, nj + j)),
        pl.BlockSpec((1, tf), lambda i, j: (0, j)),
        pl.BlockSpec((1, tf), lambda i, j: (0, nj + j)),
        pl.BlockSpec((tf, d), lambda i, j: (j, 0)),
    ]
    args = [x, x, x, g.reshape(1, d), mod, mod, mod, w_up, w_up, conv_w, conv_w, cb, cb, w_down]
    if final:
        in_specs.append(pl.BlockSpec((1, d), lambda i, j: (0, 0)))
        args.append(final_gain.reshape(1, d))
    return pl.pallas_call(
        functools.partial(
            _ffn_kernel, tile_seq=lambda i: jnp.where(i < n_long, t_lat, t_ctx), n_long=n_long,
            has_short=n_rows > n_lat and t_ctx < tm, final=final, rows=min(tm, 256), chunk=chunk),
        out_shape=jax.ShapeDtypeStruct((n_rows, d), F32),
        grid=(n_rows // tm, nj),
        in_specs=in_specs,
        out_specs=pl.BlockSpec((tm, d), lambda i, j: (i, 0)),
        scratch_shapes=[pltpu.VMEM((tm + 2 * halo, d), BF16),
                        pltpu.VMEM((tf // chunk, 2, tm + 2 * halo, chunk), F32)],
        compiler_params=_cparams(("parallel", "arbitrary")),
        name="conv_ffn",
    )(*args)


def _rope_tables(t_lat, tm):
    rows = t_lat // GRID_W
    row = jnp.repeat(jnp.arange(rows, dtype=F32), GRID_W)
    col = jnp.tile(jnp.arange(GRID_W, dtype=F32), rows)
    n_freq = HEAD_DIM // 4
    inv = ROPE_THETA ** (-jnp.arange(n_freq, dtype=F32) / n_freq)
    ang = jnp.concatenate([row[:, None] * inv, col[:, None] * inv], axis=-1)
    cos2 = jnp.repeat(jnp.cos(ang), 2, axis=-1)
    sin2 = jnp.repeat(jnp.sin(ang), 2, axis=-1) * jnp.tile(jnp.array([-1.0, 1.0], F32), HEAD_DIM // 2)
    cos2 = jnp.concatenate([cos2, jnp.ones((tm, HEAD_DIM), F32)], axis=0)
    sin2 = jnp.concatenate([sin2, jnp.zeros((tm, HEAD_DIM), F32)], axis=0)
    return cos2, sin2


def _lower_bounds(lb_param):
    p = jax.nn.softmax(lb_param.astype(F32), axis=1)
    return jnp.cumsum(p, axis=1) - p[:, :1]


def kernel(x, c, ctx, c_ctx, w_ada, b_ada, norm1_g, w_in, q_norm_g, k_norm_g, hg_lower_bounds,
           hg_norm_g, sg_norm_g, sg_w, sg_b, w_out, norm2_g, w_up, conv_w, conv_b, w_down,
           final_norm_g):
    batch, t_lat, d = x.shape
    t_ctx = ctx.shape[1]
    depth = w_in.shape[0]
    n_lat, n_ctx = batch * t_lat, batch * t_ctx
    assert t_lat & (t_lat - 1) == 0 and t_ctx & (t_ctx - 1) == 0

    tm = min(1024, n_ctx)
    assert t_lat % tm == 0 and n_ctx % tm == 0

    def tile_mod(i):
        return jnp.minimum(i // (t_lat // tm), batch)

    def tile_rope(i):
        return jnp.where(i < n_lat // tm, i % (t_lat // tm), t_lat // tm)

    xs = jnp.concatenate([x.reshape(n_lat, d), ctx.reshape(n_ctx, d)], axis=0)
    cin = jnp.concatenate([c, c_ctx[None, :], jnp.zeros((8 - batch - 1, d), F32)], axis=0)
    mods = ada_table(cin, w_ada, b_ada)[:, :batch + 1, :]
    mods = mods.reshape(depth, (batch + 1) * N_MOD, 1, d)
    cos2, sin2 = _rope_tables(t_lat, tm)
    lbs = _lower_bounds(hg_lower_bounds)
    qw = ATTN_HEADS * HEAD_DIM
    kvw = ATTN_KV_HEADS * HEAD_DIM
    hg_col0 = (qw + 2 * kvw) // (HG_HEADS * HG_D)
    sg_col0 = hg_col0 + 5

    w_in, w_out = w_in.astype(BF16), w_out.astype(BF16)

    for l in range(depth):
        mod = mods[l]
        last = l == depth - 1
        n_rows = n_lat if last else n_lat + n_ctx
        p = in_proj(xs, norm1_g[l], mod, w_in, l, tile_mod=tile_mod, tm=tm)
        q, k, vt = qk_prep(p, cos2, sin2, q_norm_g[l], k_norm_g[l], tile_rope=tile_rope, tm=tm)
        attn, w_up_l, w_down_l = attention(q, k, vt, (w_up, w_down), l, batch=batch, t_lat=t_lat,
                                           t_ctx=t_ctx, ctx_queries=not last)
        o_f = hgrn_scan(p, lbs[0, l], batch=batch, t_lat=t_lat, t_ctx=t_ctx, col0=hg_col0,
                        reverse=False)
        hg = hgrn_scan(p, lbs[1, l], batch=batch, t_lat=t_lat, t_ctx=t_ctx, col0=hg_col0,
                       reverse=True, fwd_out=o_f, gain=hg_norm_g[l])
        bias_full = jnp.repeat(sg_b[l].T, SG_DIM, axis=1)
        sg = spatial_gate(p, sg_norm_g[l], sg_w[l].astype(BF16), bias_full, col0=sg_col0,
                          n_rows=n_rows, tm=tm)
        xs = out_proj(attn, hg, sg, w_out, l, xs, mod, tile_mod=tile_mod, n_rows=n_rows, tm=tm)
        xs = conv_ffn(xs, norm2_g[l], mod, w_up_l, conv_w[l], conv_b[l], w_down_l,
                      tile_mod=tile_mod, t_lat=t_lat, t_ctx=t_ctx, n_lat=n_lat, n_rows=n_rows,
                      tm=tm, final_gain=final_norm_g if last else None)

    return xs.reshape(batch, t_lat, d)
```

```python
import functools

import jax
import jax.numpy as jnp
from jax import lax
from jax.experimental import pallas as pl
from jax.experimental.pallas import tpu as pltpu

F32 = jnp.float32
BF16 = jnp.bfloat16

EPS = 1e-6
F_MIN = 1e-30
N_MOD = 6
HEAD_DIM = 128
ATTN_HEADS = 8
ATTN_KV_HEADS = 2
ATTN_GROUP = ATTN_HEADS // ATTN_KV_HEADS
ROPE_THETA = 10000.0
GRID_W = 64
HG_HEADS = 4
HG_D = 128
SG_GROUPS = 4
SG_DIM = 128
SG_CHUNK = 128
CONV_W = 3
LANE = 128
SUBLANES = 8
LOG2_E = 1.4426950408889634
HG_CHUNK = 128
VMEM_LIMIT = 62 * 1024 * 1024


def _cparams(sem):
    return pltpu.CompilerParams(dimension_semantics=sem, vmem_limit_bytes=VMEM_LIMIT)


def _dot(a, b):
    return jnp.dot(a, b, preferred_element_type=F32)


def _dot_nt(a, b):
    return lax.dot_general(a, b, (((1,), (1,)), ((), ())), preferred_element_type=F32)


def _rms(x, g):
    return x * lax.rsqrt(jnp.mean(x * x, axis=-1, keepdims=True) + EPS) * g


def _silu(x):
    return x * jax.nn.sigmoid(x)


def _ada_kernel(c_ref, w_ref, b_ref, o_ref):
    s = _silu(c_ref[...]).astype(BF16)
    o_ref[...] = _dot(s, w_ref[...].astype(BF16)) + b_ref[...]


def ada_table(cin, w_ada, b_ada, tn=1024):
    depth, d, n = w_ada.shape
    assert n % tn == 0
    return pl.pallas_call(
        _ada_kernel,
        out_shape=jax.ShapeDtypeStruct((depth, 8, n), F32),
        grid=(depth, n // tn),
        in_specs=[
            pl.BlockSpec((8, d), lambda l, j: (0, 0)),
            pl.BlockSpec((None, d, tn), lambda l, j: (l, 0, j)),
            pl.BlockSpec((None, 1, tn), lambda l, j: (l, 0, j)),
        ],
        out_specs=pl.BlockSpec((None, 8, tn), lambda l, j: (l, 0, j)),
        compiler_params=_cparams(("parallel", "parallel")),
        name="ada_table",
    )(cin, w_ada, b_ada.reshape(depth, 1, n))


def _norm_mod(x, g, scale, shift):
    r = lax.rsqrt(jnp.mean(x * x, axis=-1, keepdims=True) + EPS)
    return ((x * r) * (g * (1.0 + scale)) + shift).astype(BF16)


def _in_kernel(x_ref, g_ref, sh_ref, sc_ref, w_ref, o_ref, h_ref, *, rows):
    @pl.when(pl.program_id(1) == 0)
    def _():
        for r in range(0, x_ref.shape[0], rows):
            h_ref[r:r + rows, :] = _norm_mod(x_ref[r:r + rows, :], g_ref[...], sc_ref[0], sh_ref[0])

    o_ref[...] = _dot(h_ref[...], w_ref[...])


def in_proj(x, g, mod, w, *, tile_mod, tm, tn=1280):
    n, d = x.shape
    cols = w.shape[1]
    assert n % tm == 0 and cols % tn == 0
    return pl.pallas_call(
        functools.partial(_in_kernel, rows=min(tm, 256)),
        out_shape=jax.ShapeDtypeStruct((n, cols), F32),
        grid=(n // tm, cols // tn),
        in_specs=[
            pl.BlockSpec((tm, d), lambda i, j: (i, 0)),
            pl.BlockSpec((1, d), lambda i, j: (0, 0)),
            pl.BlockSpec((1, 1, d), lambda i, j: (tile_mod(i) * N_MOD + 0, 0, 0)),
            pl.BlockSpec((1, 1, d), lambda i, j: (tile_mod(i) * N_MOD + 1, 0, 0)),
            pl.BlockSpec((d, tn), lambda i, j: (0, j)),
        ],
        out_specs=pl.BlockSpec((tm, tn), lambda i, j: (i, j)),
        scratch_shapes=[pltpu.VMEM((tm, d), BF16)],
        compiler_params=_cparams(("parallel", "arbitrary")),
        name="in_proj",
    )(x, g.reshape(1, d), mod, mod, w)


def _rope(y, cos2, sin2):
    lane = lax.broadcasted_iota(jnp.int32, y.shape, 1)
    swapped = jnp.where((lane & 1) == 0, pltpu.roll(y, LANE - 1, 1), pltpu.roll(y, 1, 1))
    return y * cos2 + swapped * sin2


def _qk_kernel(q_ref, kv_ref, cos_ref, sin_ref, qg_ref, kg_ref, qo_ref, ko_ref, vo_ref):
    cos2, sin2 = cos_ref[...], sin_ref[...]
    scale = HEAD_DIM ** -0.5 * LOG2_E
    for h in range(ATTN_HEADS):
        sl = slice(h * HEAD_DIM, (h + 1) * HEAD_DIM)
        y = _rms(q_ref[:, sl], qg_ref[...])
        qo_ref[:, sl] = (_rope(y, cos2, sin2) * scale).astype(BF16)
    for h in range(ATTN_KV_HEADS):
        sl = slice(h * HEAD_DIM, (h + 1) * HEAD_DIM)
        y = _rms(kv_ref[:, sl], kg_ref[...])
        ko_ref[:, sl] = _rope(y, cos2, sin2).astype(BF16)
    kvw = ATTN_KV_HEADS * HEAD_DIM
    vo_ref[...] = kv_ref[:, kvw:2 * kvw].T.astype(BF16)


def qk_prep(p, cos2, sin2, qg, kg, *, tile_rope, tm):
    n = p.shape[0]
    qw = ATTN_HEADS * HEAD_DIM
    kvw = ATTN_KV_HEADS * HEAD_DIM
    return pl.pallas_call(
        _qk_kernel,
        out_shape=(jax.ShapeDtypeStruct((n, qw), BF16),
                   jax.ShapeDtypeStruct((n, kvw), BF16),
                   jax.ShapeDtypeStruct((kvw, n), BF16)),
        grid=(n // tm,),
        in_specs=[
            pl.BlockSpec((tm, qw), lambda i: (i, 0)),
            pl.BlockSpec((tm, 2 * kvw), lambda i: (i, qw // (2 * kvw))),
            pl.BlockSpec((tm, HEAD_DIM), lambda i: (tile_rope(i), 0)),
            pl.BlockSpec((tm, HEAD_DIM), lambda i: (tile_rope(i), 0)),
            pl.BlockSpec((1, HEAD_DIM), lambda i: (0, 0)),
            pl.BlockSpec((1, HEAD_DIM), lambda i: (0, 0)),
        ],
        out_specs=(pl.BlockSpec((tm, qw), lambda i: (i, 0)),
                   pl.BlockSpec((tm, kvw), lambda i: (i, 0)),
                   pl.BlockSpec((kvw, tm), lambda i: (0, i))),
        compiler_params=_cparams(("parallel",)),
        name="qk_prep",
    )(p, p, cos2, sin2, qg.reshape(1, HEAD_DIM), kg.reshape(1, HEAD_DIM))


def _softmax_pv(q, segs, ck, s_ref):
    mq = q.shape[0]
    chunks = [(k_ref, vt_ref, c0, min(ck, k_ref.shape[0]))
              for k_ref, vt_ref in segs for c0 in range(0, k_ref.shape[0], min(ck, k_ref.shape[0]))]

    def scores(i):
        k_ref, _, c0, cs = chunks[i]
        s_ref[i % 2, 0:cs, :] = _dot_nt(k_ref[c0:c0 + cs, :], q)

    scores(0)
    m = l = acc = None
    for i, (_, vt_ref, c0, cs) in enumerate(chunks):
        if i + 1 < len(chunks):
            scores(i + 1)
        s = s_ref[i % 2, 0:cs, :]
        m_c = s.reshape(cs // SUBLANES, SUBLANES, mq).max(axis=0).max(axis=0, keepdims=True)
        m_new = m_c if m is None else jnp.maximum(m, m_c)
        p = jnp.exp2(s - m_new)
        p_rows = p.reshape(cs // SUBLANES, SUBLANES, mq).sum(axis=0)
        pv = _dot(vt_ref[:, c0:c0 + cs], p.astype(BF16))
        if m is None:
            l, acc = p_rows, pv
        else:
            alpha = jnp.exp2(m - m_new)
            l, acc = alpha * l + p_rows, alpha * acc + pv
        m = m_new
    return (acc / l.sum(axis=0, keepdims=True)).T


def _attn_kernel(*refs, n_lat, ck, n_cast):
    q_ref, kl_ref, kc_ref, vl_ref, vc_ref = refs[:5]
    o_ref, s_ref = refs[5 + n_cast], refs[-1]
    qi = pl.program_id(2)
    tq = q_ref.shape[0]


    for src, dst in zip(refs[5:5 + n_cast], refs[6 + n_cast:6 + 2 * n_cast]):
        dst[...] = src[...].astype(dst.dtype)

    def run(segs):
        hs = range(ATTN_GROUP)
        q = jnp.concatenate([q_ref[:, h * HEAD_DIM:(h + 1) * HEAD_DIM] for h in hs], axis=0)
        o = _softmax_pv(q, segs, ck, s_ref)
        for h in hs:
            o_ref[:, h * HEAD_DIM:(h + 1) * HEAD_DIM] = o[h * tq:(h + 1) * tq].astype(BF16)

    @pl.when(qi < n_lat)
    def _():
        run([(kl_ref, vl_ref), (kc_ref, vc_ref)])

    @pl.when(qi >= n_lat)
    def _():
        run([(kc_ref, vc_ref)])


def _cast_rows(total, steps):
    r = next(r for r in range(16, total + 1, 16) if total % r == 0 and total // r <= steps)
    return r


def attention(q, k, vt, weights, *, batch, t_lat, t_ctx, ctx_queries=True, tq=256, ck=512):
    n = q.shape[0]
    n_lat = t_lat // tq
    n_ctx = t_ctx // tq if ctx_queries else 0
    nq = n_lat + n_ctx
    gw = ATTN_GROUP * HEAD_DIM
    ctx0 = batch * t_lat
    steps = batch * ATTN_KV_HEADS * nq

    def qmap(b, j, i):
        return (jnp.where(i < n_lat, b * n_lat + i, ctx0 // tq + b * (t_ctx // tq) + (i - n_lat)), j)

    def cast_specs(w, layer):
        rows = _cast_rows(w.shape[1], steps)
        last = w.shape[1] // rows - 1

        def blk(b, j, i):
            return jnp.minimum((b * ATTN_KV_HEADS + j) * nq + i, last)

        return (pl.BlockSpec((None, rows, w.shape[2]), lambda b, j, i: (layer, blk(b, j, i), 0)),
                pl.BlockSpec((rows, w.shape[2]), lambda b, j, i: (blk(b, j, i), 0)))

    specs = [cast_specs(w, layer) for w, layer in weights]
    weights = [w for w, _ in weights]
    return pl.pallas_call(
        functools.partial(_attn_kernel, n_lat=n_lat, ck=ck, n_cast=len(weights)),
        out_shape=(jax.ShapeDtypeStruct((n, ATTN_HEADS * HEAD_DIM), BF16),
                   *[jax.ShapeDtypeStruct(w.shape[1:], BF16) for w in weights]),
        grid=(batch, ATTN_KV_HEADS, nq),
        in_specs=[
            pl.BlockSpec((tq, gw), qmap),
            pl.BlockSpec((t_lat, HEAD_DIM), lambda b, j, i: (b, j)),
            pl.BlockSpec((t_ctx, HEAD_DIM), lambda b, j, i: (ctx0 // t_ctx + b, j)),
            pl.BlockSpec((HEAD_DIM, t_lat), lambda b, j, i: (j, b)),
            pl.BlockSpec((HEAD_DIM, t_ctx), lambda b, j, i: (j, ctx0 // t_ctx + b)),
            *[s[0] for s in specs],
        ],
        out_specs=(pl.BlockSpec((tq, gw), qmap), *[s[1] for s in specs]),
        scratch_shapes=[pltpu.VMEM((2, ck, ATTN_GROUP * tq), F32)],
        compiler_params=_cparams(("arbitrary", "arbitrary", "arbitrary")),
        name="attention",
    )(q, k, k, vt, vt, *weights)


def _hgrn_chunk(hq, z, v, lb, st, *, reverse):
    c = hq.shape[0]
    sig = jax.nn.sigmoid(z)
    f = lb + (1.0 - lb) * sig
    logf = jnp.log(jnp.maximum(f, F_MIN)) * (1.0 / jnp.log(2.0))
    kk = (1.0 - lb) * (1.0 - sig)
    q = _silu(hq)

    row = lax.broadcasted_iota(jnp.int32, (c, c), 0)
    col = lax.broadcasted_iota(jnp.int32, (c, c), 1)
    rid = lax.broadcasted_iota(jnp.int32, (c, HG_D), 0)
    later = (row < col) if reverse else (row > col)

    a = jnp.where(row == col, _dot_nt(q.astype(BF16), kk.astype(BF16)), 0.0)
    cum, tot = logf, logf
    d, ld = 1, 0
    while d < c:
        mask = jnp.logical_and(((row ^ col) >> ld) == 1, later)
        if d < SUBLANES:
            second = (rid & d) != 0
            sel = jnp.logical_not(second) if reverse else second
            x = jnp.where(sel, q, kk) * jnp.exp2(jnp.where(sel, cum, tot - cum))
            t3 = tot.reshape(c // SUBLANES, SUBLANES, HG_D)
            up = pltpu.roll(t3, d, 1).reshape(c, HG_D)
            partner = up if 2 * d == SUBLANES else jnp.where(
                second, up, pltpu.roll(t3, SUBLANES - d, 1).reshape(c, HG_D))
            cum = cum + jnp.where(sel, partner, 0.0)
            tot = tot + partner
        else:
            def split(y):
                y4 = y.reshape(c // (2 * d), 2, d, HG_D)
                return (y4[:, 1], y4[:, 0]) if reverse else (y4[:, 0], y4[:, 1])

            def join(early, late):
                pair = [late, early] if reverse else [early, late]
                return jnp.stack(pair, axis=1).reshape(c, HG_D)

            (cum_e, cum_l), (tot_e, tot_l) = split(cum), split(tot)
            x = join(split(kk)[0] * jnp.exp2(tot_e - cum_e), split(q)[1] * jnp.exp2(cum_l))
            both = tot_e + tot_l
            cum, tot = join(cum_e, cum_l + tot_e), join(both, both)
        xb = x.astype(BF16)
        a = jnp.where(mask, _dot_nt(xb, xb), a)
        d, ld = d * 2, ld + 1

    qd = (q * jnp.exp2(cum)).astype(BF16)
    kd = (kk * jnp.exp2(tot - cum)).astype(BF16)
    vb = v.astype(BF16)
    o = _dot(a.astype(BF16), vb) + _dot_nt(qd, st.astype(BF16))
    st_new = st * jnp.exp2(tot[0:1, :]) + _dot(vb.T, kd)
    return o, st_new


def _hgrn_kernel(*refs, reverse, finalize):
    if finalize:
        hq_ref, hf_ref, hi_ref, lb_ref, of_ref, gt_ref, g_ref, o_ref, st_ref = refs
    else:
        hq_ref, hf_ref, hi_ref, lb_ref, o_ref, st_ref = refs

    @pl.when(pl.program_id(1) == 0)
    def _():
        st_ref[...] = jnp.zeros_like(st_ref)

    starts = range(0, hq_ref.shape[0], HG_CHUNK)
    for r in (reversed(starts) if reverse else starts):
        rs = slice(r, r + HG_CHUNK)
        for h in range(HG_HEADS):
            sl = slice(h * HG_D, (h + 1) * HG_D)
            o, st_new = _hgrn_chunk(hq_ref[rs, sl], hf_ref[rs, sl], hi_ref[rs, sl], lb_ref[:, sl],
                                    st_ref[h], reverse=reverse)
            st_ref[h] = st_new
            if finalize:
                y = _rms(o + of_ref[rs, sl], g_ref[...])
                o_ref[rs, sl] = (y * _silu(gt_ref[rs, sl])).astype(o_ref.dtype)
            else:
                o_ref[rs, sl] = o


def hgrn_scan(p, lb, *, batch, t_lat, t_ctx, col0, reverse, fwd_out=None, gain=None,
              chunks_per_step=2):
    n = p.shape[0]
    c = HG_CHUNK * chunks_per_step
    assert t_lat % c == 0 and t_ctx % c == 0
    w = HG_HEADS * HG_D
    n_lat, n_ctx = t_lat // c, t_ctx // c
    ctx0 = batch * n_lat

    def rows(b, s):
        if reverse:
            return jnp.where(s < n_ctx, ctx0 + b * n_ctx + (n_ctx - 1 - s),
                             b * n_lat + (n_lat - 1 - (s - n_ctx)))
        return jnp.where(s < n_ctx, ctx0 + b * n_ctx + s, b * n_lat + (s - n_ctx))

    def spec(cb):
        return pl.BlockSpec((c, w), lambda b, s: (rows(b, s), cb))

    finalize = fwd_out is not None
    in_specs = [spec(col0), spec(col0 + (2 if reverse else 1)), spec(col0 + 3),
                pl.BlockSpec((1, w), lambda b, s: (0, 0))]
    args = [p, p, p, lb.reshape(1, w)]
    if finalize:
        in_specs += [spec(0), spec(col0 + 4), pl.BlockSpec((1, HG_D), lambda b, s: (0, 0))]
        args += [fwd_out, p, gain.reshape(1, HG_D)]
    return pl.pallas_call(
        functools.partial(_hgrn_kernel, reverse=reverse, finalize=finalize),
        out_shape=jax.ShapeDtypeStruct((n, w), BF16 if finalize else F32),
        grid=(batch, n_lat + n_ctx),
        in_specs=in_specs,
        out_specs=spec(0),
        scratch_shapes=[pltpu.VMEM((HG_HEADS, HG_D, HG_D), F32)],
        compiler_params=_cparams(("parallel", "arbitrary")),
        name="hgrn_bwd" if reverse else "hgrn_fwd",
    )(*args)


def _sg_kernel(u_ref, v_ref, g_ref, w_ref, b_ref, o_ref):
    for r in range(0, u_ref.shape[0], SG_CHUNK):
        rs = slice(r, r + SG_CHUNK)
        for g in range(SG_GROUPS):
            sl = slice(g * SG_DIM, (g + 1) * SG_DIM)
            vn = _rms(jax.nn.gelu(v_ref[rs, sl]), g_ref[:, sl])
            mixed = _dot(w_ref[g], vn.astype(BF16)) + b_ref[:, sl]
            o_ref[rs, sl] = (jax.nn.gelu(u_ref[rs, sl]) * mixed).astype(BF16)


def spatial_gate(p, g, w, bias_full, *, col0, n_rows, tm):
    sw = SG_GROUPS * SG_DIM
    assert n_rows % tm == 0
    return pl.pallas_call(
        _sg_kernel,
        out_shape=jax.ShapeDtypeStruct((n_rows, sw), BF16),
        grid=(n_rows // tm,),
        in_specs=[
            pl.BlockSpec((tm, sw), lambda i: (i, col0)),
            pl.BlockSpec((tm, sw), lambda i: (i, col0 + 1)),
            pl.BlockSpec((1, sw), lambda i: (0, 0)),
            pl.BlockSpec((SG_GROUPS, SG_CHUNK, SG_CHUNK), lambda i: (0, 0, 0)),
            pl.BlockSpec((SG_CHUNK, sw), lambda i: (0, 0)),
        ],
        out_specs=pl.BlockSpec((tm, sw), lambda i: (i, 0)),
        compiler_params=_cparams(("parallel",)),
        name="spatial_gate",
    )(p, p, g.reshape(1, sw), w, bias_full)


def _out_kernel(a_ref, h_ref, s_ref, wa_ref, wh_ref, ws_ref, x_ref, gate_ref, o_ref):
    acc = _dot(a_ref[...], wa_ref[...]) + _dot(h_ref[...], wh_ref[...]) + _dot(s_ref[...], ws_ref[...])
    o_ref[...] = x_ref[...] + gate_ref[0] * acc


def out_proj(attn, hg, sg, w, x, mod, *, tile_mod, n_rows, tm, tn=2048):
    n, d = n_rows, x.shape[1]
    tn = min(tn, d)
    wa, wh, ws = attn.shape[1], hg.shape[1], sg.shape[1]
    assert wa % wh == 0 and wh == ws and n % tm == 0 and d % tn == 0
    return pl.pallas_call(
        _out_kernel,
        out_shape=jax.ShapeDtypeStruct((n, d), F32),
        grid=(n // tm, d // tn),
        in_specs=[
            pl.BlockSpec((tm, wa), lambda i, j: (i, 0)),
            pl.BlockSpec((tm, wh), lambda i, j: (i, 0)),
            pl.BlockSpec((tm, ws), lambda i, j: (i, 0)),
            pl.BlockSpec((wa, tn), lambda i, j: (0, j)),
            pl.BlockSpec((wh, tn), lambda i, j: (wa // wh, j)),
            pl.BlockSpec((ws, tn), lambda i, j: (wa // wh + 1, j)),
            pl.BlockSpec((tm, tn), lambda i, j: (i, j)),
            pl.BlockSpec((1, 1, tn), lambda i, j: (tile_mod(i) * N_MOD + 2, 0, j)),
        ],
        out_specs=pl.BlockSpec((tm, tn), lambda i, j: (i, j)),
        compiler_params=_cparams(("parallel", "parallel")),
        name="out_proj",
    )(attn, hg, sg, w, w, w, x, mod)


def _ffn_kernel(*refs, tile_seq, n_long, has_short, final, rows, chunk):
    (x_ref, xp_ref, xn_ref, g_ref, sh_ref, sc_ref, gate_ref,
     wg_ref, wv_ref, cwg_ref, cwv_ref, cbg_ref, cbv_ref, wd_ref) = refs[:14]
    gf_ref = refs[14] if final else None
    o_ref, h_ref, u_ref = refs[-3:]
    i, j = pl.program_id(0), pl.program_id(1)
    tm = x_ref.shape[0]
    halo = xp_ref.shape[0]

    def norm_mod(x):
        return _norm_mod(x, g_ref[...], sc_ref[0], sh_ref[0])

    @pl.when(j == 0)
    def _():
        for r in range(0, tm, rows):
            h_ref[r:r + rows, :] = norm_mod(x_ref[r:r + rows, :])
        h_ref[tm:tm + halo, :] = norm_mod(xp_ref[...])
        h_ref[tm + halo:tm + 2 * halo, :] = norm_mod(xn_ref[...])
        o_ref[...] = jnp.zeros_like(o_ref)

    chunks = [slice(c0, c0 + chunk) for c0 in range(0, wd_ref.shape[0], chunk)]
    rid = lax.broadcasted_iota(jnp.int32, (tm, 1), 0)

    def step(interior):
        seq = tile_seq(i)
        tile_starts = ((i * tm) & (seq - 1)) == 0
        tile_ends = (((i + 1) * tm) & (seq - 1)) == 0
        if interior:
            pos = (i * tm + rid) & (seq - 1)
            first, last = pos == 0, pos == seq - 1

        for c, cs in enumerate(chunks):
            u_ref[c, 0] = _dot(h_ref[...], wg_ref[:, cs])
            u_ref[c, 1] = _dot(h_ref[...], wv_ref[:, cs])

        def conv(c, k, cw_ref, cb_ref, cs):
            u = u_ref[c, k, 0:tm, :]
            prev_row = jnp.where(tile_starts, 0.0, u_ref[c, k, tm + halo - 1:tm + halo, :])
            next_row = jnp.where(tile_ends, 0.0, u_ref[c, k, tm + halo:tm + halo + 1, :])
            before = jnp.where(rid == 0, prev_row, pltpu.roll(u, 1, 0))
            after = jnp.where(rid == tm - 1, next_row, pltpu.roll(u, tm - 1, 0))
            if interior:
                before = jnp.where(first, 0.0, before)
                after = jnp.where(last, 0.0, after)
            return (cb_ref[:, cs] + before * cw_ref[0:1, cs] + u * cw_ref[1:2, cs]
                    + after * cw_ref[2:3, cs])

        for c, cs in enumerate(chunks):
            act = _silu(conv(c, 0, cwg_ref, cbg_ref, cs)) * conv(c, 1, cwv_ref, cbv_ref, cs)
            o_ref[...] += _dot(act.astype(BF16), wd_ref[cs, :])

    if has_short:
        pl.when(i < n_long)(functools.partial(step, False))
        pl.when(i >= n_long)(functools.partial(step, True))
    else:
        step(False)

    @pl.when(j == pl.num_programs(1) - 1)
    def _():
        def finish(r, carry):
            rs = pl.ds(pl.multiple_of(r * rows, rows), rows)
            y = x_ref[rs, :] + gate_ref[0] * o_ref[rs, :]
            o_ref[rs, :] = _rms(y, gf_ref[...]) if final else y
            return carry

        lax.fori_loop(0, tm // rows, finish, 0)


def conv_ffn(x, g, mod, w_up, conv_w, conv_b, w_down, *, tile_mod, t_lat, t_ctx, n_lat,
             n_rows, tm, tf=512, final_gain=None):
    n, d = x.shape
    dff = w_down.shape[0]
    assert n_rows % tm == 0 and n_lat % tm == 0 and t_lat % tm == 0 and dff % tf == 0
    assert n_rows == n_lat or tm % t_ctx == 0 or t_ctx % tm == 0
    nj = dff // tf
    halo = 8
    chunk = min(256, tf)
    hb = tm // halo
    nhb = n // halo
    n_long = n_lat // tm
    cb = conv_b.reshape(1, 2 * dff)
    final = final_gain is not None
    in_specs = [
        pl.BlockSpec((tm, d), lambda i, j: (i, 0)),
        pl.BlockSpec((halo, d), lambda i, j: (jnp.maximum(i * hb - 1, 0), 0)),
        pl.BlockSpec((halo, d), lambda i, j: (jnp.minimum((i + 1) * hb, nhb - 1), 0)),
        pl.BlockSpec((1, d), lambda i, j: (0, 0)),
        pl.BlockSpec((1, 1, d), lambda i, j: (tile_mod(i) * N_MOD + 3, 0, 0)),
        pl.BlockSpec((1, 1, d), lambda i, j: (tile_mod(i) * N_MOD + 4, 0, 0)),
        pl.BlockSpec((1, 1, d), lambda i, j: (tile_mod(i) * N_MOD + 5, 0, 0)),
        pl.BlockSpec((d, tf), lambda i, j: (0, j)),
        pl.BlockSpec((d, tf), lambda i, j: (0, nj + j)),
        pl.BlockSpec((CONV_W, tf), lambda i, j: (0, j)),
        pl.BlockSpec((CONV_W, tf), lambda i, j: (0, nj + j)),
        pl.BlockSpec((1, tf), lambda i, j: (0, j)),
        pl.BlockSpec((1, tf), lambda i, j: (0, nj + j)),
        pl.BlockSpec((tf, d), lambda i, j: (j, 0)),
    ]
    args = [x, x, x, g.reshape(1, d), mod, mod, mod, w_up, w_up, conv_w, conv_w, cb, cb, w_down]
    if final:
        in_specs.append(pl.BlockSpec((1, d), lambda i, j: (0, 0)))
        args.append(final_gain.reshape(1, d))
    return pl.pallas_call(
        functools.partial(
            _ffn_kernel, tile_seq=lambda i: jnp.where(i < n_long, t_lat, t_ctx), n_long=n_long,
            has_short=n_rows > n_lat and t_ctx < tm, final=final, rows=min(tm, 256), chunk=chunk),
        out_shape=jax.ShapeDtypeStruct((n_rows, d), F32),
        grid=(n_rows // tm, nj),
        in_specs=in_specs,
        out_specs=pl.BlockSpec((tm, d), lambda i, j: (i, 0)),
        scratch_shapes=[pltpu.VMEM((tm + 2 * halo, d), BF16),
                        pltpu.VMEM((tf // chunk, 2, tm + 2 * halo, chunk), F32)],
        compiler_params=_cparams(("parallel", "arbitrary")),
        name="conv_ffn",
    )(*args)


def _rope_tables(t_lat, tm):
    rows = t_lat // GRID_W
    row = jnp.repeat(jnp.arange(rows, dtype=F32), GRID_W)
    col = jnp.tile(jnp.arange(GRID_W, dtype=F32), rows)
    n_freq = HEAD_DIM // 4
    inv = ROPE_THETA ** (-jnp.arange(n_freq, dtype=F32) / n_freq)
    ang = jnp.concatenate([row[:, None] * inv, col[:, None] * inv], axis=-1)
    cos2 = jnp.repeat(jnp.cos(ang), 2, axis=-1)
    sin2 = jnp.repeat(jnp.sin(ang), 2, axis=-1) * jnp.tile(jnp.array([-1.0, 1.0], F32), HEAD_DIM // 2)
    cos2 = jnp.concatenate([cos2, jnp.ones((tm, HEAD_DIM), F32)], axis=0)
    sin2 = jnp.concatenate([sin2, jnp.zeros((tm, HEAD_DIM), F32)], axis=0)
    return cos2, sin2


def _lower_bounds(lb_param):
    p = jax.nn.softmax(lb_param.astype(F32), axis=1)
    return jnp.cumsum(p, axis=1) - p[:, :1]


def kernel(x, c, ctx, c_ctx, w_ada, b_ada, norm1_g, w_in, q_norm_g, k_norm_g, hg_lower_bounds,
           hg_norm_g, sg_norm_g, sg_w, sg_b, w_out, norm2_g, w_up, conv_w, conv_b, w_down,
           final_norm_g):
    batch, t_lat, d = x.shape
    t_ctx = ctx.shape[1]
    depth = w_in.shape[0]
    n_lat, n_ctx = batch * t_lat, batch * t_ctx
    assert t_lat & (t_lat - 1) == 0 and t_ctx & (t_ctx - 1) == 0

    tm = min(1024, n_ctx)
    assert t_lat % tm == 0 and n_ctx % tm == 0

    def tile_mod(i):
        return jnp.minimum(i // (t_lat // tm), batch)

    def tile_rope(i):
        return jnp.where(i < n_lat // tm, i % (t_lat // tm), t_lat // tm)

    xs = jnp.concatenate([x.reshape(n_lat, d), ctx.reshape(n_ctx, d)], axis=0)
    cin = jnp.concatenate([c, c_ctx[None, :], jnp.zeros((8 - batch - 1, d), F32)], axis=0)
    mods = ada_table(cin, w_ada, b_ada)[:, :batch + 1, :]
    mods = mods.reshape(depth, (batch + 1) * N_MOD, 1, d)
    cos2, sin2 = _rope_tables(t_lat, tm)
    lbs = _lower_bounds(hg_lower_bounds)
    qw = ATTN_HEADS * HEAD_DIM
    kvw = ATTN_KV_HEADS * HEAD_DIM
    hg_col0 = (qw + 2 * kvw) // (HG_HEADS * HG_D)
    sg_col0 = hg_col0 + 5

    w_in_l = w_in[0].astype(BF16)

    for l in range(depth):
        mod = mods[l]
        last = l == depth - 1
        n_rows = n_lat if last else n_lat + n_ctx
        p = in_proj(xs, norm1_g[l], mod, w_in_l, tile_mod=tile_mod, tm=tm)
        q, k, vt = qk_prep(p, cos2, sin2, q_norm_g[l], k_norm_g[l], tile_rope=tile_rope, tm=tm)
        casts = [(w_out, l), (w_up, l), (w_down, l)] + ([] if last else [(w_in, l + 1)])
        attn, w_out_l, w_up_l, w_down_l, *nxt = attention(
            q, k, vt, casts, batch=batch, t_lat=t_lat, t_ctx=t_ctx, ctx_queries=not last)
        w_in_l = nxt[0] if nxt else None
        o_f = hgrn_scan(p, lbs[0, l], batch=batch, t_lat=t_lat, t_ctx=t_ctx, col0=hg_col0,
                        reverse=False)
        hg = hgrn_scan(p, lbs[1, l], batch=batch, t_lat=t_lat, t_ctx=t_ctx, col0=hg_col0,
                       reverse=True, fwd_out=o_f, gain=hg_norm_g[l])
        bias_full = jnp.repeat(sg_b[l].T, SG_DIM, axis=1)
        sg = spatial_gate(p, sg_norm_g[l], sg_w[l].astype(BF16), bias_full, col0=sg_col0,
                          n_rows=n_rows, tm=tm)
        xs = out_proj(attn, hg, sg, w_out_l, xs, mod, tile_mod=tile_mod, n_rows=n_rows, tm=tm)
        xs = conv_ffn(xs, norm2_g[l], mod, w_up_l, conv_w[l], conv_b[l], w_down_l,
                      tile_mod=tile_mod, t_lat=t_lat, t_ctx=t_ctx, n_lat=n_lat, n_rows=n_rows,
                      tm=tm, final_gain=final_norm_g if last else None)

    return xs.reshape(batch, t_lat, d)
```

```python
import functools

import jax
import jax.numpy as jnp
from jax import lax
from jax.experimental import pallas as pl
from jax.experimental.pallas import tpu as pltpu

F32 = jnp.float32
BF16 = jnp.bfloat16

EPS = 1e-6
F_MIN = 1e-30
N_MOD = 6
HEAD_DIM = 128
ATTN_HEADS = 8
ATTN_KV_HEADS = 2
ATTN_GROUP = ATTN_HEADS // ATTN_KV_HEADS
ROPE_THETA = 10000.0
GRID_W = 64
HG_HEADS = 4
HG_D = 128
SG_GROUPS = 4
SG_DIM = 128
SG_CHUNK = 128
CONV_W = 3
LANE = 128
SUBLANES = 8
LOG2_E = 1.4426950408889634
HG_CHUNK = 128
BF16_ROWS = 2 * SUBLANES
MXU_WIDTH = 256
V7X_VMEM_BYTES = 64 * 1024 * 1024
VMEM_LIMIT = V7X_VMEM_BYTES - 2 * 1024 * 1024


def _cparams(sem):
    return pltpu.CompilerParams(dimension_semantics=sem, vmem_limit_bytes=VMEM_LIMIT)


def _dot(a, b):
    return jnp.dot(a, b, preferred_element_type=F32)


def _dot_nt(a, b):
    return lax.dot_general(a, b, (((1,), (1,)), ((), ())), preferred_element_type=F32)


def _rms(x, g):
    return x * lax.rsqrt(jnp.mean(x * x, axis=-1, keepdims=True) + EPS) * g


def _silu(x):
    return x * jax.nn.sigmoid(x)


def _ada_kernel(c_ref, w_ref, b_ref, o_ref):
    s = _silu(c_ref[...]).astype(BF16)
    o_ref[...] = _dot(s, w_ref[...].astype(BF16)) + b_ref[...]


def ada_table(cin, w_ada, b_ada, tn=1024):
    depth, d, n = w_ada.shape
    assert n % tn == 0
    return pl.pallas_call(
        _ada_kernel,
        out_shape=jax.ShapeDtypeStruct((depth, 8, n), F32),
        grid=(depth, n // tn),
        in_specs=[
            pl.BlockSpec((8, d), lambda l, j: (0, 0)),
            pl.BlockSpec((None, d, tn), lambda l, j: (l, 0, j)),
            pl.BlockSpec((None, 1, tn), lambda l, j: (l, 0, j)),
        ],
        out_specs=pl.BlockSpec((None, 8, tn), lambda l, j: (l, 0, j)),
        compiler_params=_cparams(("parallel", "parallel")),
        name="ada_table",
    )(cin, w_ada, b_ada.reshape(depth, 1, n))


def _norm_mod(x, g, scale, shift):
    r = lax.rsqrt(jnp.mean(x * x, axis=-1, keepdims=True) + EPS)
    return ((x * r) * (g * (1.0 + scale)) + shift).astype(BF16)


def _in_kernel(x_ref, g_ref, sh_ref, sc_ref, w_ref, *rest, rows):
    o_ref, h_ref = rest[-2:]

    @pl.when(pl.program_id(1) == 0)
    def _():
        for r in range(0, x_ref.shape[0], rows):
            h_ref[r:r + rows, :] = _norm_mod(x_ref[r:r + rows, :], g_ref[...], sc_ref[0], sh_ref[0])

    o_ref[...] = _dot(h_ref[...], w_ref[...])


def _dest(into, out_rows, n):
    rows = into.shape[0] if into is not None else (out_rows or n)
    extra_specs = [] if into is None else [pl.BlockSpec(memory_space=pl.ANY)]
    return rows, extra_specs, [] if into is None else [into]


def in_proj(x, g, mod, w, *, tile_mod, tm, tn=1280, tile0=0, out_rows=None, into=None):
    n, d = x.shape
    cols = w.shape[1]
    assert n % tm == 0 and cols % tn == 0
    rows, extra_specs, extra = _dest(into, out_rows, n)
    return pl.pallas_call(
        functools.partial(_in_kernel, rows=min(tm, 256)),
        out_shape=jax.ShapeDtypeStruct((rows, cols), F32),
        grid=(n // tm, cols // tn),
        in_specs=[
            pl.BlockSpec((tm, d), lambda i, j: (i, 0)),
            pl.BlockSpec((1, d), lambda i, j: (0, 0)),
            pl.BlockSpec((1, 1, d), lambda i, j: (tile_mod(i + tile0) * N_MOD + 0, 0, 0)),
            pl.BlockSpec((1, 1, d), lambda i, j: (tile_mod(i + tile0) * N_MOD + 1, 0, 0)),
            pl.BlockSpec((d, tn), lambda i, j: (0, j)),
            *extra_specs,
        ],
        out_specs=pl.BlockSpec((tm, tn), lambda i, j: (i + tile0, j)),
        scratch_shapes=[pltpu.VMEM((tm, d), BF16)],
        input_output_aliases={5: 0} if extra else {},
        compiler_params=_cparams(("parallel", "arbitrary")),
        name="in_proj",
    )(x, g.reshape(1, d), mod, mod, w, *extra)


def _rope(y, cos2, sin2):
    lane = lax.broadcasted_iota(jnp.int32, y.shape, 1)
    swapped = jnp.where((lane & 1) == 0, pltpu.roll(y, LANE - 1, 1), pltpu.roll(y, 1, 1))
    return y * cos2 + swapped * sin2


def _qk_kernel(q_ref, kv_ref, cos_ref, sin_ref, qg_ref, kg_ref, qo_ref, ko_ref, vo_ref):
    cos2, sin2 = cos_ref[...], sin_ref[...]
    scale = HEAD_DIM ** -0.5 * LOG2_E
    for h in range(ATTN_HEADS):
        sl = slice(h * HEAD_DIM, (h + 1) * HEAD_DIM)
        y = _rms(q_ref[:, sl], qg_ref[...])
        qo_ref[:, sl] = (_rope(y, cos2, sin2) * scale).astype(BF16)
    for h in range(ATTN_KV_HEADS):
        sl = slice(h * HEAD_DIM, (h + 1) * HEAD_DIM)
        y = _rms(kv_ref[:, sl], kg_ref[...])
        ko_ref[:, sl] = _rope(y, cos2, sin2).astype(BF16)
    kvw = ATTN_KV_HEADS * HEAD_DIM
    vo_ref[...] = kv_ref[:, kvw:2 * kvw].T.astype(BF16)


def qk_prep(p, cos2, sin2, qg, kg, *, tile_rope, tm):
    n = p.shape[0]
    qw = ATTN_HEADS * HEAD_DIM
    kvw = ATTN_KV_HEADS * HEAD_DIM
    return pl.pallas_call(
        _qk_kernel,
        out_shape=(jax.ShapeDtypeStruct((n, qw), BF16),
                   jax.ShapeDtypeStruct((n, kvw), BF16),
                   jax.ShapeDtypeStruct((kvw, n), BF16)),
        grid=(n // tm,),
        in_specs=[
            pl.BlockSpec((tm, qw), lambda i: (i, 0)),
            pl.BlockSpec((tm, 2 * kvw), lambda i: (i, qw // (2 * kvw))),
            pl.BlockSpec((tm, HEAD_DIM), lambda i: (tile_rope(i), 0)),
            pl.BlockSpec((tm, HEAD_DIM), lambda i: (tile_rope(i), 0)),
            pl.BlockSpec((1, HEAD_DIM), lambda i: (0, 0)),
            pl.BlockSpec((1, HEAD_DIM), lambda i: (0, 0)),
        ],
        out_specs=(pl.BlockSpec((tm, qw), lambda i: (i, 0)),
                   pl.BlockSpec((tm, kvw), lambda i: (i, 0)),
                   pl.BlockSpec((kvw, tm), lambda i: (0, i))),
        compiler_params=_cparams(("parallel",)),
        name="qk_prep",
    )(p, p, cos2, sin2, qg.reshape(1, HEAD_DIM), kg.reshape(1, HEAD_DIM))


def _softmax_pv(q, segs, ck, s_ref):
    mq = q.shape[0]
    chunks = [(k_ref, vt_ref, c0, min(ck, k_ref.shape[0]))
              for k_ref, vt_ref in segs for c0 in range(0, k_ref.shape[0], min(ck, k_ref.shape[0]))]

    def scores(i):
        k_ref, _, c0, cs = chunks[i]
        s_ref[i % 2, 0:cs, :] = _dot_nt(k_ref[c0:c0 + cs, :], q)

    scores(0)
    m = l = acc = None
    for i, (_, vt_ref, c0, cs) in enumerate(chunks):
        if i + 1 < len(chunks):
            scores(i + 1)
        s = s_ref[i % 2, 0:cs, :]
        m_c = s.reshape(cs // SUBLANES, SUBLANES, mq).max(axis=0).max(axis=0, keepdims=True)
        m_new = m_c if m is None else jnp.maximum(m, m_c)
        p = jnp.exp2(s - m_new)
        p_rows = p.reshape(cs // SUBLANES, SUBLANES, mq).sum(axis=0)
        pv = _dot(vt_ref[:, c0:c0 + cs], p.astype(BF16))
        if m is None:
            l, acc = p_rows, pv
        else:
            alpha = jnp.exp2(m - m_new)
            l, acc = alpha * l + p_rows, alpha * acc + pv
        m = m_new
    return (acc / l.sum(axis=0, keepdims=True)).T


def _attn_kernel(*refs, n_lat, ck, n_cast):
    q_ref, kl_ref, kc_ref, vl_ref, vc_ref = refs[:5]
    o_ref, s_ref = refs[5 + n_cast], refs[-1]
    qi = pl.program_id(2)
    tq = q_ref.shape[0]


    for src, dst in zip(refs[5:5 + n_cast], refs[6 + n_cast:6 + 2 * n_cast]):
        dst[...] = src[...].astype(dst.dtype)

    def run(segs):
        hs = range(ATTN_GROUP)
        q = jnp.concatenate([q_ref[:, h * HEAD_DIM:(h + 1) * HEAD_DIM] for h in hs], axis=0)
        o = _softmax_pv(q, segs, ck, s_ref)
        for h in hs:
            o_ref[:, h * HEAD_DIM:(h + 1) * HEAD_DIM] = o[h * tq:(h + 1) * tq].astype(BF16)

    @pl.when(qi < n_lat)
    def _():
        run([(kl_ref, vl_ref), (kc_ref, vc_ref)])

    @pl.when(qi >= n_lat)
    def _():
        run([(kc_ref, vc_ref)])


def _cast_rows(total, steps):
    return next(r for r in range(BF16_ROWS, total + 1, BF16_ROWS)
                if total % r == 0 and total // r <= steps)


def attention(q, k, vt, weights, *, batch, t_lat, t_ctx, ctx_queries=True, tq=256, ck=512):
    n = q.shape[0]
    n_lat = t_lat // tq
    n_ctx = t_ctx // tq if ctx_queries else 0
    nq = n_lat + n_ctx
    gw = ATTN_GROUP * HEAD_DIM
    ctx0 = batch * t_lat
    steps = batch * ATTN_KV_HEADS * nq

    def qmap(b, j, i):
        return (jnp.where(i < n_lat, b * n_lat + i, ctx0 // tq + b * (t_ctx // tq) + (i - n_lat)), j)

    def cast_specs(w, layer):
        rows = _cast_rows(w.shape[1], steps)
        last = w.shape[1] // rows - 1

        def blk(b, j, i):
            return jnp.minimum((b * ATTN_KV_HEADS + j) * nq + i, last)

        return (pl.BlockSpec((None, rows, w.shape[2]), lambda b, j, i: (layer, blk(b, j, i), 0)),
                pl.BlockSpec((rows, w.shape[2]), lambda b, j, i: (blk(b, j, i), 0)))

    specs = [cast_specs(w, layer) for w, layer in weights]
    weights = [w for w, _ in weights]
    return pl.pallas_call(
        functools.partial(_attn_kernel, n_lat=n_lat, ck=ck, n_cast=len(weights)),
        out_shape=(jax.ShapeDtypeStruct((n, ATTN_HEADS * HEAD_DIM), BF16),
                   *[jax.ShapeDtypeStruct(w.shape[1:], BF16) for w in weights]),
        grid=(batch, ATTN_KV_HEADS, nq),
        in_specs=[
            pl.BlockSpec((tq, gw), qmap),
            pl.BlockSpec((t_lat, HEAD_DIM), lambda b, j, i: (b, j)),
            pl.BlockSpec((t_ctx, HEAD_DIM), lambda b, j, i: (ctx0 // t_ctx + b, j)),
            pl.BlockSpec((HEAD_DIM, t_lat), lambda b, j, i: (j, b)),
            pl.BlockSpec((HEAD_DIM, t_ctx), lambda b, j, i: (j, ctx0 // t_ctx + b)),
            *[s[0] for s in specs],
        ],
        out_specs=(pl.BlockSpec((tq, gw), qmap), *[s[1] for s in specs]),
        scratch_shapes=[pltpu.VMEM((2, ck, ATTN_GROUP * tq), F32)],
        compiler_params=_cparams(("arbitrary", "arbitrary", "arbitrary")),
        name="attention",
    )(q, k, k, vt, vt, *weights)


def _hgrn_chunk(hq, z, v, lb, st, *, reverse):
    c = hq.shape[0]
    sig = jax.nn.sigmoid(z)
    f = lb + (1.0 - lb) * sig
    logf = jnp.log(jnp.maximum(f, F_MIN)) * (1.0 / jnp.log(2.0))
    kk = (1.0 - lb) * (1.0 - sig)
    q = _silu(hq)

    row = lax.broadcasted_iota(jnp.int32, (c, c), 0)
    col = lax.broadcasted_iota(jnp.int32, (c, c), 1)
    rid = lax.broadcasted_iota(jnp.int32, (c, HG_D), 0)
    later = (row < col) if reverse else (row > col)

    a = jnp.where(row == col, _dot_nt(q.astype(BF16), kk.astype(BF16)), 0.0)
    cum, tot = logf, logf
    d, ld = 1, 0
    while d < c:
        mask = jnp.logical_and(((row ^ col) >> ld) == 1, later)
        if d < SUBLANES:
            second = (rid & d) != 0
            sel = jnp.logical_not(second) if reverse else second
            x = jnp.where(sel, q, kk) * jnp.exp2(jnp.where(sel, cum, tot - cum))
            t3 = tot.reshape(c // SUBLANES, SUBLANES, HG_D)
            up = pltpu.roll(t3, d, 1).reshape(c, HG_D)
            partner = up if 2 * d == SUBLANES else jnp.where(
                second, up, pltpu.roll(t3, SUBLANES - d, 1).reshape(c, HG_D))
            cum = cum + jnp.where(sel, partner, 0.0)
            tot = tot + partner
        else:
            def split(y):
                y4 = y.reshape(c // (2 * d), 2, d, HG_D)
                return (y4[:, 1], y4[:, 0]) if reverse else (y4[:, 0], y4[:, 1])

            def join(early, late):
                pair = [late, early] if reverse else [early, late]
                return jnp.stack(pair, axis=1).reshape(c, HG_D)

            (cum_e, cum_l), (tot_e, tot_l) = split(cum), split(tot)
            x = join(split(kk)[0] * jnp.exp2(tot_e - cum_e), split(q)[1] * jnp.exp2(cum_l))
            both = tot_e + tot_l
            cum, tot = join(cum_e, cum_l + tot_e), join(both, both)
        xb = x.astype(BF16)
        a = jnp.where(mask, _dot_nt(xb, xb), a)
        d, ld = d * 2, ld + 1

    qd = (q * jnp.exp2(cum)).astype(BF16)
    kd = (kk * jnp.exp2(tot - cum)).astype(BF16)
    vb = v.astype(BF16)
    o = _dot(a.astype(BF16), vb) + _dot_nt(qd, st.astype(BF16))
    st_new = st * jnp.exp2(tot[0:1, :]) + _dot(vb.T, kd)
    return o, st_new


def _hgrn_kernel(*refs, reverse, finalize):
    if finalize:
        hq_ref, hf_ref, hi_ref, lb_ref, of_ref, gt_ref, g_ref, o_ref, st_ref = refs
    else:
        hq_ref, hf_ref, hi_ref, lb_ref, o_ref, st_ref = refs

    @pl.when(pl.program_id(1) == 0)
    def _():
        st_ref[...] = jnp.zeros_like(st_ref)

    starts = range(0, hq_ref.shape[0], HG_CHUNK)
    for r in (reversed(starts) if reverse else starts):
        rs = slice(r, r + HG_CHUNK)
        for h in range(HG_HEADS):
            sl = slice(h * HG_D, (h + 1) * HG_D)
            o, st_new = _hgrn_chunk(hq_ref[rs, sl], hf_ref[rs, sl], hi_ref[rs, sl], lb_ref[:, sl],
                                    st_ref[h], reverse=reverse)
            st_ref[h] = st_new
            if finalize:
                y = _rms(o + of_ref[rs, sl], g_ref[...])
                o_ref[rs, sl] = (y * _silu(gt_ref[rs, sl])).astype(o_ref.dtype)
            else:
                o_ref[rs, sl] = o


def hgrn_scan(p, lb, *, batch, t_lat, t_ctx, col0, reverse, fwd_out=None, gain=None,
              chunks_per_step=2):
    n = p.shape[0]
    c = HG_CHUNK * chunks_per_step
    assert t_lat % c == 0 and t_ctx % c == 0
    w = HG_HEADS * HG_D
    n_lat, n_ctx = t_lat // c, t_ctx // c
    ctx0 = batch * n_lat

    def rows(b, s):
        if reverse:
            return jnp.where(s < n_ctx, ctx0 + b * n_ctx + (n_ctx - 1 - s),
                             b * n_lat + (n_lat - 1 - (s - n_ctx)))
        return jnp.where(s < n_ctx, ctx0 + b * n_ctx + s, b * n_lat + (s - n_ctx))

    def spec(cb):
        return pl.BlockSpec((c, w), lambda b, s: (rows(b, s), cb))

    finalize = fwd_out is not None
    in_specs = [spec(col0), spec(col0 + (2 if reverse else 1)), spec(col0 + 3),
                pl.BlockSpec((1, w), lambda b, s: (0, 0))]
    args = [p, p, p, lb.reshape(1, w)]
    if finalize:
        in_specs += [spec(0), spec(col0 + 4), pl.BlockSpec((1, HG_D), lambda b, s: (0, 0))]
        args += [fwd_out, p, gain.reshape(1, HG_D)]
    return pl.pallas_call(
        functools.partial(_hgrn_kernel, reverse=reverse, finalize=finalize),
        out_shape=jax.ShapeDtypeStruct((n, w), BF16 if finalize else F32),
        grid=(batch, n_lat + n_ctx),
        in_specs=in_specs,
        out_specs=spec(0),
        scratch_shapes=[pltpu.VMEM((HG_HEADS, HG_D, HG_D), F32)],
        compiler_params=_cparams(("parallel", "arbitrary")),
        name="hgrn_bwd" if reverse else "hgrn_fwd",
    )(*args)


def _sg_kernel(u_ref, v_ref, g_ref, w_ref, b_ref, o_ref):
    for r in range(0, u_ref.shape[0], SG_CHUNK):
        rs = slice(r, r + SG_CHUNK)
        for g in range(SG_GROUPS):
            sl = slice(g * SG_DIM, (g + 1) * SG_DIM)
            vn = _rms(jax.nn.gelu(v_ref[rs, sl]), g_ref[:, sl])
            mixed = _dot(w_ref[g], vn.astype(BF16)) + b_ref[:, sl]
            o_ref[rs, sl] = (jax.nn.gelu(u_ref[rs, sl]) * mixed).astype(BF16)


def spatial_gate(p, g, w, bias_full, *, col0, n_rows, tm):
    sw = SG_GROUPS * SG_DIM
    assert n_rows % tm == 0
    return pl.pallas_call(
        _sg_kernel,
        out_shape=jax.ShapeDtypeStruct((n_rows, sw), BF16),
        grid=(n_rows // tm,),
        in_specs=[
            pl.BlockSpec((tm, sw), lambda i: (i, col0)),
            pl.BlockSpec((tm, sw), lambda i: (i, col0 + 1)),
            pl.BlockSpec((1, sw), lambda i: (0, 0)),
            pl.BlockSpec((SG_GROUPS, SG_CHUNK, SG_CHUNK), lambda i: (0, 0, 0)),
            pl.BlockSpec((SG_CHUNK, sw), lambda i: (0, 0)),
        ],
        out_specs=pl.BlockSpec((tm, sw), lambda i: (i, 0)),
        compiler_params=_cparams(("parallel",)),
        name="spatial_gate",
    )(p, p, g.reshape(1, sw), w, bias_full)


def _out_kernel(a_ref, h_ref, s_ref, wa_ref, wh_ref, ws_ref, x_ref, gate_ref, *rest):
    o_ref = rest[-1]
    acc = _dot(a_ref[...], wa_ref[...]) + _dot(h_ref[...], wh_ref[...]) + _dot(s_ref[...], ws_ref[...])
    o_ref[...] = x_ref[...] + gate_ref[0] * acc


def out_proj(attn, hg, sg, w, x, mod, *, tile_mod, n_rows, tm, tn=2048, tile0=0, out_rows=None,
             into=None):
    n, d = n_rows, x.shape[1]
    tn = min(tn, d)
    wa, wh, ws = attn.shape[1], hg.shape[1], sg.shape[1]
    assert wa % wh == 0 and wh == ws and n % tm == 0 and d % tn == 0
    rows, extra_specs, extra = _dest(into, out_rows, n)
    return pl.pallas_call(
        _out_kernel,
        out_shape=jax.ShapeDtypeStruct((rows, d), F32),
        grid=(n // tm, d // tn),
        in_specs=[
            pl.BlockSpec((tm, wa), lambda i, j: (i + tile0, 0)),
            pl.BlockSpec((tm, wh), lambda i, j: (i + tile0, 0)),
            pl.BlockSpec((tm, ws), lambda i, j: (i + tile0, 0)),
            pl.BlockSpec((wa, tn), lambda i, j: (0, j)),
            pl.BlockSpec((wh, tn), lambda i, j: (wa // wh, j)),
            pl.BlockSpec((ws, tn), lambda i, j: (wa // wh + 1, j)),
            pl.BlockSpec((tm, tn), lambda i, j: (i, j)),
            pl.BlockSpec((1, 1, tn), lambda i, j: (tile_mod(i + tile0) * N_MOD + 2, 0, j)),
            *extra_specs,
        ],
        out_specs=pl.BlockSpec((tm, tn), lambda i, j: (i + tile0, j)),
        input_output_aliases={8: 0} if extra else {},
        compiler_params=_cparams(("parallel", "parallel")),
        name="out_proj",
    )(attn, hg, sg, w, w, w, x, mod, *extra)


def _ffn_kernel(*refs, tile_seq, n_long, has_short, final, rows, chunk):
    (x_ref, xp_ref, xn_ref, g_ref, sh_ref, sc_ref, gate_ref,
     wg_ref, wv_ref, cwg_ref, cwv_ref, cbg_ref, cbv_ref, wd_ref) = refs[:14]
    gf_ref = refs[14] if final else None
    o_ref, h_ref, u_ref = refs[-3:]
    i, j = pl.program_id(0), pl.program_id(1)
    tm = x_ref.shape[0]
    halo = xp_ref.shape[0]

    def norm_mod(x):
        return _norm_mod(x, g_ref[...], sc_ref[0], sh_ref[0])

    @pl.when(j == 0)
    def _():
        for r in range(0, tm, rows):
            h_ref[r:r + rows, :] = norm_mod(x_ref[r:r + rows, :])
        h_ref[tm:tm + halo, :] = norm_mod(xp_ref[...])
        h_ref[tm + halo:tm + 2 * halo, :] = norm_mod(xn_ref[...])
        o_ref[...] = jnp.zeros_like(o_ref)

    chunks = [slice(c0, c0 + chunk) for c0 in range(0, wd_ref.shape[0], chunk)]
    rid = lax.broadcasted_iota(jnp.int32, (tm, 1), 0)

    def step(interior):
        seq = tile_seq(i)
        tile_starts = ((i * tm) & (seq - 1)) == 0
        tile_ends = (((i + 1) * tm) & (seq - 1)) == 0
        if interior:
            pos = (i * tm + rid) & (seq - 1)
            first, last = pos == 0, pos == seq - 1

        for c, cs in enumerate(chunks):
            u_ref[c, 0] = _dot(h_ref[...], wg_ref[:, cs])
            u_ref[c, 1] = _dot(h_ref[...], wv_ref[:, cs])

        def conv(c, k, cw_ref, cb_ref, cs):
            u = u_ref[c, k, 0:tm, :]
            prev_row = jnp.where(tile_starts, 0.0, u_ref[c, k, tm + halo - 1:tm + halo, :])
            next_row = jnp.where(tile_ends, 0.0, u_ref[c, k, tm + halo:tm + halo + 1, :])
            before = jnp.where(rid == 0, prev_row, pltpu.roll(u, 1, 0))
            after = jnp.where(rid == tm - 1, next_row, pltpu.roll(u, tm - 1, 0))
            if interior:
                before = jnp.where(first, 0.0, before)
                after = jnp.where(last, 0.0, after)
            return (cb_ref[:, cs] + before * cw_ref[0:1, cs] + u * cw_ref[1:2, cs]
                    + after * cw_ref[2:3, cs])

        for c, cs in enumerate(chunks):
            act = _silu(conv(c, 0, cwg_ref, cbg_ref, cs)) * conv(c, 1, cwv_ref, cbv_ref, cs)
            o_ref[...] += _dot(act.astype(BF16), wd_ref[cs, :])

    if has_short:
        pl.when(i < n_long)(functools.partial(step, False))
        pl.when(i >= n_long)(functools.partial(step, True))
    else:
        step(False)

    @pl.when(j == pl.num_programs(1) - 1)
    def _():
        def finish(r, carry):
            rs = pl.ds(pl.multiple_of(r * rows, rows), rows)
            y = x_ref[rs, :] + gate_ref[0] * o_ref[rs, :]
            o_ref[rs, :] = _rms(y, gf_ref[...]) if final else y
            return carry

        lax.fori_loop(0, tm // rows, finish, 0)


def conv_ffn(x, g, mod, w_up, conv_w, conv_b, w_down, *, tile_mod, t_lat, t_ctx, n_lat,
             n_rows, tm, tf=512, final_gain=None):
    n, d = x.shape
    dff = w_down.shape[0]
    assert n_rows % tm == 0 and n_lat % tm == 0 and t_lat % tm == 0 and dff % tf == 0
    assert n_rows == n_lat or tm % t_ctx == 0 or t_ctx % tm == 0
    nj = dff // tf
    halo = SUBLANES
    chunk = min(MXU_WIDTH, tf)
    hb = tm // halo
    nhb = n // halo
    n_long = n_lat // tm
    cb = conv_b.reshape(1, 2 * dff)
    final = final_gain is not None
    in_specs = [
        pl.BlockSpec((tm, d), lambda i, j: (i, 0)),
        pl.BlockSpec((halo, d), lambda i, j: (jnp.maximum(i * hb - 1, 0), 0)),
        pl.BlockSpec((halo, d), lambda i, j: (jnp.minimum((i + 1) * hb, nhb - 1), 0)),
        pl.BlockSpec((1, d), lambda i, j: (0, 0)),
        pl.BlockSpec((1, 1, d), lambda i, j: (tile_mod(i) * N_MOD + 3, 0, 0)),
        pl.BlockSpec((1, 1, d), lambda i, j: (tile_mod(i) * N_MOD + 4, 0, 0)),
        pl.BlockSpec((1, 1, d), lambda i, j: (tile_mod(i) * N_MOD + 5, 0, 0)),
        pl.BlockSpec((d, tf), lambda i, j: (0, j)),
        pl.BlockSpec((d, tf), lambda i, j: (0, nj + j)),
        pl.BlockSpec((CONV_W, tf), lambda i, j: (0, j)),
        pl.BlockSpec((CONV_W, tf), lambda i, j: (0, nj + j)),
        pl.BlockSpec((1, tf), lambda i, j: (0, j)),
        pl.BlockSpec((1, tf), lambda i, j: (0, nj + j)),
        pl.BlockSpec((tf, d), lambda i, j: (j, 0)),
    ]
    args = [x, x, x, g.reshape(1, d), mod, mod, mod, w_up, w_up, conv_w, conv_w, cb, cb, w_down]
    if final:
        in_specs.append(pl.BlockSpec((1, d), lambda i, j: (0, 0)))
        args.append(final_gain.reshape(1, d))
    return pl.pallas_call(
        functools.partial(
            _ffn_kernel, tile_seq=lambda i: jnp.where(i < n_long, t_lat, t_ctx), n_long=n_long,
            has_short=n_rows > n_lat and t_ctx < tm, final=final, rows=min(tm, 256), chunk=chunk),
        out_shape=jax.ShapeDtypeStruct((n_rows, d), F32),
        grid=(n_rows // tm, nj),
        in_specs=in_specs,
        out_specs=pl.BlockSpec((tm, d), lambda i, j: (i, 0)),
        scratch_shapes=[pltpu.VMEM((tm + 2 * halo, d), BF16),
                        pltpu.VMEM((tf // chunk, 2, tm + 2 * halo, chunk), F32)],
        compiler_params=_cparams(("parallel", "arbitrary")),
        name="conv_ffn",
    )(*args)


def _rope_tables(t_lat, tm):
    rows = t_lat // GRID_W
    row = jnp.repeat(jnp.arange(rows, dtype=F32), GRID_W)
    col = jnp.tile(jnp.arange(GRID_W, dtype=F32), rows)
    n_freq = HEAD_DIM // 4
    inv = ROPE_THETA ** (-jnp.arange(n_freq, dtype=F32) / n_freq)
    ang = jnp.concatenate([row[:, None] * inv, col[:, None] * inv], axis=-1)
    cos2 = jnp.repeat(jnp.cos(ang), 2, axis=-1)
    sin2 = jnp.repeat(jnp.sin(ang), 2, axis=-1) * jnp.tile(jnp.array([-1.0, 1.0], F32), HEAD_DIM // 2)
    cos2 = jnp.concatenate([cos2, jnp.ones((tm, HEAD_DIM), F32)], axis=0)
    sin2 = jnp.concatenate([sin2, jnp.zeros((tm, HEAD_DIM), F32)], axis=0)
    return cos2, sin2


def _lower_bounds(lb_param):
    p = jax.nn.softmax(lb_param.astype(F32), axis=1)
    return jnp.cumsum(p, axis=1) - p[:, :1]


def kernel(x, c, ctx, c_ctx, w_ada, b_ada, norm1_g, w_in, q_norm_g, k_norm_g, hg_lower_bounds,
           hg_norm_g, sg_norm_g, sg_w, sg_b, w_out, norm2_g, w_up, conv_w, conv_b, w_down,
           final_norm_g):
    batch, t_lat, d = x.shape
    t_ctx = ctx.shape[1]
    depth = w_in.shape[0]
    n_lat, n_ctx = batch * t_lat, batch * t_ctx
    assert t_lat & (t_lat - 1) == 0 and t_ctx & (t_ctx - 1) == 0

    tm = min(1024, n_ctx)
    assert t_lat % tm == 0 and n_ctx % tm == 0

    def tile_mod(i):
        return jnp.minimum(i // (t_lat // tm), batch)

    def tile_rope(i):
        return jnp.where(i < n_lat // tm, i % (t_lat // tm), t_lat // tm)

    stream = [(x.reshape(n_lat, d), 0), (ctx.reshape(n_ctx, d), n_lat // tm)]
    cin = jnp.concatenate([c, c_ctx[None, :], jnp.zeros((8 - batch - 1, d), F32)], axis=0)
    mods = ada_table(cin, w_ada, b_ada)[:, :batch + 1, :]
    mods = mods.reshape(depth, (batch + 1) * N_MOD, 1, d)
    cos2, sin2 = _rope_tables(t_lat, tm)
    lbs = _lower_bounds(hg_lower_bounds)
    qw = ATTN_HEADS * HEAD_DIM
    kvw = ATTN_KV_HEADS * HEAD_DIM
    hg_col0 = (qw + 2 * kvw) // (HG_HEADS * HG_D)
    sg_col0 = hg_col0 + 5

    w_in_l = w_in[0].astype(BF16)

    for l in range(depth):
        mod = mods[l]
        last = l == depth - 1
        n_rows = n_lat if last else n_lat + n_ctx
        p = None
        for part, tile0 in stream:
            p = in_proj(part, norm1_g[l], mod, w_in_l, tile_mod=tile_mod, tm=tm, tile0=tile0,
                        out_rows=n_lat + n_ctx, into=p)
        q, k, vt = qk_prep(p, cos2, sin2, q_norm_g[l], k_norm_g[l], tile_rope=tile_rope, tm=tm)
        casts = [(w_out, l), (w_up, l), (w_down, l)] + ([] if last else [(w_in, l + 1)])
        attn, w_out_l, w_up_l, w_down_l, *nxt = attention(
            q, k, vt, casts, batch=batch, t_lat=t_lat, t_ctx=t_ctx, ctx_queries=not last)
        w_in_l = nxt[0] if nxt else None
        o_f = hgrn_scan(p, lbs[0, l], batch=batch, t_lat=t_lat, t_ctx=t_ctx, col0=hg_col0,
                        reverse=False)
        hg = hgrn_scan(p, lbs[1, l], batch=batch, t_lat=t_lat, t_ctx=t_ctx, col0=hg_col0,
                       reverse=True, fwd_out=o_f, gain=hg_norm_g[l])
        bias_full = jnp.repeat(sg_b[l].T, SG_DIM, axis=1)
        sg = spatial_gate(p, sg_norm_g[l], sg_w[l].astype(BF16), bias_full, col0=sg_col0,
                          n_rows=n_rows, tm=tm)
        xs = None
        for part, tile0 in stream:
            rows = min(part.shape[0], n_rows - tile0 * tm)
            if rows > 0:
                xs = out_proj(attn, hg, sg, w_out_l, part, mod, tile_mod=tile_mod, n_rows=rows,
                              tm=tm, tile0=tile0, out_rows=n_rows, into=xs)
        xs = conv_ffn(xs, norm2_g[l], mod, w_up_l, conv_w[l], conv_b[l], w_down_l,
                      tile_mod=tile_mod, t_lat=t_lat, t_ctx=t_ctx, n_lat=n_lat, n_rows=n_rows,
                      tm=tm, final_gain=final_norm_g if last else None)
        stream = [(xs, 0)]

    return xs.reshape(batch, t_lat, d)
```

```python
import functools

import jax
import jax.numpy as jnp
from jax import lax
from jax.experimental import pallas as pl
from jax.experimental.pallas import tpu as pltpu

F32 = jnp.float32
BF16 = jnp.bfloat16

EPS = 1e-6
F_MIN = 1e-30
N_MOD = 6
HEAD_DIM = 128
ATTN_HEADS = 8
ATTN_KV_HEADS = 2
ATTN_GROUP = ATTN_HEADS // ATTN_KV_HEADS
ROPE_THETA = 10000.0
GRID_W = 64
HG_HEADS = 4
HG_D = 128
SG_GROUPS = 4
SG_DIM = 128
SG_CHUNK = 128
CONV_W = 3
LANE = 128
SUBLANES = 8
LOG2_E = 1.4426950408889634
HG_CHUNK = 128
ATTN_Q_ROWS = 512
BF16_ROWS = 2 * SUBLANES
MXU_WIDTH = 256
V7X_VMEM_BYTES = 64 * 1024 * 1024
VMEM_LIMIT = V7X_VMEM_BYTES - 2 * 1024 * 1024


def _cparams(sem):
    return pltpu.CompilerParams(dimension_semantics=sem, vmem_limit_bytes=VMEM_LIMIT)


def _dot(a, b):
    return jnp.dot(a, b, preferred_element_type=F32)


def _dot_nt(a, b):
    return lax.dot_general(a, b, (((1,), (1,)), ((), ())), preferred_element_type=F32)


def _rms(x, g):
    return x * lax.rsqrt(jnp.mean(x * x, axis=-1, keepdims=True) + EPS) * g


def _silu(x):
    return x * jax.nn.sigmoid(x)


def _ada_kernel(c_ref, w_ref, b_ref, o_ref):
    s = _silu(c_ref[...]).astype(BF16)
    o_ref[...] = _dot(s, w_ref[...].astype(BF16)) + b_ref[...]


def ada_table(cin, w_ada, b_ada, tn=1024):
    depth, d, n = w_ada.shape
    assert n % tn == 0
    return pl.pallas_call(
        _ada_kernel,
        out_shape=jax.ShapeDtypeStruct((depth, 8, n), F32),
        grid=(depth, n // tn),
        in_specs=[
            pl.BlockSpec((8, d), lambda l, j: (0, 0)),
            pl.BlockSpec((None, d, tn), lambda l, j: (l, 0, j)),
            pl.BlockSpec((None, 1, tn), lambda l, j: (l, 0, j)),
        ],
        out_specs=pl.BlockSpec((None, 8, tn), lambda l, j: (l, 0, j)),
        compiler_params=_cparams(("parallel", "parallel")),
        name="ada_table",
    )(cin, w_ada, b_ada.reshape(depth, 1, n))


def _norm_mod(x, g, scale, shift):
    r = lax.rsqrt(jnp.mean(x * x, axis=-1, keepdims=True) + EPS)
    return ((x * r) * (g * (1.0 + scale)) + shift).astype(BF16)


def _in_kernel(x_ref, g_ref, sh_ref, sc_ref, w_ref, *rest, rows):
    o_ref, h_ref = rest[-2:]

    @pl.when(pl.program_id(1) == 0)
    def _():
        for r in range(0, x_ref.shape[0], rows):
            h_ref[r:r + rows, :] = _norm_mod(x_ref[r:r + rows, :], g_ref[...], sc_ref[0], sh_ref[0])

    o_ref[...] = _dot(h_ref[...], w_ref[...])


def _dest(into, out_rows, n):
    rows = into.shape[0] if into is not None else (out_rows or n)
    extra_specs = [] if into is None else [pl.BlockSpec(memory_space=pl.ANY)]
    return rows, extra_specs, [] if into is None else [into]


def in_proj(x, g, mod, w, *, tile_mod, tm, tn=1280, tile0=0, out_rows=None, into=None):
    n, d = x.shape
    cols = w.shape[1]
    assert n % tm == 0 and cols % tn == 0
    rows, extra_specs, extra = _dest(into, out_rows, n)
    return pl.pallas_call(
        functools.partial(_in_kernel, rows=min(tm, 256)),
        out_shape=jax.ShapeDtypeStruct((rows, cols), F32),
        grid=(n // tm, cols // tn),
        in_specs=[
            pl.BlockSpec((tm, d), lambda i, j: (i, 0)),
            pl.BlockSpec((1, d), lambda i, j: (0, 0)),
            pl.BlockSpec((1, 1, d), lambda i, j: (tile_mod(i + tile0) * N_MOD + 0, 0, 0)),
            pl.BlockSpec((1, 1, d), lambda i, j: (tile_mod(i + tile0) * N_MOD + 1, 0, 0)),
            pl.BlockSpec((d, tn), lambda i, j: (0, j)),
            *extra_specs,
        ],
        out_specs=pl.BlockSpec((tm, tn), lambda i, j: (i + tile0, j)),
        scratch_shapes=[pltpu.VMEM((tm, d), BF16)],
        input_output_aliases={5: 0} if extra else {},
        compiler_params=_cparams(("parallel", "arbitrary")),
        name="in_proj",
    )(x, g.reshape(1, d), mod, mod, w, *extra)


def _rope(y, cos2, sin2):
    lane = lax.broadcasted_iota(jnp.int32, y.shape, 1)
    swapped = jnp.where((lane & 1) == 0, pltpu.roll(y, LANE - 1, 1), pltpu.roll(y, 1, 1))
    return y * cos2 + swapped * sin2


def _qk_kernel(q_ref, kv_ref, cos_ref, sin_ref, qg_ref, kg_ref, qo_ref, ko_ref, vo_ref):
    cos2, sin2 = cos_ref[...], sin_ref[...]
    scale = HEAD_DIM ** -0.5 * LOG2_E
    for h in range(ATTN_HEADS):
        sl = slice(h * HEAD_DIM, (h + 1) * HEAD_DIM)
        y = _rms(q_ref[:, sl], qg_ref[...])
        qo_ref[:, sl] = (_rope(y, cos2, sin2) * scale).astype(BF16)
    for h in range(ATTN_KV_HEADS):
        sl = slice(h * HEAD_DIM, (h + 1) * HEAD_DIM)
        y = _rms(kv_ref[:, sl], kg_ref[...])
        ko_ref[:, sl] = _rope(y, cos2, sin2).astype(BF16)
    kvw = ATTN_KV_HEADS * HEAD_DIM
    vo_ref[...] = kv_ref[:, kvw:2 * kvw].T.astype(BF16)


def qk_prep(p, cos2, sin2, qg, kg, *, tile_rope, tm):
    n = p.shape[0]
    qw = ATTN_HEADS * HEAD_DIM
    kvw = ATTN_KV_HEADS * HEAD_DIM
    return pl.pallas_call(
        _qk_kernel,
        out_shape=(jax.ShapeDtypeStruct((n, qw), BF16),
                   jax.ShapeDtypeStruct((n, kvw), BF16),
                   jax.ShapeDtypeStruct((kvw, n), BF16)),
        grid=(n // tm,),
        in_specs=[
            pl.BlockSpec((tm, qw), lambda i: (i, 0)),
            pl.BlockSpec((tm, 2 * kvw), lambda i: (i, qw // (2 * kvw))),
            pl.BlockSpec((tm, HEAD_DIM), lambda i: (tile_rope(i), 0)),
            pl.BlockSpec((tm, HEAD_DIM), lambda i: (tile_rope(i), 0)),
            pl.BlockSpec((1, HEAD_DIM), lambda i: (0, 0)),
            pl.BlockSpec((1, HEAD_DIM), lambda i: (0, 0)),
        ],
        out_specs=(pl.BlockSpec((tm, qw), lambda i: (i, 0)),
                   pl.BlockSpec((tm, kvw), lambda i: (i, 0)),
                   pl.BlockSpec((kvw, tm), lambda i: (0, i))),
        compiler_params=_cparams(("parallel",)),
        name="qk_prep",
    )(p, p, cos2, sin2, qg.reshape(1, HEAD_DIM), kg.reshape(1, HEAD_DIM))


def _softmax_pv(q, segs, ck, s_ref):
    mq = q.shape[0]
    chunks = [(k_ref, vt_ref, c0, min(ck, k_ref.shape[0]))
              for k_ref, vt_ref in segs for c0 in range(0, k_ref.shape[0], min(ck, k_ref.shape[0]))]

    def scores(i):
        k_ref, _, c0, cs = chunks[i]
        s_ref[i % 2, 0:cs, :] = _dot_nt(k_ref[c0:c0 + cs, :], q)

    scores(0)
    m = l = acc = None
    for i, (_, vt_ref, c0, cs) in enumerate(chunks):
        if i + 1 < len(chunks):
            scores(i + 1)
        s = s_ref[i % 2, 0:cs, :]
        m_c = s.reshape(cs // SUBLANES, SUBLANES, mq).max(axis=0).max(axis=0, keepdims=True)
        m_new = m_c if m is None else jnp.maximum(m, m_c)
        p = jnp.exp2(s - m_new)
        p_rows = p.reshape(cs // SUBLANES, SUBLANES, mq).sum(axis=0)
        pv = _dot(vt_ref[:, c0:c0 + cs], p.astype(BF16))
        if m is None:
            l, acc = p_rows, pv
        else:
            alpha = jnp.exp2(m - m_new)
            l, acc = alpha * l + p_rows, alpha * acc + pv
        m = m_new
    return (acc / l.sum(axis=0, keepdims=True)).T


def _attn_kernel(*refs, n_seg, ck, n_cast, aliased):
    q_ref, s_ref = refs[0], refs[-1]
    segs = list(zip(refs[1:1 + n_seg], refs[1 + n_seg:1 + 2 * n_seg]))
    cast_in = refs[1 + 2 * n_seg:1 + 2 * n_seg + n_cast]
    o_ref, *cast_out = refs[1 + 2 * n_seg + n_cast + aliased:-1]
    tq = q_ref.shape[0]

    for src, dst in zip(cast_in, cast_out):
        dst[...] = src[...].astype(dst.dtype)

    hs = range(ATTN_GROUP)
    q = jnp.concatenate([q_ref[:, h * HEAD_DIM:(h + 1) * HEAD_DIM] for h in hs], axis=0)
    o = _softmax_pv(q, segs, ck, s_ref)
    for h in hs:
        o_ref[:, h * HEAD_DIM:(h + 1) * HEAD_DIM] = o[h * tq:(h + 1) * tq].astype(BF16)


def _cast_rows(total, steps):
    return next(r for r in range(BF16_ROWS, total + 1, BF16_ROWS)
                if total % r == 0 and total // r <= steps)


def attention(q, k, vt, weights, *, latent, batch, t_lat, t_ctx, tq, ck=512, into=None):
    n = q.shape[0]
    nq = (t_lat if latent else t_ctx) // tq
    q0 = 0 if latent else batch * t_lat // tq
    gw = ATTN_GROUP * HEAD_DIM
    ctx0 = batch * t_lat
    steps = batch * ATTN_KV_HEADS * nq
    ck = min(ck, t_lat if latent else t_ctx)

    def qmap(b, j, i):
        return (q0 + b * nq + i, j)

    k_specs = [pl.BlockSpec((t_ctx, HEAD_DIM), lambda b, j, i: (ctx0 // t_ctx + b, j))]
    v_specs = [pl.BlockSpec((HEAD_DIM, t_ctx), lambda b, j, i: (j, ctx0 // t_ctx + b))]
    if latent:
        k_specs.insert(0, pl.BlockSpec((t_lat, HEAD_DIM), lambda b, j, i: (b, j)))
        v_specs.insert(0, pl.BlockSpec((HEAD_DIM, t_lat), lambda b, j, i: (j, b)))
    n_seg = len(k_specs)

    def cast_specs(w, layer):
        rows = _cast_rows(w.shape[1], steps)
        last = w.shape[1] // rows - 1

        def blk(b, j, i):
            return jnp.minimum((b * ATTN_KV_HEADS + j) * nq + i, last)

        return (pl.BlockSpec((None, rows, w.shape[2]), lambda b, j, i: (layer, blk(b, j, i), 0)),
                pl.BlockSpec((rows, w.shape[2]), lambda b, j, i: (blk(b, j, i), 0)))

    specs = [cast_specs(w, layer) for w, layer in weights]
    weights = [w for w, _ in weights]
    _, extra_specs, extra = _dest(into, None, n)
    n_in = 1 + 2 * n_seg + len(weights)
    return pl.pallas_call(
        functools.partial(_attn_kernel, n_seg=n_seg, ck=ck, n_cast=len(weights),
                          aliased=len(extra)),
        out_shape=(jax.ShapeDtypeStruct((n, ATTN_HEADS * HEAD_DIM), BF16),
                   *[jax.ShapeDtypeStruct(w.shape[1:], BF16) for w in weights]),
        grid=(batch, ATTN_KV_HEADS, nq),
        in_specs=[pl.BlockSpec((tq, gw), qmap), *k_specs, *v_specs, *[s[0] for s in specs],
                  *extra_specs],
        out_specs=(pl.BlockSpec((tq, gw), qmap), *[s[1] for s in specs]),
        scratch_shapes=[pltpu.VMEM((2, ck, ATTN_GROUP * tq), F32)],
        input_output_aliases={n_in: 0} if extra else {},
        compiler_params=_cparams(("arbitrary", "arbitrary", "arbitrary")),
        name="attention" if latent else "attention_ctx",
    )(q, *[k] * n_seg, *[vt] * n_seg, *weights, *extra)


def _hgrn_chunk(hq, z, v, lb, st, *, reverse):
    c = hq.shape[0]
    sig = jax.nn.sigmoid(z)
    f = lb + (1.0 - lb) * sig
    logf = jnp.log(jnp.maximum(f, F_MIN)) * (1.0 / jnp.log(2.0))
    kk = (1.0 - lb) * (1.0 - sig)
    q = _silu(hq)

    row = lax.broadcasted_iota(jnp.int32, (c, c), 0)
    col = lax.broadcasted_iota(jnp.int32, (c, c), 1)
    rid = lax.broadcasted_iota(jnp.int32, (c, HG_D), 0)
    later = (row < col) if reverse else (row > col)

    a = jnp.where(row == col, _dot_nt(q.astype(BF16), kk.astype(BF16)), 0.0)
    cum, tot = logf, logf
    d, ld = 1, 0
    while d < c:
        mask = jnp.logical_and(((row ^ col) >> ld) == 1, later)
        if d < SUBLANES:
            second = (rid & d) != 0
            sel = jnp.logical_not(second) if reverse else second
            x = jnp.where(sel, q, kk) * jnp.exp2(jnp.where(sel, cum, tot - cum))
            t3 = tot.reshape(c // SUBLANES, SUBLANES, HG_D)
            up = pltpu.roll(t3, d, 1).reshape(c, HG_D)
            partner = up if 2 * d == SUBLANES else jnp.where(
                second, up, pltpu.roll(t3, SUBLANES - d, 1).reshape(c, HG_D))
            cum = cum + jnp.where(sel, partner, 0.0)
            tot = tot + partner
        else:
            def split(y):
                y4 = y.reshape(c // (2 * d), 2, d, HG_D)
                return (y4[:, 1], y4[:, 0]) if reverse else (y4[:, 0], y4[:, 1])

            def join(early, late):
                pair = [late, early] if reverse else [early, late]
                return jnp.stack(pair, axis=1).reshape(c, HG_D)

            (cum_e, cum_l), (tot_e, tot_l) = split(cum), split(tot)
            x = join(split(kk)[0] * jnp.exp2(tot_e - cum_e), split(q)[1] * jnp.exp2(cum_l))
            both = tot_e + tot_l
            cum, tot = join(cum_e, cum_l + tot_e), join(both, both)
        xb = x.astype(BF16)
        a = jnp.where(mask, _dot_nt(xb, xb), a)
        d, ld = d * 2, ld + 1

    qd = (q * jnp.exp2(cum)).astype(BF16)
    kd = (kk * jnp.exp2(tot - cum)).astype(BF16)
    vb = v.astype(BF16)
    o = _dot(a.astype(BF16), vb) + _dot_nt(qd, st.astype(BF16))
    st_new = st * jnp.exp2(tot[0:1, :]) + _dot(vb.T, kd)
    return o, st_new


def _hgrn_kernel(*refs, reverse, finalize):
    if finalize:
        hq_ref, hf_ref, hi_ref, lb_ref, of_ref, gt_ref, g_ref, o_ref, st_ref = refs
    else:
        hq_ref, hf_ref, hi_ref, lb_ref, o_ref, st_ref = refs

    @pl.when(pl.program_id(1) == 0)
    def _():
        st_ref[...] = jnp.zeros_like(st_ref)

    starts = range(0, hq_ref.shape[0], HG_CHUNK)
    for r in (reversed(starts) if reverse else starts):
        rs = slice(r, r + HG_CHUNK)
        for h in range(HG_HEADS):
            sl = slice(h * HG_D, (h + 1) * HG_D)
            o, st_new = _hgrn_chunk(hq_ref[rs, sl], hf_ref[rs, sl], hi_ref[rs, sl], lb_ref[:, sl],
                                    st_ref[h], reverse=reverse)
            st_ref[h] = st_new
            if finalize:
                y = _rms(o + of_ref[rs, sl], g_ref[...])
                o_ref[rs, sl] = (y * _silu(gt_ref[rs, sl])).astype(o_ref.dtype)
            else:
                o_ref[rs, sl] = o


def hgrn_scan(p, lb, *, batch, t_lat, t_ctx, col0, reverse, fwd_out=None, gain=None,
              chunks_per_step=2):
    n = p.shape[0]
    c = HG_CHUNK * chunks_per_step
    assert t_lat % c == 0 and t_ctx % c == 0
    w = HG_HEADS * HG_D
    n_lat, n_ctx = t_lat // c, t_ctx // c
    ctx0 = batch * n_lat

    def rows(b, s):
        if reverse:
            return jnp.where(s < n_ctx, ctx0 + b * n_ctx + (n_ctx - 1 - s),
                             b * n_lat + (n_lat - 1 - (s - n_ctx)))
        return jnp.where(s < n_ctx, ctx0 + b * n_ctx + s, b * n_lat + (s - n_ctx))

    def spec(cb):
        return pl.BlockSpec((c, w), lambda b, s: (rows(b, s), cb))

    finalize = fwd_out is not None
    in_specs = [spec(col0), spec(col0 + (2 if reverse else 1)), spec(col0 + 3),
                pl.BlockSpec((1, w), lambda b, s: (0, 0))]
    args = [p, p, p, lb.reshape(1, w)]
    if finalize:
        in_specs += [spec(0), spec(col0 + 4), pl.BlockSpec((1, HG_D), lambda b, s: (0, 0))]
        args += [fwd_out, p, gain.reshape(1, HG_D)]
    return pl.pallas_call(
        functools.partial(_hgrn_kernel, reverse=reverse, finalize=finalize),
        out_shape=jax.ShapeDtypeStruct((n, w), BF16 if finalize else F32),
        grid=(batch, n_lat + n_ctx),
        in_specs=in_specs,
        out_specs=spec(0),
        scratch_shapes=[pltpu.VMEM((HG_HEADS, HG_D, HG_D), F32)],
        compiler_params=_cparams(("parallel", "arbitrary")),
        name="hgrn_bwd" if reverse else "hgrn_fwd",
    )(*args)


def _sg_kernel(u_ref, v_ref, g_ref, w_ref, b_ref, o_ref):
    for r in range(0, u_ref.shape[0], SG_CHUNK):
        rs = slice(r, r + SG_CHUNK)
        for g in range(SG_GROUPS):
            sl = slice(g * SG_DIM, (g + 1) * SG_DIM)
            vn = _rms(jax.nn.gelu(v_ref[rs, sl]), g_ref[:, sl])
            mixed = _dot(w_ref[g], vn.astype(BF16)) + b_ref[:, sl]
            o_ref[rs, sl] = (jax.nn.gelu(u_ref[rs, sl]) * mixed).astype(BF16)


def spatial_gate(p, g, w, bias_full, *, col0, n_rows, tm):
    sw = SG_GROUPS * SG_DIM
    assert n_rows % tm == 0
    return pl.pallas_call(
        _sg_kernel,
        out_shape=jax.ShapeDtypeStruct((n_rows, sw), BF16),
        grid=(n_rows // tm,),
        in_specs=[
            pl.BlockSpec((tm, sw), lambda i: (i, col0)),
            pl.BlockSpec((tm, sw), lambda i: (i, col0 + 1)),
            pl.BlockSpec((1, sw), lambda i: (0, 0)),
            pl.BlockSpec((SG_GROUPS, SG_CHUNK, SG_CHUNK), lambda i: (0, 0, 0)),
            pl.BlockSpec((SG_CHUNK, sw), lambda i: (0, 0)),
        ],
        out_specs=pl.BlockSpec((tm, sw), lambda i: (i, 0)),
        compiler_params=_cparams(("parallel",)),
        name="spatial_gate",
    )(p, p, g.reshape(1, sw), w, bias_full)


def _out_kernel(a_ref, h_ref, s_ref, wa_ref, wh_ref, ws_ref, x_ref, gate_ref, *rest):
    o_ref = rest[-1]
    acc = _dot(a_ref[...], wa_ref[...]) + _dot(h_ref[...], wh_ref[...]) + _dot(s_ref[...], ws_ref[...])
    o_ref[...] = x_ref[...] + gate_ref[0] * acc


def out_proj(attn, hg, sg, w, x, mod, *, tile_mod, n_rows, tm, tn=2048, tile0=0, out_rows=None,
             into=None):
    n, d = n_rows, x.shape[1]
    tn = min(tn, d)
    wa, wh, ws = attn.shape[1], hg.shape[1], sg.shape[1]
    assert wa % wh == 0 and wh == ws and n % tm == 0 and d % tn == 0
    rows, extra_specs, extra = _dest(into, out_rows, n)
    return pl.pallas_call(
        _out_kernel,
        out_shape=jax.ShapeDtypeStruct((rows, d), F32),
        grid=(n // tm, d // tn),
        in_specs=[
            pl.BlockSpec((tm, wa), lambda i, j: (i + tile0, 0)),
            pl.BlockSpec((tm, wh), lambda i, j: (i + tile0, 0)),
            pl.BlockSpec((tm, ws), lambda i, j: (i + tile0, 0)),
            pl.BlockSpec((wa, tn), lambda i, j: (0, j)),
            pl.BlockSpec((wh, tn), lambda i, j: (wa // wh, j)),
            pl.BlockSpec((ws, tn), lambda i, j: (wa // wh + 1, j)),
            pl.BlockSpec((tm, tn), lambda i, j: (i, j)),
            pl.BlockSpec((1, 1, tn), lambda i, j: (tile_mod(i + tile0) * N_MOD + 2, 0, j)),
            *extra_specs,
        ],
        out_specs=pl.BlockSpec((tm, tn), lambda i, j: (i + tile0, j)),
        input_output_aliases={8: 0} if extra else {},
        compiler_params=_cparams(("parallel", "parallel")),
        name="out_proj",
    )(attn, hg, sg, w, w, w, x, mod, *extra)


def _ffn_kernel(*refs, tile_seq, n_long, has_short, final, rows, chunk):
    (x_ref, xp_ref, xn_ref, g_ref, sh_ref, sc_ref, gate_ref,
     wg_ref, wv_ref, cwg_ref, cwv_ref, cbg_ref, cbv_ref, wd_ref) = refs[:14]
    gf_ref = refs[14] if final else None
    o_ref, h_ref, u_ref = refs[-3:]
    i, j = pl.program_id(0), pl.program_id(1)
    tm = x_ref.shape[0]
    halo = xp_ref.shape[0]

    def norm_mod(x):
        return _norm_mod(x, g_ref[...], sc_ref[0], sh_ref[0])

    @pl.when(j == 0)
    def _():
        for r in range(0, tm, rows):
            h_ref[r:r + rows, :] = norm_mod(x_ref[r:r + rows, :])
        h_ref[tm:tm + halo, :] = norm_mod(xp_ref[...])
        h_ref[tm + halo:tm + 2 * halo, :] = norm_mod(xn_ref[...])
        o_ref[...] = jnp.zeros_like(o_ref)

    chunks = [slice(c0, c0 + chunk) for c0 in range(0, wd_ref.shape[0], chunk)]
    rid = lax.broadcasted_iota(jnp.int32, (tm, 1), 0)

    def step(interior):
        seq = tile_seq(i)
        tile_starts = ((i * tm) & (seq - 1)) == 0
        tile_ends = (((i + 1) * tm) & (seq - 1)) == 0
        if interior:
            pos = (i * tm + rid) & (seq - 1)
            first, last = pos == 0, pos == seq - 1

        for c, cs in enumerate(chunks):
            u_ref[c, 0] = _dot(h_ref[...], wg_ref[:, cs])
            u_ref[c, 1] = _dot(h_ref[...], wv_ref[:, cs])

        def conv(c, k, cw_ref, cb_ref, cs):
            u = u_ref[c, k, 0:tm, :]
            prev_row = jnp.where(tile_starts, 0.0, u_ref[c, k, tm + halo - 1:tm + halo, :])
            next_row = jnp.where(tile_ends, 0.0, u_ref[c, k, tm + halo:tm + halo + 1, :])
            before = jnp.where(rid == 0, prev_row, pltpu.roll(u, 1, 0))
            after = jnp.where(rid == tm - 1, next_row, pltpu.roll(u, tm - 1, 0))
            if interior:
                before = jnp.where(first, 0.0, before)
                after = jnp.where(last, 0.0, after)
            return (cb_ref[:, cs] + before * cw_ref[0:1, cs] + u * cw_ref[1:2, cs]
                    + after * cw_ref[2:3, cs])

        for c, cs in enumerate(chunks):
            act = _silu(conv(c, 0, cwg_ref, cbg_ref, cs)) * conv(c, 1, cwv_ref, cbv_ref, cs)
            o_ref[...] += _dot(act.astype(BF16), wd_ref[cs, :])

    if has_short:
        pl.when(i < n_long)(functools.partial(step, False))
        pl.when(i >= n_long)(functools.partial(step, True))
    else:
        step(False)

    @pl.when(j == pl.num_programs(1) - 1)
    def _():
        def finish(r, carry):
            rs = pl.ds(pl.multiple_of(r * rows, rows), rows)
            y = x_ref[rs, :] + gate_ref[0] * o_ref[rs, :]
            o_ref[rs, :] = _rms(y, gf_ref[...]) if final else y
            return carry

        lax.fori_loop(0, tm // rows, finish, 0)


def conv_ffn(x, g, mod, w_up, conv_w, conv_b, w_down, *, tile_mod, t_lat, t_ctx, n_lat,
             n_rows, tm, tf=512, final_gain=None):
    n, d = x.shape
    dff = w_down.shape[0]
    assert n_rows % tm == 0 and n_lat % tm == 0 and t_lat % tm == 0 and dff % tf == 0
    assert n_rows == n_lat or tm % t_ctx == 0 or t_ctx % tm == 0
    nj = dff // tf
    halo = SUBLANES
    chunk = min(MXU_WIDTH, tf)
    hb = tm // halo
    nhb = n // halo
    n_long = n_lat // tm
    cb = conv_b.reshape(1, 2 * dff)
    final = final_gain is not None
    in_specs = [
        pl.BlockSpec((tm, d), lambda i, j: (i, 0)),
        pl.BlockSpec((halo, d), lambda i, j: (jnp.maximum(i * hb - 1, 0), 0)),
        pl.BlockSpec((halo, d), lambda i, j: (jnp.minimum((i + 1) * hb, nhb - 1), 0)),
        pl.BlockSpec((1, d), lambda i, j: (0, 0)),
        pl.BlockSpec((1, 1, d), lambda i, j: (tile_mod(i) * N_MOD + 3, 0, 0)),
        pl.BlockSpec((1, 1, d), lambda i, j: (tile_mod(i) * N_MOD + 4, 0, 0)),
        pl.BlockSpec((1, 1, d), lambda i, j: (tile_mod(i) * N_MOD + 5, 0, 0)),
        pl.BlockSpec((d, tf), lambda i, j: (0, j)),
        pl.BlockSpec((d, tf), lambda i, j: (0, nj + j)),
        pl.BlockSpec((CONV_W, tf), lambda i, j: (0, j)),
        pl.BlockSpec((CONV_W, tf), lambda i, j: (0, nj + j)),
        pl.BlockSpec((1, tf), lambda i, j: (0, j)),
        pl.BlockSpec((1, tf), lambda i, j: (0, nj + j)),
        pl.BlockSpec((tf, d), lambda i, j: (j, 0)),
    ]
    args = [x, x, x, g.reshape(1, d), mod, mod, mod, w_up, w_up, conv_w, conv_w, cb, cb, w_down]
    if final:
        in_specs.append(pl.BlockSpec((1, d), lambda i, j: (0, 0)))
        args.append(final_gain.reshape(1, d))
    return pl.pallas_call(
        functools.partial(
            _ffn_kernel, tile_seq=lambda i: jnp.where(i < n_long, t_lat, t_ctx), n_long=n_long,
            has_short=n_rows > n_lat and t_ctx < tm, final=final, rows=min(tm, 256), chunk=chunk),
        out_shape=jax.ShapeDtypeStruct((n_rows, d), F32),
        grid=(n_rows // tm, nj),
        in_specs=in_specs,
        out_specs=pl.BlockSpec((tm, d), lambda i, j: (i, 0)),
        scratch_shapes=[pltpu.VMEM((tm + 2 * halo, d), BF16),
                        pltpu.VMEM((tf // chunk, 2, tm + 2 * halo, chunk), F32)],
        compiler_params=_cparams(("parallel", "arbitrary")),
        name="conv_ffn",
    )(*args)


def _rope_tables(t_lat, tm):
    rows = t_lat // GRID_W
    row = jnp.repeat(jnp.arange(rows, dtype=F32), GRID_W)
    col = jnp.tile(jnp.arange(GRID_W, dtype=F32), rows)
    n_freq = HEAD_DIM // 4
    inv = ROPE_THETA ** (-jnp.arange(n_freq, dtype=F32) / n_freq)
    ang = jnp.concatenate([row[:, None] * inv, col[:, None] * inv], axis=-1)
    cos2 = jnp.repeat(jnp.cos(ang), 2, axis=-1)
    sin2 = jnp.repeat(jnp.sin(ang), 2, axis=-1) * jnp.tile(jnp.array([-1.0, 1.0], F32), HEAD_DIM // 2)
    cos2 = jnp.concatenate([cos2, jnp.ones((tm, HEAD_DIM), F32)], axis=0)
    sin2 = jnp.concatenate([sin2, jnp.zeros((tm, HEAD_DIM), F32)], axis=0)
    return cos2, sin2


def _lower_bounds(lb_param):
    p = jax.nn.softmax(lb_param.astype(F32), axis=1)
    return jnp.cumsum(p, axis=1) - p[:, :1]


def kernel(x, c, ctx, c_ctx, w_ada, b_ada, norm1_g, w_in, q_norm_g, k_norm_g, hg_lower_bounds,
           hg_norm_g, sg_norm_g, sg_w, sg_b, w_out, norm2_g, w_up, conv_w, conv_b, w_down,
           final_norm_g):
    batch, t_lat, d = x.shape
    t_ctx = ctx.shape[1]
    depth = w_in.shape[0]
    n_lat, n_ctx = batch * t_lat, batch * t_ctx
    assert t_lat & (t_lat - 1) == 0 and t_ctx & (t_ctx - 1) == 0

    tm = min(1024, n_ctx)
    assert t_lat % tm == 0 and n_ctx % tm == 0

    def tile_mod(i):
        return jnp.minimum(i // (t_lat // tm), batch)

    def tile_rope(i):
        return jnp.where(i < n_lat // tm, i % (t_lat // tm), t_lat // tm)

    stream = [(x.reshape(n_lat, d), 0), (ctx.reshape(n_ctx, d), n_lat // tm)]
    cin = jnp.concatenate([c, c_ctx[None, :], jnp.zeros((8 - batch - 1, d), F32)], axis=0)
    mods = ada_table(cin, w_ada, b_ada)[:, :batch + 1, :]
    mods = mods.reshape(depth, (batch + 1) * N_MOD, 1, d)
    cos2, sin2 = _rope_tables(t_lat, tm)
    lbs = _lower_bounds(hg_lower_bounds)
    qw = ATTN_HEADS * HEAD_DIM
    kvw = ATTN_KV_HEADS * HEAD_DIM
    hg_col0 = (qw + 2 * kvw) // (HG_HEADS * HG_D)
    sg_col0 = hg_col0 + 5

    w_in_l = w_in[0].astype(BF16)

    for l in range(depth):
        mod = mods[l]
        last = l == depth - 1
        n_rows = n_lat if last else n_lat + n_ctx
        p = None
        for part, tile0 in stream:
            p = in_proj(part, norm1_g[l], mod, w_in_l, tile_mod=tile_mod, tm=tm, tile0=tile0,
                        out_rows=n_lat + n_ctx, into=p)
        q, k, vt = qk_prep(p, cos2, sin2, q_norm_g[l], k_norm_g[l], tile_rope=tile_rope, tm=tm)
        casts = [(w_out, l), (w_up, l), (w_down, l)] + ([] if last else [(w_in, l + 1)])
        attn, w_out_l, w_up_l, w_down_l, *nxt = attention(
            q, k, vt, casts, latent=True, batch=batch, t_lat=t_lat, t_ctx=t_ctx,
            tq=min(ATTN_Q_ROWS, t_lat))
        w_in_l = nxt[0] if nxt else None
        if not last:
            attn, = attention(q, k, vt, [], latent=False, batch=batch, t_lat=t_lat, t_ctx=t_ctx,
                              tq=min(ATTN_Q_ROWS, t_ctx), into=attn)
        o_f = hgrn_scan(p, lbs[0, l], batch=batch, t_lat=t_lat, t_ctx=t_ctx, col0=hg_col0,
                        reverse=False)
        hg = hgrn_scan(p, lbs[1, l], batch=batch, t_lat=t_lat, t_ctx=t_ctx, col0=hg_col0,
                       reverse=True, fwd_out=o_f, gain=hg_norm_g[l])
        bias_full = jnp.repeat(sg_b[l].T, SG_DIM, axis=1)
        sg = spatial_gate(p, sg_norm_g[l], sg_w[l].astype(BF16), bias_full, col0=sg_col0,
                          n_rows=n_rows, tm=tm)
        xs = None
        for part, tile0 in stream:
            rows = min(part.shape[0], n_rows - tile0 * tm)
            if rows > 0:
                xs = out_proj(attn, hg, sg, w_out_l, part, mod, tile_mod=tile_mod, n_rows=rows,
                              tm=tm, tile0=tile0, out_rows=n_rows, into=xs)
        xs = conv_ffn(xs, norm2_g[l], mod, w_up_l, conv_w[l], conv_b[l], w_down_l,
                      tile_mod=tile_mod, t_lat=t_lat, t_ctx=t_ctx, n_lat=n_lat, n_rows=n_rows,
                      tm=tm, final_gain=final_norm_g if last else None)
        stream = [(xs, 0)]

    return xs.reshape(batch, t_lat, d)
```

```python
import functools

import jax
import jax.numpy as jnp
from jax import lax
from jax.experimental import pallas as pl
from jax.experimental.pallas import tpu as pltpu

F32 = jnp.float32
BF16 = jnp.bfloat16

EPS = 1e-6
F_MIN = 1e-30
N_MOD = 6
HEAD_DIM = 128
ATTN_HEADS = 8
ATTN_KV_HEADS = 2
ATTN_GROUP = ATTN_HEADS // ATTN_KV_HEADS
ROPE_THETA = 10000.0
GRID_W = 64
HG_HEADS = 4
HG_D = 128
SG_GROUPS = 4
SG_DIM = 128
SG_CHUNK = 128
CONV_W = 3
LANE = 128
SUBLANES = 8
LOG2_E = 1.4426950408889634
HG_CHUNK = 128
BF16_ROWS = 2 * SUBLANES
MXU_WIDTH = 256
V7X_VMEM_BYTES = 64 * 1024 * 1024
VMEM_LIMIT = V7X_VMEM_BYTES - 2 * 1024 * 1024


def _cparams(sem):
    return pltpu.CompilerParams(dimension_semantics=sem, vmem_limit_bytes=VMEM_LIMIT)


def _dot(a, b):
    return jnp.dot(a, b, preferred_element_type=F32)


def _dot_nt(a, b):
    return lax.dot_general(a, b, (((1,), (1,)), ((), ())), preferred_element_type=F32)


def _rms(x, g):
    return x * lax.rsqrt(jnp.mean(x * x, axis=-1, keepdims=True) + EPS) * g


def _silu(x):
    return x * jax.nn.sigmoid(x)


def _ada_kernel(c_ref, w_ref, b_ref, o_ref):
    s = _silu(c_ref[...]).astype(BF16)
    o_ref[...] = _dot(s, w_ref[...].astype(BF16)) + b_ref[...]


def ada_table(cin, w_ada, b_ada, tn=1024):
    depth, d, n = w_ada.shape
    assert n % tn == 0
    return pl.pallas_call(
        _ada_kernel,
        out_shape=jax.ShapeDtypeStruct((depth, 8, n), F32),
        grid=(depth, n // tn),
        in_specs=[
            pl.BlockSpec((8, d), lambda l, j: (0, 0)),
            pl.BlockSpec((None, d, tn), lambda l, j: (l, 0, j)),
            pl.BlockSpec((None, 1, tn), lambda l, j: (l, 0, j)),
        ],
        out_specs=pl.BlockSpec((None, 8, tn), lambda l, j: (l, 0, j)),
        compiler_params=_cparams(("parallel", "parallel")),
        name="ada_table",
    )(cin, w_ada, b_ada.reshape(depth, 1, n))


def _norm_mod(x, g, scale, shift):
    r = lax.rsqrt(jnp.mean(x * x, axis=-1, keepdims=True) + EPS)
    return ((x * r) * (g * (1.0 + scale)) + shift).astype(BF16)


def _in_kernel(x_ref, g_ref, sh_ref, sc_ref, w_ref, o_ref, h_ref, *, rows):
    @pl.when(pl.program_id(1) == 0)
    def _():
        for r in range(0, x_ref.shape[0], rows):
            h_ref[r:r + rows, :] = _norm_mod(x_ref[r:r + rows, :], g_ref[...], sc_ref[0], sh_ref[0])

    o_ref[...] = _dot(h_ref[...], w_ref[...])


def in_proj(x, g, mod, w, *, tile_mod, tm, tn=1280):
    n, d = x.shape
    cols = w.shape[1]
    assert n % tm == 0 and cols % tn == 0
    return pl.pallas_call(
        functools.partial(_in_kernel, rows=min(tm, 256)),
        out_shape=jax.ShapeDtypeStruct((n, cols), F32),
        grid=(n // tm, cols // tn),
        in_specs=[
            pl.BlockSpec((tm, d), lambda i, j: (i, 0)),
            pl.BlockSpec((1, d), lambda i, j: (0, 0)),
            pl.BlockSpec((1, 1, d), lambda i, j: (tile_mod(i) * N_MOD + 0, 0, 0)),
            pl.BlockSpec((1, 1, d), lambda i, j: (tile_mod(i) * N_MOD + 1, 0, 0)),
            pl.BlockSpec((d, tn), lambda i, j: (0, j)),
        ],
        out_specs=pl.BlockSpec((tm, tn), lambda i, j: (i, j)),
        scratch_shapes=[pltpu.VMEM((tm, d), BF16)],
        compiler_params=_cparams(("parallel", "arbitrary")),
        name="in_proj",
    )(x, g.reshape(1, d), mod, mod, w)


def _rope(y, cos2, sin2):
    lane = lax.broadcasted_iota(jnp.int32, y.shape, 1)
    swapped = jnp.where((lane & 1) == 0, pltpu.roll(y, LANE - 1, 1), pltpu.roll(y, 1, 1))
    return y * cos2 + swapped * sin2


def _qk_kernel(q_ref, kv_ref, cos_ref, sin_ref, qg_ref, kg_ref, qo_ref, ko_ref, vo_ref):
    cos2, sin2 = cos_ref[...], sin_ref[...]
    scale = HEAD_DIM ** -0.5 * LOG2_E
    for h in range(ATTN_HEADS):
        sl = slice(h * HEAD_DIM, (h + 1) * HEAD_DIM)
        y = _rms(q_ref[:, sl], qg_ref[...])
        qo_ref[:, sl] = (_rope(y, cos2, sin2) * scale).astype(BF16)
    for h in range(ATTN_KV_HEADS):
        sl = slice(h * HEAD_DIM, (h + 1) * HEAD_DIM)
        y = _rms(kv_ref[:, sl], kg_ref[...])
        ko_ref[:, sl] = _rope(y, cos2, sin2).astype(BF16)
    kvw = ATTN_KV_HEADS * HEAD_DIM
    vo_ref[...] = kv_ref[:, kvw:2 * kvw].T.astype(BF16)


def qk_prep(p, cos2, sin2, qg, kg, *, tile_rope, tm):
    n = p.shape[0]
    qw = ATTN_HEADS * HEAD_DIM
    kvw = ATTN_KV_HEADS * HEAD_DIM
    return pl.pallas_call(
        _qk_kernel,
        out_shape=(jax.ShapeDtypeStruct((n, qw), BF16),
                   jax.ShapeDtypeStruct((n, kvw), BF16),
                   jax.ShapeDtypeStruct((kvw, n), BF16)),
        grid=(n // tm,),
        in_specs=[
            pl.BlockSpec((tm, qw), lambda i: (i, 0)),
            pl.BlockSpec((tm, 2 * kvw), lambda i: (i, qw // (2 * kvw))),
            pl.BlockSpec((tm, HEAD_DIM), lambda i: (tile_rope(i), 0)),
            pl.BlockSpec((tm, HEAD_DIM), lambda i: (tile_rope(i), 0)),
            pl.BlockSpec((1, HEAD_DIM), lambda i: (0, 0)),
            pl.BlockSpec((1, HEAD_DIM), lambda i: (0, 0)),
        ],
        out_specs=(pl.BlockSpec((tm, qw), lambda i: (i, 0)),
                   pl.BlockSpec((tm, kvw), lambda i: (i, 0)),
                   pl.BlockSpec((kvw, tm), lambda i: (0, i))),
        compiler_params=_cparams(("parallel",)),
        name="qk_prep",
    )(p, p, cos2, sin2, qg.reshape(1, HEAD_DIM), kg.reshape(1, HEAD_DIM))


def _softmax_pv(q, segs, ck, s_ref):
    mq = q.shape[0]
    chunks = [(k_ref, vt_ref, c0, min(ck, k_ref.shape[0]))
              for k_ref, vt_ref in segs for c0 in range(0, k_ref.shape[0], min(ck, k_ref.shape[0]))]

    def scores(i):
        k_ref, _, c0, cs = chunks[i]
        s_ref[i % 2, 0:cs, :] = _dot_nt(k_ref[c0:c0 + cs, :], q)

    scores(0)
    m = l = acc = None
    for i, (_, vt_ref, c0, cs) in enumerate(chunks):
        if i + 1 < len(chunks):
            scores(i + 1)
        s = s_ref[i % 2, 0:cs, :]
        m_c = s.reshape(cs // SUBLANES, SUBLANES, mq).max(axis=0).max(axis=0, keepdims=True)
        m_new = m_c if m is None else jnp.maximum(m, m_c)
        p = jnp.exp2(s - m_new)
        p_rows = p.reshape(cs // SUBLANES, SUBLANES, mq).sum(axis=0)
        pv = _dot(vt_ref[:, c0:c0 + cs], p.astype(BF16))
        if m is None:
            l, acc = p_rows, pv
        else:
            alpha = jnp.exp2(m - m_new)
            l, acc = alpha * l + p_rows, alpha * acc + pv
        m = m_new
    return (acc / l.sum(axis=0, keepdims=True)).T


def _attn_kernel(*refs, n_lat, ck, n_cast):
    q_ref, kl_ref, kc_ref, vl_ref, vc_ref = refs[:5]
    o_ref, s_ref = refs[5 + n_cast], refs[-1]
    qi = pl.program_id(2)
    tq = q_ref.shape[0]

    for src, dst in zip(refs[5:5 + n_cast], refs[6 + n_cast:6 + 2 * n_cast]):
        dst[...] = src[...].astype(dst.dtype)

    def run(segs):
        hs = range(ATTN_GROUP)
        q = jnp.concatenate([q_ref[:, h * HEAD_DIM:(h + 1) * HEAD_DIM] for h in hs], axis=0)
        o = _softmax_pv(q, segs, ck, s_ref)
        for h in hs:
            o_ref[:, h * HEAD_DIM:(h + 1) * HEAD_DIM] = o[h * tq:(h + 1) * tq].astype(BF16)

    @pl.when(qi < n_lat)
    def _():
        run([(kl_ref, vl_ref), (kc_ref, vc_ref)])

    @pl.when(qi >= n_lat)
    def _():
        run([(kc_ref, vc_ref)])


def _cast_rows(total, steps):
    return next(r for r in range(BF16_ROWS, total + 1, BF16_ROWS)
                if total % r == 0 and total // r <= steps)


def attention(q, k, vt, weights, *, batch, t_lat, t_ctx, ctx_queries=True, tq=256, ck=512):
    n = q.shape[0]
    n_lat = t_lat // tq
    n_ctx = t_ctx // tq if ctx_queries else 0
    nq = n_lat + n_ctx
    gw = ATTN_GROUP * HEAD_DIM
    ctx0 = batch * t_lat
    steps = batch * ATTN_KV_HEADS * nq

    def qmap(b, j, i):
        return (jnp.where(i < n_lat, b * n_lat + i, ctx0 // tq + b * (t_ctx // tq) + (i - n_lat)), j)

    def cast_specs(w, layer):
        rows = _cast_rows(w.shape[1], steps)
        last = w.shape[1] // rows - 1

        def blk(b, j, i):
            return jnp.minimum((b * ATTN_KV_HEADS + j) * nq + i, last)

        return (pl.BlockSpec((None, rows, w.shape[2]), lambda b, j, i: (layer, blk(b, j, i), 0)),
                pl.BlockSpec((rows, w.shape[2]), lambda b, j, i: (blk(b, j, i), 0)))

    specs = [cast_specs(w, layer) for w, layer in weights]
    weights = [w for w, _ in weights]
    return pl.pallas_call(
        functools.partial(_attn_kernel, n_lat=n_lat, ck=ck, n_cast=len(weights)),
        out_shape=(jax.ShapeDtypeStruct((n if ctx_queries else ctx0, ATTN_HEADS * HEAD_DIM), BF16),
                   *[jax.ShapeDtypeStruct(w.shape[1:], BF16) for w in weights]),
        grid=(batch, ATTN_KV_HEADS, nq),
        in_specs=[
            pl.BlockSpec((tq, gw), qmap),
            pl.BlockSpec((t_lat, HEAD_DIM), lambda b, j, i: (b, j)),
            pl.BlockSpec((t_ctx, HEAD_DIM), lambda b, j, i: (ctx0 // t_ctx + b, j)),
            pl.BlockSpec((HEAD_DIM, t_lat), lambda b, j, i: (j, b)),
            pl.BlockSpec((HEAD_DIM, t_ctx), lambda b, j, i: (j, ctx0 // t_ctx + b)),
            *[s[0] for s in specs],
        ],
        out_specs=(pl.BlockSpec((tq, gw), qmap), *[s[1] for s in specs]),
        scratch_shapes=[pltpu.VMEM((2, ck, ATTN_GROUP * tq), F32)],
        compiler_params=_cparams(("arbitrary", "arbitrary", "arbitrary")),
        name="attention",
    )(q, k, k, vt, vt, *weights)


def _hgrn_chunk(hq, z, v, lb, st, *, reverse):
    c = hq.shape[0]
    sig = jax.nn.sigmoid(z)
    f = lb + (1.0 - lb) * sig
    logf = jnp.log(jnp.maximum(f, F_MIN)) * (1.0 / jnp.log(2.0))
    kk = (1.0 - lb) * (1.0 - sig)
    q = _silu(hq)

    row = lax.broadcasted_iota(jnp.int32, (c, c), 0)
    col = lax.broadcasted_iota(jnp.int32, (c, c), 1)
    rid = lax.broadcasted_iota(jnp.int32, (c, HG_D), 0)
    later = (row < col) if reverse else (row > col)

    a = jnp.where(row == col, _dot_nt(q.astype(BF16), kk.astype(BF16)), 0.0)
    cum, tot = logf, logf
    d, ld = 1, 0
    while d < c:
        mask = jnp.logical_and(((row ^ col) >> ld) == 1, later)
        if d < SUBLANES:
            second = (rid & d) != 0
            sel = jnp.logical_not(second) if reverse else second
            x = jnp.where(sel, q, kk) * jnp.exp2(jnp.where(sel, cum, tot - cum))
            t3 = tot.reshape(c // SUBLANES, SUBLANES, HG_D)
            up = pltpu.roll(t3, d, 1).reshape(c, HG_D)
            partner = up if 2 * d == SUBLANES else jnp.where(
                second, up, pltpu.roll(t3, SUBLANES - d, 1).reshape(c, HG_D))
            cum = cum + jnp.where(sel, partner, 0.0)
            tot = tot + partner
        else:
            def split(y):
                y4 = y.reshape(c // (2 * d), 2, d, HG_D)
                return (y4[:, 1], y4[:, 0]) if reverse else (y4[:, 0], y4[:, 1])

            def join(early, late):
                pair = [late, early] if reverse else [early, late]
                return jnp.stack(pair, axis=1).reshape(c, HG_D)

            (cum_e, cum_l), (tot_e, tot_l) = split(cum), split(tot)
            x = join(split(kk)[0] * jnp.exp2(tot_e - cum_e), split(q)[1] * jnp.exp2(cum_l))
            both = tot_e + tot_l
            cum, tot = join(cum_e, cum_l + tot_e), join(both, both)
        xb = x.astype(BF16)
        a = jnp.where(mask, _dot_nt(xb, xb), a)
        d, ld = d * 2, ld + 1

    qd = (q * jnp.exp2(cum)).astype(BF16)
    kd = (kk * jnp.exp2(tot - cum)).astype(BF16)
    vb = v.astype(BF16)
    o = _dot(a.astype(BF16), vb) + _dot_nt(qd, st.astype(BF16))
    st_new = st * jnp.exp2(tot[0:1, :]) + _dot(vb.T, kd)
    return o, st_new


def _hgrn_kernel(*refs, reverse, finalize):
    if finalize:
        hq_ref, hf_ref, hi_ref, lb_ref, of_ref, gt_ref, g_ref, o_ref, st_ref = refs
    else:
        hq_ref, hf_ref, hi_ref, lb_ref, o_ref, st_ref = refs

    @pl.when(pl.program_id(1) == 0)
    def _():
        st_ref[...] = jnp.zeros_like(st_ref)

    starts = range(0, hq_ref.shape[0], HG_CHUNK)
    for r in (reversed(starts) if reverse else starts):
        rs = slice(r, r + HG_CHUNK)
        for h in range(HG_HEADS):
            sl = slice(h * HG_D, (h + 1) * HG_D)
            o, st_new = _hgrn_chunk(hq_ref[rs, sl], hf_ref[rs, sl], hi_ref[rs, sl], lb_ref[:, sl],
                                    st_ref[h], reverse=reverse)
            st_ref[h] = st_new
            if finalize:
                y = _rms(o + of_ref[rs, sl], g_ref[...])
                o_ref[rs, sl] = (y * _silu(gt_ref[rs, sl])).astype(o_ref.dtype)
            else:
                o_ref[rs, sl] = o


def hgrn_scan(p, lb, *, batch, t_lat, t_ctx, col0, reverse, fwd_out=None, gain=None,
              chunks_per_step=2):
    n = p.shape[0]
    c = HG_CHUNK * chunks_per_step
    assert t_lat % c == 0 and t_ctx % c == 0
    w = HG_HEADS * HG_D
    n_lat, n_ctx = t_lat // c, t_ctx // c
    ctx0 = batch * n_lat

    def rows(b, s):
        if reverse:
            return jnp.where(s < n_ctx, ctx0 + b * n_ctx + (n_ctx - 1 - s),
                             b * n_lat + (n_lat - 1 - (s - n_ctx)))
        return jnp.where(s < n_ctx, ctx0 + b * n_ctx + s, b * n_lat + (s - n_ctx))

    def spec(cb):
        return pl.BlockSpec((c, w), lambda b, s: (rows(b, s), cb))

    finalize = fwd_out is not None
    in_specs = [spec(col0), spec(col0 + (2 if reverse else 1)), spec(col0 + 3),
                pl.BlockSpec((1, w), lambda b, s: (0, 0))]
    args = [p, p, p, lb.reshape(1, w)]
    if finalize:
        in_specs += [spec(0), spec(col0 + 4), pl.BlockSpec((1, HG_D), lambda b, s: (0, 0))]
        args += [fwd_out, p, gain.reshape(1, HG_D)]
    return pl.pallas_call(
        functools.partial(_hgrn_kernel, reverse=reverse, finalize=finalize),
        out_shape=jax.ShapeDtypeStruct((n, w), BF16 if finalize else F32),
        grid=(batch, n_lat + n_ctx),
        in_specs=in_specs,
        out_specs=spec(0),
        scratch_shapes=[pltpu.VMEM((HG_HEADS, HG_D, HG_D), F32)],
        compiler_params=_cparams(("parallel", "arbitrary")),
        name="hgrn_bwd" if reverse else "hgrn_fwd",
    )(*args)


def _sg_kernel(u_ref, v_ref, g_ref, w_ref, b_ref, o_ref):
    for r in range(0, u_ref.shape[0], SG_CHUNK):
        rs = slice(r, r + SG_CHUNK)
        for g in range(SG_GROUPS):
            sl = slice(g * SG_DIM, (g + 1) * SG_DIM)
            vn = _rms(jax.nn.gelu(v_ref[rs, sl]), g_ref[:, sl])
            mixed = _dot(w_ref[g], vn.astype(BF16)) + b_ref[:, sl]
            o_ref[rs, sl] = (jax.nn.gelu(u_ref[rs, sl]) * mixed).astype(BF16)


def spatial_gate(p, g, w, bias_full, *, col0, n_rows, tm):
    sw = SG_GROUPS * SG_DIM
    assert n_rows % tm == 0
    return pl.pallas_call(
        _sg_kernel,
        out_shape=jax.ShapeDtypeStruct((n_rows, sw), BF16),
        grid=(n_rows // tm,),
        in_specs=[
            pl.BlockSpec((tm, sw), lambda i: (i, col0)),
            pl.BlockSpec((tm, sw), lambda i: (i, col0 + 1)),
            pl.BlockSpec((1, sw), lambda i: (0, 0)),
            pl.BlockSpec((SG_GROUPS, SG_CHUNK, SG_CHUNK), lambda i: (0, 0, 0)),
            pl.BlockSpec((SG_CHUNK, sw), lambda i: (0, 0)),
        ],
        out_specs=pl.BlockSpec((tm, sw), lambda i: (i, 0)),
        compiler_params=_cparams(("parallel",)),
        name="spatial_gate",
    )(p, p, g.reshape(1, sw), w, bias_full)


def _out_kernel(a_ref, h_ref, s_ref, wa_ref, wh_ref, ws_ref, x_ref, gate_ref, o_ref):
    acc = _dot(a_ref[...], wa_ref[...]) + _dot(h_ref[...], wh_ref[...]) + _dot(s_ref[...], ws_ref[...])
    o_ref[...] = x_ref[...] + gate_ref[0] * acc


def out_proj(attn, hg, sg, w, x, mod, *, tile_mod, n_rows, tm, tn=2048):
    n, d = n_rows, x.shape[1]
    tn = min(tn, d)
    wa, wh, ws = attn.shape[1], hg.shape[1], sg.shape[1]
    assert wa % wh == 0 and wh == ws and n % tm == 0 and d % tn == 0
    return pl.pallas_call(
        _out_kernel,
        out_shape=jax.ShapeDtypeStruct((n, d), F32),
        grid=(n // tm, d // tn),
        in_specs=[
            pl.BlockSpec((tm, wa), lambda i, j: (i, 0)),
            pl.BlockSpec((tm, wh), lambda i, j: (i, 0)),
            pl.BlockSpec((tm, ws), lambda i, j: (i, 0)),
            pl.BlockSpec((wa, tn), lambda i, j: (0, j)),
            pl.BlockSpec((wh, tn), lambda i, j: (wa // wh, j)),
            pl.BlockSpec((ws, tn), lambda i, j: (wa // wh + 1, j)),
            pl.BlockSpec((tm, tn), lambda i, j: (i, j)),
            pl.BlockSpec((1, 1, tn), lambda i, j: (tile_mod(i) * N_MOD + 2, 0, j)),
        ],
        out_specs=pl.BlockSpec((tm, tn), lambda i, j: (i, j)),
        compiler_params=_cparams(("parallel", "parallel")),
        name="out_proj",
    )(attn, hg, sg, w, w, w, x, mod)


def _ffn_kernel(*refs, tile_seq, n_long, has_short, final, rows, chunk):
    (x_ref, xp_ref, xn_ref, g_ref, sh_ref, sc_ref, gate_ref,
     wg_ref, wv_ref, cwg_ref, cwv_ref, cbg_ref, cbv_ref, wd_ref) = refs[:14]
    gf_ref = refs[14] if final else None
    o_ref, h_ref, u_ref = refs[-3:]
    i, j = pl.program_id(0), pl.program_id(1)
    tm = x_ref.shape[0]
    halo = xp_ref.shape[0]

    def norm_mod(x):
        return _norm_mod(x, g_ref[...], sc_ref[0], sh_ref[0])

    @pl.when(j == 0)
    def _():
        for r in range(0, tm, rows):
            h_ref[r:r + rows, :] = norm_mod(x_ref[r:r + rows, :])
        h_ref[tm:tm + halo, :] = norm_mod(xp_ref[...])
        h_ref[tm + halo:tm + 2 * halo, :] = norm_mod(xn_ref[...])
        o_ref[...] = jnp.zeros_like(o_ref)

    chunks = [slice(c0, c0 + chunk) for c0 in range(0, wd_ref.shape[0], chunk)]
    rid = lax.broadcasted_iota(jnp.int32, (tm, 1), 0)

    def step(interior):
        seq = tile_seq(i)
        tile_starts = ((i * tm) & (seq - 1)) == 0
        tile_ends = (((i + 1) * tm) & (seq - 1)) == 0
        if interior:
            pos = (i * tm + rid) & (seq - 1)
            first, last = pos == 0, pos == seq - 1

        for c, cs in enumerate(chunks):
            u_ref[c, 0] = _dot(h_ref[...], wg_ref[:, cs])
            u_ref[c, 1] = _dot(h_ref[...], wv_ref[:, cs])

        def conv(c, k, cw_ref, cb_ref, cs):
            u = u_ref[c, k, 0:tm, :]
            prev_row = jnp.where(tile_starts, 0.0, u_ref[c, k, tm + halo - 1:tm + halo, :])
            next_row = jnp.where(tile_ends, 0.0, u_ref[c, k, tm + halo:tm + halo + 1, :])
            before = jnp.where(rid == 0, prev_row, pltpu.roll(u, 1, 0))
            after = jnp.where(rid == tm - 1, next_row, pltpu.roll(u, tm - 1, 0))
            if interior:
                before = jnp.where(first, 0.0, before)
                after = jnp.where(last, 0.0, after)
            return (cb_ref[:, cs] + before * cw_ref[0:1, cs] + u * cw_ref[1:2, cs]
                    + after * cw_ref[2:3, cs])

        for c, cs in enumerate(chunks):
            act = _silu(conv(c, 0, cwg_ref, cbg_ref, cs)) * conv(c, 1, cwv_ref, cbv_ref, cs)
            o_ref[...] += _dot(act.astype(BF16), wd_ref[cs, :])

    if has_short:
        pl.when(i < n_long)(functools.partial(step, False))
        pl.when(i >= n_long)(functools.partial(step, True))
    else:
        step(False)

    @pl.when(j == pl.num_programs(1) - 1)
    def _():
        def finish(r, carry):
            rs = pl.ds(pl.multiple_of(r * rows, rows), rows)
            y = x_ref[rs, :] + gate_ref[0] * o_ref[rs, :]
            o_ref[rs, :] = _rms(y, gf_ref[...]) if final else y
            return carry

        lax.fori_loop(0, tm // rows, finish, 0)


def conv_ffn(x, g, mod, w_up, conv_w, conv_b, w_down, *, tile_mod, t_lat, t_ctx, n_lat,
             n_rows, tm, tf=512, final_gain=None):
    n, d = x.shape
    dff = w_down.shape[0]
    assert n_rows % tm == 0 and n_lat % tm == 0 and t_lat % tm == 0 and dff % tf == 0
    assert n_rows == n_lat or tm % t_ctx == 0 or t_ctx % tm == 0
    nj = dff // tf
    halo = SUBLANES
    chunk = min(MXU_WIDTH, tf)
    hb = tm // halo
    nhb = n // halo
    n_long = n_lat // tm
    cb = conv_b.reshape(1, 2 * dff)
    final = final_gain is not None
    in_specs = [
        pl.BlockSpec((tm, d), lambda i, j: (i, 0)),
        pl.BlockSpec((halo, d), lambda i, j: (jnp.maximum(i * hb - 1, 0), 0)),
        pl.BlockSpec((halo, d), lambda i, j: (jnp.minimum((i + 1) * hb, nhb - 1), 0)),
        pl.BlockSpec((1, d), lambda i, j: (0, 0)),
        pl.BlockSpec((1, 1, d), lambda i, j: (tile_mod(i) * N_MOD + 3, 0, 0)),
        pl.BlockSpec((1, 1, d), lambda i, j: (tile_mod(i) * N_MOD + 4, 0, 0)),
        pl.BlockSpec((1, 1, d), lambda i, j: (tile_mod(i) * N_MOD + 5, 0, 0)),
        pl.BlockSpec((d, tf), lambda i, j: (0, j)),
        pl.BlockSpec((d, tf), lambda i, j: (0, nj + j)),
        pl.BlockSpec((CONV_W, tf), lambda i, j: (0, j)),
        pl.BlockSpec((CONV_W, tf), lambda i, j: (0, nj + j)),
        pl.BlockSpec((1, tf), lambda i, j: (0, j)),
        pl.BlockSpec((1, tf), lambda i, j: (0, nj + j)),
        pl.BlockSpec((tf, d), lambda i, j: (j, 0)),
    ]
    args = [x, x, x, g.reshape(1, d), mod, mod, mod, w_up, w_up, conv_w, conv_w, cb, cb, w_down]
    if final:
        in_specs.append(pl.BlockSpec((1, d), lambda i, j: (0, 0)))
        args.append(final_gain.reshape(1, d))
    return pl.pallas_call(
        functools.partial(
            _ffn_kernel, tile_seq=lambda i: jnp.where(i < n_long, t_lat, t_ctx), n_long=n_long,
            has_short=n_rows > n_lat and t_ctx < tm, final=final, rows=min(tm, 256), chunk=chunk),
        out_shape=jax.ShapeDtypeStruct((n_rows, d), F32),
        grid=(n_rows // tm, nj),
        in_specs=in_specs,
        out_specs=pl.BlockSpec((tm, d), lambda i, j: (i, 0)),
        scratch_shapes=[pltpu.VMEM((tm + 2 * halo, d), BF16),
                        pltpu.VMEM((tf // chunk, 2, tm + 2 * halo, chunk), F32)],
        compiler_params=_cparams(("parallel", "arbitrary")),
        name="conv_ffn",
    )(*args)


def _rope_tables(t_lat, tm):
    rows = t_lat // GRID_W
    row = jnp.repeat(jnp.arange(rows, dtype=F32), GRID_W)
    col = jnp.tile(jnp.arange(GRID_W, dtype=F32), rows)
    n_freq = HEAD_DIM // 4
    inv = ROPE_THETA ** (-jnp.arange(n_freq, dtype=F32) / n_freq)
    ang = jnp.concatenate([row[:, None] * inv, col[:, None] * inv], axis=-1)
    cos2 = jnp.repeat(jnp.cos(ang), 2, axis=-1)
    sin2 = jnp.repeat(jnp.sin(ang), 2, axis=-1) * jnp.tile(jnp.array([-1.0, 1.0], F32), HEAD_DIM // 2)
    cos2 = jnp.concatenate([cos2, jnp.ones((tm, HEAD_DIM), F32)], axis=0)
    sin2 = jnp.concatenate([sin2, jnp.zeros((tm, HEAD_DIM), F32)], axis=0)
    return cos2, sin2


def _lower_bounds(lb_param):
    p = jax.nn.softmax(lb_param.astype(F32), axis=1)
    return jnp.cumsum(p, axis=1) - p[:, :1]


def kernel(x, c, ctx, c_ctx, w_ada, b_ada, norm1_g, w_in, q_norm_g, k_norm_g, hg_lower_bounds,
           hg_norm_g, sg_norm_g, sg_w, sg_b, w_out, norm2_g, w_up, conv_w, conv_b, w_down,
           final_norm_g):
    batch, t_lat, d = x.shape
    t_ctx = ctx.shape[1]
    depth = w_in.shape[0]
    n_lat, n_ctx = batch * t_lat, batch * t_ctx
    assert t_lat & (t_lat - 1) == 0 and t_ctx & (t_ctx - 1) == 0

    tm = min(1024, n_ctx)
    assert t_lat % tm == 0 and n_ctx % tm == 0

    def tile_mod(i):
        return jnp.minimum(i // (t_lat // tm), batch)

    def tile_rope(i):
        return jnp.where(i < n_lat // tm, i % (t_lat // tm), t_lat // tm)

    xs = jnp.concatenate([x.reshape(n_lat, d), ctx.reshape(n_ctx, d)], axis=0)
    cin = jnp.concatenate([c, c_ctx[None, :], jnp.zeros((8 - batch - 1, d), F32)], axis=0)
    mods = ada_table(cin, w_ada, b_ada)[:, :batch + 1, :]
    mods = mods.reshape(depth, (batch + 1) * N_MOD, 1, d)
    cos2, sin2 = _rope_tables(t_lat, tm)
    lbs = _lower_bounds(hg_lower_bounds)
    qw = ATTN_HEADS * HEAD_DIM
    kvw = ATTN_KV_HEADS * HEAD_DIM
    hg_col0 = (qw + 2 * kvw) // (HG_HEADS * HG_D)
    sg_col0 = hg_col0 + 5

    w_in_l = w_in[0].astype(BF16)

    for l in range(depth):
        mod = mods[l]
        last = l == depth - 1
        n_rows = n_lat if last else n_lat + n_ctx
        p = in_proj(xs, norm1_g[l], mod, w_in_l, tile_mod=tile_mod, tm=tm)
        q, k, vt = qk_prep(p, cos2, sin2, q_norm_g[l], k_norm_g[l], tile_rope=tile_rope, tm=tm)
        casts = [(w_out, l), (w_up, l), (w_down, l)] + ([] if last else [(w_in, l + 1)])
        attn, w_out_l, w_up_l, w_down_l, *nxt = attention(
            q, k, vt, casts, batch=batch, t_lat=t_lat, t_ctx=t_ctx, ctx_queries=not last)
        w_in_l = nxt[0] if nxt else None
        o_f = hgrn_scan(p, lbs[0, l], batch=batch, t_lat=t_lat, t_ctx=t_ctx, col0=hg_col0,
                        reverse=False)
        hg = hgrn_scan(p, lbs[1, l], batch=batch, t_lat=t_lat, t_ctx=t_ctx, col0=hg_col0,
                       reverse=True, fwd_out=o_f, gain=hg_norm_g[l])
        bias_full = jnp.repeat(sg_b[l].T, SG_DIM, axis=1)
        sg = spatial_gate(p, sg_norm_g[l], sg_w[l].astype(BF16), bias_full, col0=sg_col0,
                          n_rows=n_rows, tm=tm)
        xs = out_proj(attn, hg, sg, w_out_l, xs, mod, tile_mod=tile_mod, n_rows=n_rows, tm=tm)
        xs = conv_ffn(xs, norm2_g[l], mod, w_up_l, conv_w[l], conv_b[l], w_down_l,
                      tile_mod=tile_mod, t_lat=t_lat, t_ctx=t_ctx, n_lat=n_lat, n_rows=n_rows,
                      tm=tm, final_gain=final_norm_g if last else None)

    return xs.reshape(batch, t_lat, d)
```

```python
import functools

import jax
import jax.numpy as jnp
from jax import lax
from jax.experimental import pallas as pl
from jax.experimental.pallas import tpu as pltpu

F32 = jnp.float32
BF16 = jnp.bfloat16

EPS = 1e-6
F_MIN = 1e-30
N_MOD = 6
HEAD_DIM = 128
ATTN_HEADS = 8
ATTN_KV_HEADS = 2
ATTN_GROUP = ATTN_HEADS // ATTN_KV_HEADS
ROPE_THETA = 10000.0
GRID_W = 64
HG_HEADS = 4
HG_D = 128
SG_GROUPS = 4
SG_DIM = 128
SG_CHUNK = 128
CONV_W = 3
LANE = 128
SUBLANES = 8
LOG2_E = 1.4426950408889634
HG_CHUNK = 128
BF16_ROWS = 2 * SUBLANES
MXU_WIDTH = 256
V7X_VMEM_BYTES = 64 * 1024 * 1024
VMEM_LIMIT = V7X_VMEM_BYTES - 2 * 1024 * 1024


def _cparams(sem):
    return pltpu.CompilerParams(dimension_semantics=sem, vmem_limit_bytes=VMEM_LIMIT)


def _dot(a, b):
    return jnp.dot(a, b, preferred_element_type=F32)


def _dot_nt(a, b):
    return lax.dot_general(a, b, (((1,), (1,)), ((), ())), preferred_element_type=F32)


def _rms(x, g):
    return x * lax.rsqrt(jnp.mean(x * x, axis=-1, keepdims=True) + EPS) * g


def _silu(x):
    return x * jax.nn.sigmoid(x)


def _ada_kernel(c_ref, w_ref, b_ref, o_ref):
    s = _silu(c_ref[...]).astype(BF16)
    o_ref[...] = _dot(s, w_ref[...].astype(BF16)) + b_ref[...]


def ada_table(cin, w_ada, b_ada, tn=1024):
    depth, d, n = w_ada.shape
    assert n % tn == 0
    return pl.pallas_call(
        _ada_kernel,
        out_shape=jax.ShapeDtypeStruct((depth, 8, n), F32),
        grid=(depth, n // tn),
        in_specs=[
            pl.BlockSpec((8, d), lambda l, j: (0, 0)),
            pl.BlockSpec((None, d, tn), lambda l, j: (l, 0, j)),
            pl.BlockSpec((None, 1, tn), lambda l, j: (l, 0, j)),
        ],
        out_specs=pl.BlockSpec((None, 8, tn), lambda l, j: (l, 0, j)),
        compiler_params=_cparams(("parallel", "parallel")),
        name="ada_table",
    )(cin, w_ada, b_ada.reshape(depth, 1, n))


def _norm_mod(x, g, scale, shift):
    r = lax.rsqrt(jnp.mean(x * x, axis=-1, keepdims=True) + EPS)
    return ((x * r) * (g * (1.0 + scale)) + shift).astype(BF16)


def _in_kernel(x_ref, g_ref, sh_ref, sc_ref, w_ref, o_ref, h_ref, *, rows):
    @pl.when(pl.program_id(1) == 0)
    def _():
        for r in range(0, x_ref.shape[0], rows):
            h_ref[r:r + rows, :] = _norm_mod(x_ref[r:r + rows, :], g_ref[...], sc_ref[0], sh_ref[0])

    o_ref[...] = _dot(h_ref[...], w_ref[...])


def in_proj(x, g, mod, w, *, tile_mod, tm, tn=1280):
    n, d = x.shape
    cols = w.shape[1]
    assert n % tm == 0 and cols % tn == 0
    return pl.pallas_call(
        functools.partial(_in_kernel, rows=min(tm, 256)),
        out_shape=jax.ShapeDtypeStruct((n, cols), F32),
        grid=(n // tm, cols // tn),
        in_specs=[
            pl.BlockSpec((tm, d), lambda i, j: (i, 0)),
            pl.BlockSpec((1, d), lambda i, j: (0, 0)),
            pl.BlockSpec((1, 1, d), lambda i, j: (tile_mod(i) * N_MOD + 0, 0, 0)),
            pl.BlockSpec((1, 1, d), lambda i, j: (tile_mod(i) * N_MOD + 1, 0, 0)),
            pl.BlockSpec((d, tn), lambda i, j: (0, j)),
        ],
        out_specs=pl.BlockSpec((tm, tn), lambda i, j: (i, j)),
        scratch_shapes=[pltpu.VMEM((tm, d), BF16)],
        compiler_params=_cparams(("parallel", "arbitrary")),
        name="in_proj",
    )(x, g.reshape(1, d), mod, mod, w)


def _rope(y, cos2, sin2):
    lane = lax.broadcasted_iota(jnp.int32, y.shape, 1)
    swapped = jnp.where((lane & 1) == 0, pltpu.roll(y, LANE - 1, 1), pltpu.roll(y, 1, 1))
    return y * cos2 + swapped * sin2


def _qk_kernel(q_ref, kv_ref, cos_ref, sin_ref, qg_ref, kg_ref, qo_ref, ko_ref, vo_ref):
    cos2, sin2 = cos_ref[...], sin_ref[...]
    scale = HEAD_DIM ** -0.5 * LOG2_E
    for h in range(ATTN_HEADS):
        sl = slice(h * HEAD_DIM, (h + 1) * HEAD_DIM)
        y = _rms(q_ref[:, sl], qg_ref[...])
        qo_ref[:, sl] = (_rope(y, cos2, sin2) * scale).astype(BF16)
    for h in range(ATTN_KV_HEADS):
        sl = slice(h * HEAD_DIM, (h + 1) * HEAD_DIM)
        y = _rms(kv_ref[:, sl], kg_ref[...])
        ko_ref[:, sl] = _rope(y, cos2, sin2).astype(BF16)
    kvw = ATTN_KV_HEADS * HEAD_DIM
    vo_ref[...] = kv_ref[:, kvw:2 * kvw].T.astype(BF16)


def qk_prep(p, cos2, sin2, qg, kg, *, tile_rope, tm):
    n = p.shape[0]
    qw = ATTN_HEADS * HEAD_DIM
    kvw = ATTN_KV_HEADS * HEAD_DIM
    return pl.pallas_call(
        _qk_kernel,
        out_shape=(jax.ShapeDtypeStruct((n, qw), BF16),
                   jax.ShapeDtypeStruct((n, kvw), BF16),
                   jax.ShapeDtypeStruct((kvw, n), BF16)),
        grid=(n // tm,),
        in_specs=[
            pl.BlockSpec((tm, qw), lambda i: (i, 0)),
            pl.BlockSpec((tm, 2 * kvw), lambda i: (i, qw // (2 * kvw))),
            pl.BlockSpec((tm, HEAD_DIM), lambda i: (tile_rope(i), 0)),
            pl.BlockSpec((tm, HEAD_DIM), lambda i: (tile_rope(i), 0)),
            pl.BlockSpec((1, HEAD_DIM), lambda i: (0, 0)),
            pl.BlockSpec((1, HEAD_DIM), lambda i: (0, 0)),
        ],
        out_specs=(pl.BlockSpec((tm, qw), lambda i: (i, 0)),
                   pl.BlockSpec((tm, kvw), lambda i: (i, 0)),
                   pl.BlockSpec((kvw, tm), lambda i: (0, i))),
        compiler_params=_cparams(("parallel",)),
        name="qk_prep",
    )(p, p, cos2, sin2, qg.reshape(1, HEAD_DIM), kg.reshape(1, HEAD_DIM))


def _softmax_pv(q, segs, ck, s_ref):
    mq = q.shape[0]
    chunks = [(k_ref, vt_ref, c0, min(ck, k_ref.shape[0]))
              for k_ref, vt_ref in segs for c0 in range(0, k_ref.shape[0], min(ck, k_ref.shape[0]))]

    def scores(i):
        k_ref, _, c0, cs = chunks[i]
        s_ref[i % 2, 0:cs, :] = _dot_nt(k_ref[c0:c0 + cs, :], q)

    scores(0)
    m = l = acc = None
    for i, (_, vt_ref, c0, cs) in enumerate(chunks):
        if i + 1 < len(chunks):
            scores(i + 1)
        s = s_ref[i % 2, 0:cs, :]
        m_c = s.reshape(cs // SUBLANES, SUBLANES, mq).max(axis=0).max(axis=0, keepdims=True)
        m_new = m_c if m is None else jnp.maximum(m, m_c)
        p = jnp.exp2(s - m_new)
        p_rows = p.reshape(cs // SUBLANES, SUBLANES, mq).sum(axis=0)
        pv = _dot(vt_ref[:, c0:c0 + cs], p.astype(BF16))
        if m is None:
            l, acc = p_rows, pv
        else:
            alpha = jnp.exp2(m - m_new)
            l, acc = alpha * l + p_rows, alpha * acc + pv
        m = m_new
    return (acc / l.sum(axis=0, keepdims=True)).T


def _attn_kernel(q_ref, kl_ref, kc_ref, vl_ref, vc_ref, o_ref, s_ref, *, n_lat, ck):
    qi = pl.program_id(2)
    tq = q_ref.shape[0]

    def run(segs):
        hs = range(ATTN_GROUP)
        q = jnp.concatenate([q_ref[:, h * HEAD_DIM:(h + 1) * HEAD_DIM] for h in hs], axis=0)
        o = _softmax_pv(q, segs, ck, s_ref)
        for h in hs:
            o_ref[:, h * HEAD_DIM:(h + 1) * HEAD_DIM] = o[h * tq:(h + 1) * tq].astype(BF16)

    @pl.when(qi < n_lat)
    def _():
        run([(kl_ref, vl_ref), (kc_ref, vc_ref)])

    @pl.when(qi >= n_lat)
    def _():
        run([(kc_ref, vc_ref)])


def attention(q, k, vt, *, batch, t_lat, t_ctx, ctx_queries=True, tq=256, ck=512):
    n = q.shape[0]
    n_lat = t_lat // tq
    n_ctx = t_ctx // tq if ctx_queries else 0
    gw = ATTN_GROUP * HEAD_DIM
    ctx0 = batch * t_lat

    def qmap(b, j, i):
        return (jnp.where(i < n_lat, b * n_lat + i, ctx0 // tq + b * (t_ctx // tq) + (i - n_lat)), j)

    return pl.pallas_call(
        functools.partial(_attn_kernel, n_lat=n_lat, ck=ck),
        out_shape=jax.ShapeDtypeStruct((n if ctx_queries else ctx0, ATTN_HEADS * HEAD_DIM), BF16),
        grid=(batch, ATTN_KV_HEADS, n_lat + n_ctx),
        in_specs=[
            pl.BlockSpec((tq, gw), qmap),
            pl.BlockSpec((t_lat, HEAD_DIM), lambda b, j, i: (b, j)),
            pl.BlockSpec((t_ctx, HEAD_DIM), lambda b, j, i: (ctx0 // t_ctx + b, j)),
            pl.BlockSpec((HEAD_DIM, t_lat), lambda b, j, i: (j, b)),
            pl.BlockSpec((HEAD_DIM, t_ctx), lambda b, j, i: (j, ctx0 // t_ctx + b)),
        ],
        out_specs=pl.BlockSpec((tq, gw), qmap),
        scratch_shapes=[pltpu.VMEM((2, ck, ATTN_GROUP * tq), F32)],
        compiler_params=_cparams(("parallel", "parallel", "arbitrary")),
        name="attention",
    )(q, k, k, vt, vt)


def _hgrn_chunk(hq, z, v, lb, st, *, reverse):
    c = hq.shape[0]
    sig = jax.nn.sigmoid(z)
    f = lb + (1.0 - lb) * sig
    logf = jnp.log(jnp.maximum(f, F_MIN)) * (1.0 / jnp.log(2.0))
    kk = (1.0 - lb) * (1.0 - sig)
    q = _silu(hq)

    row = lax.broadcasted_iota(jnp.int32, (c, c), 0)
    col = lax.broadcasted_iota(jnp.int32, (c, c), 1)
    rid = lax.broadcasted_iota(jnp.int32, (c, HG_D), 0)
    later = (row < col) if reverse else (row > col)

    a = jnp.where(row == col, _dot_nt(q.astype(BF16), kk.astype(BF16)), 0.0)
    cum, tot = logf, logf
    d, ld = 1, 0
    while d < c:
        mask = jnp.logical_and(((row ^ col) >> ld) == 1, later)
        if d < SUBLANES:
            second = (rid & d) != 0
            sel = jnp.logical_not(second) if reverse else second
            x = jnp.where(sel, q, kk) * jnp.exp2(jnp.where(sel, cum, tot - cum))
            t3 = tot.reshape(c // SUBLANES, SUBLANES, HG_D)
            up = pltpu.roll(t3, d, 1).reshape(c, HG_D)
            partner = up if 2 * d == SUBLANES else jnp.where(
                second, up, pltpu.roll(t3, SUBLANES - d, 1).reshape(c, HG_D))
            cum = cum + jnp.where(sel, partner, 0.0)
            tot = tot + partner
        else:
            def split(y):
                y4 = y.reshape(c // (2 * d), 2, d, HG_D)
                return (y4[:, 1], y4[:, 0]) if reverse else (y4[:, 0], y4[:, 1])

            def join(early, late):
                pair = [late, early] if reverse else [early, late]
                return jnp.stack(pair, axis=1).reshape(c, HG_D)

            (cum_e, cum_l), (tot_e, tot_l) = split(cum), split(tot)
            x = join(split(kk)[0] * jnp.exp2(tot_e - cum_e), split(q)[1] * jnp.exp2(cum_l))
            both = tot_e + tot_l
            cum, tot = join(cum_e, cum_l + tot_e), join(both, both)
        xb = x.astype(BF16)
        a = jnp.where(mask, _dot_nt(xb, xb), a)
        d, ld = d * 2, ld + 1

    qd = (q * jnp.exp2(cum)).astype(BF16)
    kd = (kk * jnp.exp2(tot - cum)).astype(BF16)
    vb = v.astype(BF16)
    o = _dot(a.astype(BF16), vb) + _dot_nt(qd, st.astype(BF16))
    st_new = st * jnp.exp2(tot[0:1, :]) + _dot(vb.T, kd)
    return o, st_new


def _hgrn_kernel(*refs, reverse, finalize, n_cast):
    n_in = 7 if finalize else 4
    hq_ref, hf_ref, hi_ref, lb_ref = refs[:4]
    if finalize:
        of_ref, gt_ref, g_ref = refs[4:7]
    o_ref, st_ref = refs[n_in + n_cast], refs[-1]

    @pl.when(pl.program_id(1) == 0)
    def _():
        st_ref[...] = jnp.zeros_like(st_ref)

    for src, dst in zip(refs[n_in:n_in + n_cast], refs[n_in + n_cast + 1:-1]):
        dst[...] = src[...].astype(dst.dtype)

    starts = range(0, hq_ref.shape[0], HG_CHUNK)
    for r in (reversed(starts) if reverse else starts):
        rs = slice(r, r + HG_CHUNK)
        for h in range(HG_HEADS):
            sl = slice(h * HG_D, (h + 1) * HG_D)
            o, st_new = _hgrn_chunk(hq_ref[rs, sl], hf_ref[rs, sl], hi_ref[rs, sl], lb_ref[:, sl],
                                    st_ref[h], reverse=reverse)
            st_ref[h] = st_new
            if finalize:
                y = _rms(o + of_ref[rs, sl], g_ref[...])
                o_ref[rs, sl] = (y * _silu(gt_ref[rs, sl])).astype(o_ref.dtype)
            else:
                o_ref[rs, sl] = o


def _cast_rows(total, steps):
    return next(r for r in range(BF16_ROWS, total + 1, BF16_ROWS)
                if total % r == 0 and total // r <= steps)


def _side_casts(weights, steps, flat_step):
    in_specs, out_specs, shapes = [], [], []
    for w, layer in weights:
        rows = _cast_rows(w.shape[1], steps)
        last = w.shape[1] // rows - 1
        in_specs.append(pl.BlockSpec(
            (None, rows, w.shape[2]),
            lambda *ids, layer=layer, last=last: (layer, jnp.minimum(flat_step(*ids), last), 0)))
        out_specs.append(pl.BlockSpec(
            (rows, w.shape[2]), lambda *ids, last=last: (jnp.minimum(flat_step(*ids), last), 0)))
        shapes.append(jax.ShapeDtypeStruct(w.shape[1:], BF16))
    return in_specs, out_specs, shapes


def hgrn_scan(p, lb, weights, *, batch, t_lat, t_ctx, col0, reverse, fwd_out=None, gain=None,
              chunks_per_step=2):
    n = p.shape[0]
    c = HG_CHUNK * chunks_per_step
    assert t_lat % c == 0 and t_ctx % c == 0
    w = HG_HEADS * HG_D
    n_lat, n_ctx = t_lat // c, t_ctx // c
    ctx0 = batch * n_lat

    def rows(b, s):
        if reverse:
            return jnp.where(s < n_ctx, ctx0 + b * n_ctx + (n_ctx - 1 - s),
                             b * n_lat + (n_lat - 1 - (s - n_ctx)))
        return jnp.where(s < n_ctx, ctx0 + b * n_ctx + s, b * n_lat + (s - n_ctx))

    def spec(cb):
        return pl.BlockSpec((c, w), lambda b, s: (rows(b, s), cb))

    finalize = fwd_out is not None
    in_specs = [spec(col0), spec(col0 + (2 if reverse else 1)), spec(col0 + 3),
                pl.BlockSpec((1, w), lambda b, s: (0, 0))]
    args = [p, p, p, lb.reshape(1, w)]
    if finalize:
        in_specs += [spec(0), spec(col0 + 4), pl.BlockSpec((1, HG_D), lambda b, s: (0, 0))]
        args += [fwd_out, p, gain.reshape(1, HG_D)]
    n_steps = n_lat + n_ctx
    c_in, c_out, c_shapes = _side_casts(weights, batch * n_steps, lambda b, s: b * n_steps + s)
    return pl.pallas_call(
        functools.partial(_hgrn_kernel, reverse=reverse, finalize=finalize, n_cast=len(weights)),
        out_shape=(jax.ShapeDtypeStruct((n, w), BF16 if finalize else F32), *c_shapes),
        grid=(batch, n_steps),
        in_specs=in_specs + c_in,
        out_specs=(spec(0), *c_out),
        scratch_shapes=[pltpu.VMEM((HG_HEADS, HG_D, HG_D), F32)],
        compiler_params=_cparams(("arbitrary", "arbitrary")),
        name="hgrn_bwd" if reverse else "hgrn_fwd",
    )(*args, *[w for w, _ in weights])


def _sg_kernel(u_ref, v_ref, g_ref, w_ref, b_ref, o_ref):
    for r in range(0, u_ref.shape[0], SG_CHUNK):
        rs = slice(r, r + SG_CHUNK)
        for g in range(SG_GROUPS):
            sl = slice(g * SG_DIM, (g + 1) * SG_DIM)
            vn = _rms(jax.nn.gelu(v_ref[rs, sl]), g_ref[:, sl])
            mixed = _dot(w_ref[g], vn.astype(BF16)) + b_ref[:, sl]
            o_ref[rs, sl] = (jax.nn.gelu(u_ref[rs, sl]) * mixed).astype(BF16)


def spatial_gate(p, g, w, bias_full, *, col0, n_rows, tm):
    sw = SG_GROUPS * SG_DIM
    assert n_rows % tm == 0
    return pl.pallas_call(
        _sg_kernel,
        out_shape=jax.ShapeDtypeStruct((n_rows, sw), BF16),
        grid=(n_rows // tm,),
        in_specs=[
            pl.BlockSpec((tm, sw), lambda i: (i, col0)),
            pl.BlockSpec((tm, sw), lambda i: (i, col0 + 1)),
            pl.BlockSpec((1, sw), lambda i: (0, 0)),
            pl.BlockSpec((SG_GROUPS, SG_CHUNK, SG_CHUNK), lambda i: (0, 0, 0)),
            pl.BlockSpec((SG_CHUNK, sw), lambda i: (0, 0)),
        ],
        out_specs=pl.BlockSpec((tm, sw), lambda i: (i, 0)),
        compiler_params=_cparams(("parallel",)),
        name="spatial_gate",
    )(p, p, g.reshape(1, sw), w, bias_full)


def _out_kernel(a_ref, h_ref, s_ref, wa_ref, wh_ref, ws_ref, x_ref, gate_ref, o_ref):
    acc = _dot(a_ref[...], wa_ref[...]) + _dot(h_ref[...], wh_ref[...]) + _dot(s_ref[...], ws_ref[...])
    o_ref[...] = x_ref[...] + gate_ref[0] * acc


def out_proj(attn, hg, sg, w, x, mod, *, tile_mod, n_rows, tm, tn=2048):
    n, d = n_rows, x.shape[1]
    tn = min(tn, d)
    wa, wh, ws = attn.shape[1], hg.shape[1], sg.shape[1]
    assert wa % wh == 0 and wh == ws and n % tm == 0 and d % tn == 0
    return pl.pallas_call(
        _out_kernel,
        out_shape=jax.ShapeDtypeStruct((n, d), F32),
        grid=(n // tm, d // tn),
        in_specs=[
            pl.BlockSpec((tm, wa), lambda i, j: (i, 0)),
            pl.BlockSpec((tm, wh), lambda i, j: (i, 0)),
            pl.BlockSpec((tm, ws), lambda i, j: (i, 0)),
            pl.BlockSpec((wa, tn), lambda i, j: (0, j)),
            pl.BlockSpec((wh, tn), lambda i, j: (wa // wh, j)),
            pl.BlockSpec((ws, tn), lambda i, j: (wa // wh + 1, j)),
            pl.BlockSpec((tm, tn), lambda i, j: (i, j)),
            pl.BlockSpec((1, 1, tn), lambda i, j: (tile_mod(i) * N_MOD + 2, 0, j)),
        ],
        out_specs=pl.BlockSpec((tm, tn), lambda i, j: (i, j)),
        compiler_params=_cparams(("parallel", "parallel")),
        name="out_proj",
    )(attn, hg, sg, w, w, w, x, mod)


def _ffn_kernel(*refs, tile_seq, n_long, has_short, final, rows, chunk):
    (x_ref, xp_ref, xn_ref, g_ref, sh_ref, sc_ref, gate_ref,
     wg_ref, wv_ref, cwg_ref, cwv_ref, cbg_ref, cbv_ref, wd_ref) = refs[:14]
    gf_ref = refs[14] if final else None
    o_ref, h_ref, u_ref = refs[-3:]
    i, j = pl.program_id(0), pl.program_id(1)
    tm = x_ref.shape[0]
    halo = xp_ref.shape[0]

    def norm_mod(x):
        return _norm_mod(x, g_ref[...], sc_ref[0], sh_ref[0])

    @pl.when(j == 0)
    def _():
        for r in range(0, tm, rows):
            h_ref[r:r + rows, :] = norm_mod(x_ref[r:r + rows, :])
        h_ref[tm:tm + halo, :] = norm_mod(xp_ref[...])
        h_ref[tm + halo:tm + 2 * halo, :] = norm_mod(xn_ref[...])
        o_ref[...] = jnp.zeros_like(o_ref)

    chunks = [slice(c0, c0 + chunk) for c0 in range(0, wd_ref.shape[0], chunk)]
    rid = lax.broadcasted_iota(jnp.int32, (tm, 1), 0)

    def step(interior):
        seq = tile_seq(i)
        tile_starts = ((i * tm) & (seq - 1)) == 0
        tile_ends = (((i + 1) * tm) & (seq - 1)) == 0
        if interior:
            pos = (i * tm + rid) & (seq - 1)
            first, last = pos == 0, pos == seq - 1

        for c, cs in enumerate(chunks):
            u_ref[c, 0] = _dot(h_ref[...], wg_ref[:, cs])
            u_ref[c, 1] = _dot(h_ref[...], wv_ref[:, cs])

        def conv(c, k, cw_ref, cb_ref, cs):
            u = u_ref[c, k, 0:tm, :]
            prev_row = jnp.where(tile_starts, 0.0, u_ref[c, k, tm + halo - 1:tm + halo, :])
            next_row = jnp.where(tile_ends, 0.0, u_ref[c, k, tm + halo:tm + halo + 1, :])
            before = jnp.where(rid == 0, prev_row, pltpu.roll(u, 1, 0))
            after = jnp.where(rid == tm - 1, next_row, pltpu.roll(u, tm - 1, 0))
            if interior:
                before = jnp.where(first, 0.0, before)
                after = jnp.where(last, 0.0, after)
            return (cb_ref[:, cs] + before * cw_ref[0:1, cs] + u * cw_ref[1:2, cs]
                    + after * cw_ref[2:3, cs])

        for c, cs in enumerate(chunks):
            act = _silu(conv(c, 0, cwg_ref, cbg_ref, cs)) * conv(c, 1, cwv_ref, cbv_ref, cs)
            o_ref[...] += _dot(act.astype(BF16), wd_ref[cs, :])

    if has_short:
        pl.when(i < n_long)(functools.partial(step, False))
        pl.when(i >= n_long)(functools.partial(step, True))
    else:
        step(False)

    @pl.when(j == pl.num_programs(1) - 1)
    def _():
        def finish(r, carry):
            rs = pl.ds(pl.multiple_of(r * rows, rows), rows)
            y = x_ref[rs, :] + gate_ref[0] * o_ref[rs, :]
            o_ref[rs, :] = _rms(y, gf_ref[...]) if final else y
            return carry

        lax.fori_loop(0, tm // rows, finish, 0)


def conv_ffn(x, g, mod, w_up, conv_w, conv_b, w_down, *, tile_mod, t_lat, t_ctx, n_lat,
             n_rows, tm, tf=512, final_gain=None):
    n, d = x.shape
    dff = w_down.shape[0]
    assert n_rows % tm == 0 and n_lat % tm == 0 and t_lat % tm == 0 and dff % tf == 0
    assert n_rows == n_lat or tm % t_ctx == 0 or t_ctx % tm == 0
    nj = dff // tf
    halo = SUBLANES
    chunk = min(MXU_WIDTH, tf)
    hb = tm // halo
    nhb = n // halo
    n_long = n_lat // tm
    cb = conv_b.reshape(1, 2 * dff)
    final = final_gain is not None
    in_specs = [
        pl.BlockSpec((tm, d), lambda i, j: (i, 0)),
        pl.BlockSpec((halo, d), lambda i, j: (jnp.maximum(i * hb - 1, 0), 0)),
        pl.BlockSpec((halo, d), lambda i, j: (jnp.minimum((i + 1) * hb, nhb - 1), 0)),
        pl.BlockSpec((1, d), lambda i, j: (0, 0)),
        pl.BlockSpec((1, 1, d), lambda i, j: (tile_mod(i) * N_MOD + 3, 0, 0)),
        pl.BlockSpec((1, 1, d), lambda i, j: (tile_mod(i) * N_MOD + 4, 0, 0)),
        pl.BlockSpec((1, 1, d), lambda i, j: (tile_mod(i) * N_MOD + 5, 0, 0)),
        pl.BlockSpec((d, tf), lambda i, j: (0, j)),
        pl.BlockSpec((d, tf), lambda i, j: (0, nj + j)),
        pl.BlockSpec((CONV_W, tf), lambda i, j: (0, j)),
        pl.BlockSpec((CONV_W, tf), lambda i, j: (0, nj + j)),
        pl.BlockSpec((1, tf), lambda i, j: (0, j)),
        pl.BlockSpec((1, tf), lambda i, j: (0, nj + j)),
        pl.BlockSpec((tf, d), lambda i, j: (j, 0)),
    ]
    args = [x, x, x, g.reshape(1, d), mod, mod, mod, w_up, w_up, conv_w, conv_w, cb, cb, w_down]
    if final:
        in_specs.append(pl.BlockSpec((1, d), lambda i, j: (0, 0)))
        args.append(final_gain.reshape(1, d))
    return pl.pallas_call(
        functools.partial(
            _ffn_kernel, tile_seq=lambda i: jnp.where(i < n_long, t_lat, t_ctx), n_long=n_long,
            has_short=n_rows > n_lat and t_ctx < tm, final=final, rows=min(tm, 256), chunk=chunk),
        out_shape=jax.ShapeDtypeStruct((n_rows, d), F32),
        grid=(n_rows // tm, nj),
        in_specs=in_specs,
        out_specs=pl.BlockSpec((tm, d), lambda i, j: (i, 0)),
        scratch_shapes=[pltpu.VMEM((tm + 2 * halo, d), BF16),
                        pltpu.VMEM((tf // chunk, 2, tm + 2 * halo, chunk), F32)],
        compiler_params=_cparams(("parallel", "arbitrary")),
        name="conv_ffn",
    )(*args)


def _rope_tables(t_lat, tm):
    rows = t_lat // GRID_W
    row = jnp.repeat(jnp.arange(rows, dtype=F32), GRID_W)
    col = jnp.tile(jnp.arange(GRID_W, dtype=F32), rows)
    n_freq = HEAD_DIM // 4
    inv = ROPE_THETA ** (-jnp.arange(n_freq, dtype=F32) / n_freq)
    ang = jnp.concatenate([row[:, None] * inv, col[:, None] * inv], axis=-1)
    cos2 = jnp.repeat(jnp.cos(ang), 2, axis=-1)
    sin2 = jnp.repeat(jnp.sin(ang), 2, axis=-1) * jnp.tile(jnp.array([-1.0, 1.0], F32), HEAD_DIM // 2)
    cos2 = jnp.concatenate([cos2, jnp.ones((tm, HEAD_DIM), F32)], axis=0)
    sin2 = jnp.concatenate([sin2, jnp.zeros((tm, HEAD_DIM), F32)], axis=0)
    return cos2, sin2


def _lower_bounds(lb_param):
    p = jax.nn.softmax(lb_param.astype(F32), axis=1)
    return jnp.cumsum(p, axis=1) - p[:, :1]


def kernel(x, c, ctx, c_ctx, w_ada, b_ada, norm1_g, w_in, q_norm_g, k_norm_g, hg_lower_bounds,
           hg_norm_g, sg_norm_g, sg_w, sg_b, w_out, norm2_g, w_up, conv_w, conv_b, w_down,
           final_norm_g):
    batch, t_lat, d = x.shape
    t_ctx = ctx.shape[1]
    depth = w_in.shape[0]
    n_lat, n_ctx = batch * t_lat, batch * t_ctx
    assert t_lat & (t_lat - 1) == 0 and t_ctx & (t_ctx - 1) == 0

    tm = min(1024, n_ctx)
    assert t_lat % tm == 0 and n_ctx % tm == 0

    def tile_mod(i):
        return jnp.minimum(i // (t_lat // tm), batch)

    def tile_rope(i):
        return jnp.where(i < n_lat // tm, i % (t_lat // tm), t_lat // tm)

    xs = jnp.concatenate([x.reshape(n_lat, d), ctx.reshape(n_ctx, d)], axis=0)
    cin = jnp.concatenate([c, c_ctx[None, :], jnp.zeros((8 - batch - 1, d), F32)], axis=0)
    mods = ada_table(cin, w_ada, b_ada)[:, :batch + 1, :]
    mods = mods.reshape(depth, (batch + 1) * N_MOD, 1, d)
    cos2, sin2 = _rope_tables(t_lat, tm)
    lbs = _lower_bounds(hg_lower_bounds)
    qw = ATTN_HEADS * HEAD_DIM
    kvw = ATTN_KV_HEADS * HEAD_DIM
    hg_col0 = (qw + 2 * kvw) // (HG_HEADS * HG_D)
    sg_col0 = hg_col0 + 5

    w_in_l = w_in[0].astype(BF16)

    for l in range(depth):
        mod = mods[l]
        last = l == depth - 1
        n_rows = n_lat if last else n_lat + n_ctx
        p = in_proj(xs, norm1_g[l], mod, w_in_l, tile_mod=tile_mod, tm=tm)
        q, k, vt = qk_prep(p, cos2, sin2, q_norm_g[l], k_norm_g[l], tile_rope=tile_rope, tm=tm)
        attn = attention(q, k, vt, batch=batch, t_lat=t_lat, t_ctx=t_ctx, ctx_queries=not last)
        o_f, w_up_l = hgrn_scan(p, lbs[0, l], [(w_up, l)], batch=batch, t_lat=t_lat, t_ctx=t_ctx,
                                col0=hg_col0, reverse=False)
        casts = [(w_out, l), (w_down, l)] + ([] if last else [(w_in, l + 1)])
        hg, w_out_l, w_down_l, *nxt = hgrn_scan(
            p, lbs[1, l], casts, batch=batch, t_lat=t_lat, t_ctx=t_ctx, col0=hg_col0,
            reverse=True, fwd_out=o_f, gain=hg_norm_g[l])
        w_in_l = nxt[0] if nxt else None
        bias_full = jnp.repeat(sg_b[l].T, SG_DIM, axis=1)
        sg = spatial_gate(p, sg_norm_g[l], sg_w[l].astype(BF16), bias_full, col0=sg_col0,
                          n_rows=n_rows, tm=tm)
        xs = out_proj(attn, hg, sg, w_out_l, xs, mod, tile_mod=tile_mod, n_rows=n_rows, tm=tm)
        xs = conv_ffn(xs, norm2_g[l], mod, w_up_l, conv_w[l], conv_b[l], w_down_l,
                      tile_mod=tile_mod, t_lat=t_lat, t_ctx=t_ctx, n_lat=n_lat, n_rows=n_rows,
                      tm=tm, final_gain=final_norm_g if last else None)

    return xs.reshape(batch, t_lat, d)
```

```python
import functools

import jax
import jax.numpy as jnp
from jax import lax
from jax.experimental import pallas as pl
from jax.experimental.pallas import tpu as pltpu

F32 = jnp.float32
BF16 = jnp.bfloat16

EPS = 1e-6
F_MIN = 1e-30
N_MOD = 6
HEAD_DIM = 128
ATTN_HEADS = 8
ATTN_KV_HEADS = 2
ATTN_GROUP = ATTN_HEADS // ATTN_KV_HEADS
ROPE_THETA = 10000.0
GRID_W = 64
HG_HEADS = 4
HG_D = 128
SG_GROUPS = 4
SG_DIM = 128
SG_CHUNK = 128
CONV_W = 3
LANE = 128
SUBLANES = 8
LOG2_E = 1.4426950408889634
HG_CHUNK = 128

ROW_TILE = 1024
NORM_ROWS = 256
ADA_COLS = 1024
IN_PROJ_COLS = 1280
OUT_PROJ_COLS = 2048
FFN_COLS = 512
ATTN_Q_ROWS = 256
ATTN_KEY_CHUNK = 512
HG_CHUNKS_PER_STEP = 2
BF16_ROWS = 2 * SUBLANES
MXU_WIDTH = 256
V7X_VMEM_BYTES = 64 * 1024 * 1024
VMEM_LIMIT = V7X_VMEM_BYTES - 2 * 1024 * 1024


def _cparams(sem):
    return pltpu.CompilerParams(dimension_semantics=sem, vmem_limit_bytes=VMEM_LIMIT)


def _dot(a, b):
    return jnp.dot(a, b, preferred_element_type=F32)


def _dot_nt(a, b):
    return lax.dot_general(a, b, (((1,), (1,)), ((), ())), preferred_element_type=F32)


def _rms(x, g):
    return x * lax.rsqrt(jnp.mean(x * x, axis=-1, keepdims=True) + EPS) * g


def _silu(x):
    return x * jax.nn.sigmoid(x)


def _ada_kernel(c_ref, w_ref, b_ref, o_ref):
    s = _silu(c_ref[...]).astype(BF16)
    o_ref[...] = _dot(s, w_ref[...].astype(BF16)) + b_ref[...]


def ada_table(cin, w_ada, b_ada, tn=ADA_COLS):
    depth, d, n = w_ada.shape
    assert n % tn == 0
    return pl.pallas_call(
        _ada_kernel,
        out_shape=jax.ShapeDtypeStruct((depth, 8, n), F32),
        grid=(depth, n // tn),
        in_specs=[
            pl.BlockSpec((8, d), lambda l, j: (0, 0)),
            pl.BlockSpec((None, d, tn), lambda l, j: (l, 0, j)),
            pl.BlockSpec((None, 1, tn), lambda l, j: (l, 0, j)),
        ],
        out_specs=pl.BlockSpec((None, 8, tn), lambda l, j: (l, 0, j)),
        compiler_params=_cparams(("parallel", "parallel")),
        name="ada_table",
    )(cin, w_ada, b_ada.reshape(depth, 1, n))


def _norm_mod(x, g, scale, shift):
    r = lax.rsqrt(jnp.mean(x * x, axis=-1, keepdims=True) + EPS)
    return ((x * r) * (g * (1.0 + scale)) + shift).astype(BF16)


def _in_kernel(x_ref, g_ref, sh_ref, sc_ref, w_ref, o_ref, h_ref, *, rows):
    @pl.when(pl.program_id(1) == 0)
    def _():
        for r in range(0, x_ref.shape[0], rows):
            h_ref[r:r + rows, :] = _norm_mod(x_ref[r:r + rows, :], g_ref[...], sc_ref[0], sh_ref[0])

    o_ref[...] = _dot(h_ref[...], w_ref[...])


def in_proj(x, g, mod, w, *, tile_mod, tm, tn=IN_PROJ_COLS):
    n, d = x.shape
    cols = w.shape[1]
    assert n % tm == 0 and cols % tn == 0
    return pl.pallas_call(
        functools.partial(_in_kernel, rows=min(tm, NORM_ROWS)),
        out_shape=jax.ShapeDtypeStruct((n, cols), F32),
        grid=(n // tm, cols // tn),
        in_specs=[
            pl.BlockSpec((tm, d), lambda i, j: (i, 0)),
            pl.BlockSpec((1, d), lambda i, j: (0, 0)),
            pl.BlockSpec((1, 1, d), lambda i, j: (tile_mod(i) * N_MOD + 0, 0, 0)),
            pl.BlockSpec((1, 1, d), lambda i, j: (tile_mod(i) * N_MOD + 1, 0, 0)),
            pl.BlockSpec((d, tn), lambda i, j: (0, j)),
        ],
        out_specs=pl.BlockSpec((tm, tn), lambda i, j: (i, j)),
        scratch_shapes=[pltpu.VMEM((tm, d), BF16)],
        compiler_params=_cparams(("parallel", "arbitrary")),
        name="in_proj",
    )(x, g.reshape(1, d), mod, mod, w)


def _rope(y, cos2, sin2):
    lane = lax.broadcasted_iota(jnp.int32, y.shape, 1)
    swapped = jnp.where((lane & 1) == 0, pltpu.roll(y, LANE - 1, 1), pltpu.roll(y, 1, 1))
    return y * cos2 + swapped * sin2


def _qk_kernel(q_ref, kv_ref, cos_ref, sin_ref, qg_ref, kg_ref, qo_ref, ko_ref, vo_ref):
    cos2, sin2 = cos_ref[...], sin_ref[...]
    scale = HEAD_DIM ** -0.5 * LOG2_E
    for h in range(ATTN_HEADS):
        sl = slice(h * HEAD_DIM, (h + 1) * HEAD_DIM)
        y = _rms(q_ref[:, sl], qg_ref[...])
        qo_ref[:, sl] = (_rope(y, cos2, sin2) * scale).astype(BF16)
    for h in range(ATTN_KV_HEADS):
        sl = slice(h * HEAD_DIM, (h + 1) * HEAD_DIM)
        y = _rms(kv_ref[:, sl], kg_ref[...])
        ko_ref[:, sl] = _rope(y, cos2, sin2).astype(BF16)
    kvw = ATTN_KV_HEADS * HEAD_DIM
    vo_ref[...] = kv_ref[:, kvw:2 * kvw].T.astype(BF16)


def qk_prep(p, cos2, sin2, qg, kg, *, tile_rope, tm):
    n = p.shape[0]
    qw = ATTN_HEADS * HEAD_DIM
    kvw = ATTN_KV_HEADS * HEAD_DIM
    return pl.pallas_call(
        _qk_kernel,
        out_shape=(jax.ShapeDtypeStruct((n, qw), BF16),
                   jax.ShapeDtypeStruct((n, kvw), BF16),
                   jax.ShapeDtypeStruct((kvw, n), BF16)),
        grid=(n // tm,),
        in_specs=[
            pl.BlockSpec((tm, qw), lambda i: (i, 0)),
            pl.BlockSpec((tm, 2 * kvw), lambda i: (i, qw // (2 * kvw))),
            pl.BlockSpec((tm, HEAD_DIM), lambda i: (tile_rope(i), 0)),
            pl.BlockSpec((tm, HEAD_DIM), lambda i: (tile_rope(i), 0)),
            pl.BlockSpec((1, HEAD_DIM), lambda i: (0, 0)),
            pl.BlockSpec((1, HEAD_DIM), lambda i: (0, 0)),
        ],
        out_specs=(pl.BlockSpec((tm, qw), lambda i: (i, 0)),
                   pl.BlockSpec((tm, kvw), lambda i: (i, 0)),
                   pl.BlockSpec((kvw, tm), lambda i: (0, i))),
        compiler_params=_cparams(("parallel",)),
        name="qk_prep",
    )(p, p, cos2, sin2, qg.reshape(1, HEAD_DIM), kg.reshape(1, HEAD_DIM))


def _softmax_pv(q, segs, ck, s_ref):
    mq = q.shape[0]
    chunks = [(k_ref, vt_ref, c0, min(ck, k_ref.shape[0]))
              for k_ref, vt_ref in segs for c0 in range(0, k_ref.shape[0], min(ck, k_ref.shape[0]))]

    def scores(i):
        k_ref, _, c0, cs = chunks[i]
        s_ref[i % 2, 0:cs, :] = _dot_nt(k_ref[c0:c0 + cs, :], q)

    scores(0)
    m = l = acc = None
    for i, (_, vt_ref, c0, cs) in enumerate(chunks):
        if i + 1 < len(chunks):
            scores(i + 1)
        s = s_ref[i % 2, 0:cs, :]
        m_c = s.reshape(cs // SUBLANES, SUBLANES, mq).max(axis=0).max(axis=0, keepdims=True)
        m_new = m_c if m is None else jnp.maximum(m, m_c)
        p = jnp.exp2(s - m_new)
        p_rows = p.reshape(cs // SUBLANES, SUBLANES, mq).sum(axis=0)
        pv = _dot(vt_ref[:, c0:c0 + cs], p.astype(BF16))
        if m is None:
            l, acc = p_rows, pv
        else:
            alpha = jnp.exp2(m - m_new)
            l, acc = alpha * l + p_rows, alpha * acc + pv
        m = m_new
    return (acc / l.sum(axis=0, keepdims=True)).T


def _attn_kernel(q_ref, kl_ref, kc_ref, vl_ref, vc_ref, o_ref, s_ref, *, n_lat, ck):
    qi = pl.program_id(2)
    tq = q_ref.shape[0]

    def run(segs):
        hs = range(ATTN_GROUP)
        q = jnp.concatenate([q_ref[:, h * HEAD_DIM:(h + 1) * HEAD_DIM] for h in hs], axis=0)
        o = _softmax_pv(q, segs, ck, s_ref)
        for h in hs:
            o_ref[:, h * HEAD_DIM:(h + 1) * HEAD_DIM] = o[h * tq:(h + 1) * tq].astype(BF16)

    @pl.when(qi < n_lat)
    def _():
        run([(kl_ref, vl_ref), (kc_ref, vc_ref)])

    @pl.when(qi >= n_lat)
    def _():
        run([(kc_ref, vc_ref)])


def attention(q, k, vt, *, batch, t_lat, t_ctx, ctx_queries=True, tq=ATTN_Q_ROWS,
              ck=ATTN_KEY_CHUNK):
    n = q.shape[0]
    n_lat = t_lat // tq
    n_ctx = t_ctx // tq if ctx_queries else 0
    gw = ATTN_GROUP * HEAD_DIM
    ctx0 = batch * t_lat

    def qmap(b, j, i):
        return (jnp.where(i < n_lat, b * n_lat + i, ctx0 // tq + b * (t_ctx // tq) + (i - n_lat)), j)

    return pl.pallas_call(
        functools.partial(_attn_kernel, n_lat=n_lat, ck=ck),
        out_shape=jax.ShapeDtypeStruct((n if ctx_queries else ctx0, ATTN_HEADS * HEAD_DIM), BF16),
        grid=(batch, ATTN_KV_HEADS, n_lat + n_ctx),
        in_specs=[
            pl.BlockSpec((tq, gw), qmap),
            pl.BlockSpec((t_lat, HEAD_DIM), lambda b, j, i: (b, j)),
            pl.BlockSpec((t_ctx, HEAD_DIM), lambda b, j, i: (ctx0 // t_ctx + b, j)),
            pl.BlockSpec((HEAD_DIM, t_lat), lambda b, j, i: (j, b)),
            pl.BlockSpec((HEAD_DIM, t_ctx), lambda b, j, i: (j, ctx0 // t_ctx + b)),
        ],
        out_specs=pl.BlockSpec((tq, gw), qmap),
        scratch_shapes=[pltpu.VMEM((2, ck, ATTN_GROUP * tq), F32)],
        compiler_params=_cparams(("parallel", "parallel", "arbitrary")),
        name="attention",
    )(q, k, k, vt, vt)


def _hgrn_chunk(hq, z, v, lb, st, *, reverse):
    c = hq.shape[0]
    sig = jax.nn.sigmoid(z)
    f = lb + (1.0 - lb) * sig
    logf = jnp.log(jnp.maximum(f, F_MIN)) * (1.0 / jnp.log(2.0))
    kk = (1.0 - lb) * (1.0 - sig)
    q = _silu(hq)

    row = lax.broadcasted_iota(jnp.int32, (c, c), 0)
    col = lax.broadcasted_iota(jnp.int32, (c, c), 1)
    rid = lax.broadcasted_iota(jnp.int32, (c, HG_D), 0)
    later = (row < col) if reverse else (row > col)

    a = jnp.where(row == col, _dot_nt(q.astype(BF16), kk.astype(BF16)), 0.0)
    cum, tot = logf, logf
    d, ld = 1, 0
    while d < c:
        mask = jnp.logical_and(((row ^ col) >> ld) == 1, later)
        if d < SUBLANES:
            second = (rid & d) != 0
            sel = jnp.logical_not(second) if reverse else second
            x = jnp.where(sel, q, kk) * jnp.exp2(jnp.where(sel, cum, tot - cum))
            t3 = tot.reshape(c // SUBLANES, SUBLANES, HG_D)
            up = pltpu.roll(t3, d, 1).reshape(c, HG_D)
            partner = up if 2 * d == SUBLANES else jnp.where(
                second, up, pltpu.roll(t3, SUBLANES - d, 1).reshape(c, HG_D))
            cum = cum + jnp.where(sel, partner, 0.0)
            tot = tot + partner
        else:
            def split(y):
                y4 = y.reshape(c // (2 * d), 2, d, HG_D)
                return (y4[:, 1], y4[:, 0]) if reverse else (y4[:, 0], y4[:, 1])

            def join(early, late):
                pair = [late, early] if reverse else [early, late]
                return jnp.stack(pair, axis=1).reshape(c, HG_D)

            (cum_e, cum_l), (tot_e, tot_l) = split(cum), split(tot)
            x = join(split(kk)[0] * jnp.exp2(tot_e - cum_e), split(q)[1] * jnp.exp2(cum_l))
            both = tot_e + tot_l
            cum, tot = join(cum_e, cum_l + tot_e), join(both, both)
        xb = x.astype(BF16)
        a = jnp.where(mask, _dot_nt(xb, xb), a)
        d, ld = d * 2, ld + 1

    qd = (q * jnp.exp2(cum)).astype(BF16)
    kd = (kk * jnp.exp2(tot - cum)).astype(BF16)
    vb = v.astype(BF16)
    o = _dot(a.astype(BF16), vb) + _dot_nt(qd, st.astype(BF16))
    st_new = st * jnp.exp2(tot[0:1, :]) + _dot(vb.T, kd)
    return o, st_new


def _hgrn_kernel(*refs, reverse, finalize, n_cast):
    n_in = 7 if finalize else 4
    hq_ref, hf_ref, hi_ref, lb_ref = refs[:4]
    if finalize:
        of_ref, gt_ref, g_ref = refs[4:7]
    o_ref, st_ref = refs[n_in + n_cast], refs[-1]

    @pl.when(pl.program_id(1) == 0)
    def _():
        st_ref[...] = jnp.zeros_like(st_ref)

    for src, dst in zip(refs[n_in:n_in + n_cast], refs[n_in + n_cast + 1:-1]):
        dst[...] = src[...].astype(dst.dtype)

    starts = range(0, hq_ref.shape[0], HG_CHUNK)
    for r in (reversed(starts) if reverse else starts):
        rs = slice(r, r + HG_CHUNK)
        for h in range(HG_HEADS):
            sl = slice(h * HG_D, (h + 1) * HG_D)
            o, st_new = _hgrn_chunk(hq_ref[rs, sl], hf_ref[rs, sl], hi_ref[rs, sl], lb_ref[:, sl],
                                    st_ref[h], reverse=reverse)
            st_ref[h] = st_new
            if finalize:
                y = _rms(o + of_ref[rs, sl], g_ref[...])
                o_ref[rs, sl] = (y * _silu(gt_ref[rs, sl])).astype(o_ref.dtype)
            else:
                o_ref[rs, sl] = o


def _cast_rows(total, steps):
    return next(r for r in range(BF16_ROWS, total + 1, BF16_ROWS)
                if total % r == 0 and total // r <= steps)


def _side_casts(weights, steps, flat_step):
    in_specs, out_specs, shapes = [], [], []
    for w, layer in weights:
        rows = _cast_rows(w.shape[1], steps)
        last = w.shape[1] // rows - 1
        in_specs.append(pl.BlockSpec(
            (None, rows, w.shape[2]),
            lambda *ids, layer=layer, last=last: (layer, jnp.minimum(flat_step(*ids), last), 0)))
        out_specs.append(pl.BlockSpec(
            (rows, w.shape[2]), lambda *ids, last=last: (jnp.minimum(flat_step(*ids), last), 0)))
        shapes.append(jax.ShapeDtypeStruct(w.shape[1:], BF16))
    return in_specs, out_specs, shapes


def hgrn_scan(p, lb, weights, *, batch, t_lat, t_ctx, col0, reverse, fwd_out=None, gain=None,
              chunks_per_step=HG_CHUNKS_PER_STEP):
    n = p.shape[0]
    c = HG_CHUNK * chunks_per_step
    assert t_lat % c == 0 and t_ctx % c == 0
    w = HG_HEADS * HG_D
    n_lat, n_ctx = t_lat // c, t_ctx // c
    ctx0 = batch * n_lat

    def rows(b, s):
        if reverse:
            return jnp.where(s < n_ctx, ctx0 + b * n_ctx + (n_ctx - 1 - s),
                             b * n_lat + (n_lat - 1 - (s - n_ctx)))
        return jnp.where(s < n_ctx, ctx0 + b * n_ctx + s, b * n_lat + (s - n_ctx))

    def spec(cb):
        return pl.BlockSpec((c, w), lambda b, s: (rows(b, s), cb))

    finalize = fwd_out is not None
    in_specs = [spec(col0), spec(col0 + (2 if reverse else 1)), spec(col0 + 3),
                pl.BlockSpec((1, w), lambda b, s: (0, 0))]
    args = [p, p, p, lb.reshape(1, w)]
    if finalize:
        in_specs += [spec(0), spec(col0 + 4), pl.BlockSpec((1, HG_D), lambda b, s: (0, 0))]
        args += [fwd_out, p, gain.reshape(1, HG_D)]
    n_steps = n_lat + n_ctx
    c_in, c_out, c_shapes = _side_casts(weights, batch * n_steps, lambda b, s: b * n_steps + s)
    return pl.pallas_call(
        functools.partial(_hgrn_kernel, reverse=reverse, finalize=finalize, n_cast=len(weights)),
        out_shape=(jax.ShapeDtypeStruct((n, w), BF16 if finalize else F32), *c_shapes),
        grid=(batch, n_steps),
        in_specs=in_specs + c_in,
        out_specs=(spec(0), *c_out),
        scratch_shapes=[pltpu.VMEM((HG_HEADS, HG_D, HG_D), F32)],
        compiler_params=_cparams(("arbitrary", "arbitrary")),
        name="hgrn_bwd" if reverse else "hgrn_fwd",
    )(*args, *[w for w, _ in weights])


def _sg_kernel(u_ref, v_ref, g_ref, w_ref, b_ref, o_ref):
    for r in range(0, u_ref.shape[0], SG_CHUNK):
        rs = slice(r, r + SG_CHUNK)
        for g in range(SG_GROUPS):
            sl = slice(g * SG_DIM, (g + 1) * SG_DIM)
            vn = _rms(jax.nn.gelu(v_ref[rs, sl]), g_ref[:, sl])
            mixed = _dot(w_ref[g], vn.astype(BF16)) + b_ref[:, sl]
            o_ref[rs, sl] = (jax.nn.gelu(u_ref[rs, sl]) * mixed).astype(BF16)


def spatial_gate(p, g, w, bias_full, *, col0, n_rows, tm):
    sw = SG_GROUPS * SG_DIM
    assert n_rows % tm == 0
    return pl.pallas_call(
        _sg_kernel,
        out_shape=jax.ShapeDtypeStruct((n_rows, sw), BF16),
        grid=(n_rows // tm,),
        in_specs=[
            pl.BlockSpec((tm, sw), lambda i: (i, col0)),
            pl.BlockSpec((tm, sw), lambda i: (i, col0 + 1)),
            pl.BlockSpec((1, sw), lambda i: (0, 0)),
            pl.BlockSpec((SG_GROUPS, SG_CHUNK, SG_CHUNK), lambda i: (0, 0, 0)),
            pl.BlockSpec((SG_CHUNK, sw), lambda i: (0, 0)),
        ],
        out_specs=pl.BlockSpec((tm, sw), lambda i: (i, 0)),
        compiler_params=_cparams(("parallel",)),
        name="spatial_gate",
    )(p, p, g.reshape(1, sw), w, bias_full)


def _out_kernel(a_ref, h_ref, s_ref, wa_ref, wh_ref, ws_ref, x_ref, gate_ref, o_ref):
    acc = _dot(a_ref[...], wa_ref[...]) + _dot(h_ref[...], wh_ref[...]) + _dot(s_ref[...], ws_ref[...])
    o_ref[...] = x_ref[...] + gate_ref[0] * acc


def out_proj(attn, hg, sg, w, x, mod, *, tile_mod, n_rows, tm, tn=OUT_PROJ_COLS):
    n, d = n_rows, x.shape[1]
    tn = min(tn, d)
    wa, wh, ws = attn.shape[1], hg.shape[1], sg.shape[1]
    assert wa % wh == 0 and wh == ws and n % tm == 0 and d % tn == 0
    return pl.pallas_call(
        _out_kernel,
        out_shape=jax.ShapeDtypeStruct((n, d), F32),
        grid=(n // tm, d // tn),
        in_specs=[
            pl.BlockSpec((tm, wa), lambda i, j: (i, 0)),
            pl.BlockSpec((tm, wh), lambda i, j: (i, 0)),
            pl.BlockSpec((tm, ws), lambda i, j: (i, 0)),
            pl.BlockSpec((wa, tn), lambda i, j: (0, j)),
            pl.BlockSpec((wh, tn), lambda i, j: (wa // wh, j)),
            pl.BlockSpec((ws, tn), lambda i, j: (wa // wh + 1, j)),
            pl.BlockSpec((tm, tn), lambda i, j: (i, j)),
            pl.BlockSpec((1, 1, tn), lambda i, j: (tile_mod(i) * N_MOD + 2, 0, j)),
        ],
        out_specs=pl.BlockSpec((tm, tn), lambda i, j: (i, j)),
        compiler_params=_cparams(("parallel", "parallel")),
        name="out_proj",
    )(attn, hg, sg, w, w, w, x, mod)


def _ffn_kernel(*refs, tile_seq, n_long, has_short, final, rows, chunk):
    (x_ref, xp_ref, xn_ref, g_ref, sh_ref, sc_ref, gate_ref,
     wg_ref, wv_ref, cwg_ref, cwv_ref, cbg_ref, cbv_ref, wd_ref) = refs[:14]
    gf_ref = refs[14] if final else None
    o_ref, h_ref, u_ref = refs[-3:]
    i, j = pl.program_id(0), pl.program_id(1)
    tm = x_ref.shape[0]
    halo = xp_ref.shape[0]

    def norm_mod(x):
        return _norm_mod(x, g_ref[...], sc_ref[0], sh_ref[0])

    @pl.when(j == 0)
    def _():
        for r in range(0, tm, rows):
            h_ref[r:r + rows, :] = norm_mod(x_ref[r:r + rows, :])
        h_ref[tm:tm + halo, :] = norm_mod(xp_ref[...])
        h_ref[tm + halo:tm + 2 * halo, :] = norm_mod(xn_ref[...])
        o_ref[...] = jnp.zeros_like(o_ref)

    chunks = [slice(c0, c0 + chunk) for c0 in range(0, wd_ref.shape[0], chunk)]
    rid = lax.broadcasted_iota(jnp.int32, (tm, 1), 0)

    def step(interior):
        seq = tile_seq(i)
        tile_starts = ((i * tm) & (seq - 1)) == 0
        tile_ends = (((i + 1) * tm) & (seq - 1)) == 0
        if interior:
            pos = (i * tm + rid) & (seq - 1)
            first, last = pos == 0, pos == seq - 1

        for c, cs in enumerate(chunks):
            u_ref[c, 0] = _dot(h_ref[...], wg_ref[:, cs])
            u_ref[c, 1] = _dot(h_ref[...], wv_ref[:, cs])

        def conv(c, k, cw_ref, cb_ref, cs):
            u = u_ref[c, k, 0:tm, :]
            prev_row = jnp.where(tile_starts, 0.0, u_ref[c, k, tm + halo - 1:tm + halo, :])
            next_row = jnp.where(tile_ends, 0.0, u_ref[c, k, tm + halo:tm + halo + 1, :])
            before = jnp.where(rid == 0, prev_row, pltpu.roll(u, 1, 0))
            after = jnp.where(rid == tm - 1, next_row, pltpu.roll(u, tm - 1, 0))
            if interior:
                before = jnp.where(first, 0.0, before)
                after = jnp.where(last, 0.0, after)
            return (cb_ref[:, cs] + before * cw_ref[0:1, cs] + u * cw_ref[1:2, cs]
                    + after * cw_ref[2:3, cs])

        for c, cs in enumerate(chunks):
            act = _silu(conv(c, 0, cwg_ref, cbg_ref, cs)) * conv(c, 1, cwv_ref, cbv_ref, cs)
            o_ref[...] += _dot(act.astype(BF16), wd_ref[cs, :])

    if has_short:
        pl.when(i < n_long)(functools.partial(step, False))
        pl.when(i >= n_long)(functools.partial(step, True))
    else:
        step(False)

    @pl.when(j == pl.num_programs(1) - 1)
    def _():
        def finish(r, carry):
            rs = pl.ds(pl.multiple_of(r * rows, rows), rows)
            y = x_ref[rs, :] + gate_ref[0] * o_ref[rs, :]
            o_ref[rs, :] = _rms(y, gf_ref[...]) if final else y
            return carry

        lax.fori_loop(0, tm // rows, finish, 0)


def conv_ffn(x, g, mod, w_up, conv_w, conv_b, w_down, *, tile_mod, t_lat, t_ctx, n_lat,
             n_rows, tm, tf=FFN_COLS, final_gain=None):
    n, d = x.shape
    dff = w_down.shape[0]
    assert n_rows % tm == 0 and n_lat % tm == 0 and t_lat % tm == 0 and dff % tf == 0
    assert n_rows == n_lat or tm % t_ctx == 0 or t_ctx % tm == 0
    nj = dff // tf
    halo = SUBLANES
    chunk = min(MXU_WIDTH, tf)
    hb = tm // halo
    nhb = n // halo
    n_long = n_lat // tm
    cb = conv_b.reshape(1, 2 * dff)
    final = final_gain is not None
    in_specs = [
        pl.BlockSpec((tm, d), lambda i, j: (i, 0)),
        pl.BlockSpec((halo, d), lambda i, j: (jnp.maximum(i * hb - 1, 0), 0)),
        pl.BlockSpec((halo, d), lambda i, j: (jnp.minimum((i + 1) * hb, nhb - 1), 0)),
        pl.BlockSpec((1, d), lambda i, j: (0, 0)),
        pl.BlockSpec((1, 1, d), lambda i, j: (tile_mod(i) * N_MOD + 3, 0, 0)),
        pl.BlockSpec((1, 1, d), lambda i, j: (tile_mod(i) * N_MOD + 4, 0, 0)),
        pl.BlockSpec((1, 1, d), lambda i, j: (tile_mod(i) * N_MOD + 5, 0, 0)),
        pl.BlockSpec((d, tf), lambda i, j: (0, j)),
        pl.BlockSpec((d, tf), lambda i, j: (0, nj + j)),
        pl.BlockSpec((CONV_W, tf), lambda i, j: (0, j)),
        pl.BlockSpec((CONV_W, tf), lambda i, j: (0, nj + j)),
        pl.BlockSpec((1, tf), lambda i, j: (0, j)),
        pl.BlockSpec((1, tf), lambda i, j: (0, nj + j)),
        pl.BlockSpec((tf, d), lambda i, j: (j, 0)),
    ]
    args = [x, x, x, g.reshape(1, d), mod, mod, mod, w_up, w_up, conv_w, conv_w, cb, cb, w_down]
    if final:
        in_specs.append(pl.BlockSpec((1, d), lambda i, j: (0, 0)))
        args.append(final_gain.reshape(1, d))
    return pl.pallas_call(
        functools.partial(
            _ffn_kernel, tile_seq=lambda i: jnp.where(i < n_long, t_lat, t_ctx), n_long=n_long,
            has_short=n_rows > n_lat and t_ctx < tm, final=final, rows=min(tm, NORM_ROWS),
            chunk=chunk),
        out_shape=jax.ShapeDtypeStruct((n_rows, d), F32),
        grid=(n_rows // tm, nj),
        in_specs=in_specs,
        out_specs=pl.BlockSpec((tm, d), lambda i, j: (i, 0)),
        scratch_shapes=[pltpu.VMEM((tm + 2 * halo, d), BF16),
                        pltpu.VMEM((tf // chunk, 2, tm + 2 * halo, chunk), F32)],
        compiler_params=_cparams(("parallel", "arbitrary")),
        name="conv_ffn",
    )(*args)


def _rope_tables(t_lat, tm):
    rows = t_lat // GRID_W
    row = jnp.repeat(jnp.arange(rows, dtype=F32), GRID_W)
    col = jnp.tile(jnp.arange(GRID_W, dtype=F32), rows)
    n_freq = HEAD_DIM // 4
    inv = ROPE_THETA ** (-jnp.arange(n_freq, dtype=F32) / n_freq)
    ang = jnp.concatenate([row[:, None] * inv, col[:, None] * inv], axis=-1)
    cos2 = jnp.repeat(jnp.cos(ang), 2, axis=-1)
    sin2 = jnp.repeat(jnp.sin(ang), 2, axis=-1) * jnp.tile(jnp.array([-1.0, 1.0], F32), HEAD_DIM // 2)
    cos2 = jnp.concatenate([cos2, jnp.ones((tm, HEAD_DIM), F32)], axis=0)
    sin2 = jnp.concatenate([sin2, jnp.zeros((tm, HEAD_DIM), F32)], axis=0)
    return cos2, sin2


def _lower_bounds(lb_param):
    p = jax.nn.softmax(lb_param.astype(F32), axis=1)
    return jnp.cumsum(p, axis=1) - p[:, :1]


def kernel(x, c, ctx, c_ctx, w_ada, b_ada, norm1_g, w_in, q_norm_g, k_norm_g, hg_lower_bounds,
           hg_norm_g, sg_norm_g, sg_w, sg_b, w_out, norm2_g, w_up, conv_w, conv_b, w_down,
           final_norm_g):
    batch, t_lat, d = x.shape
    t_ctx = ctx.shape[1]
    depth = w_in.shape[0]
    n_lat, n_ctx = batch * t_lat, batch * t_ctx
    assert t_lat & (t_lat - 1) == 0 and t_ctx & (t_ctx - 1) == 0

    tm = min(ROW_TILE, n_ctx)
    assert t_lat % tm == 0 and n_ctx % tm == 0

    def tile_mod(i):
        return jnp.minimum(i // (t_lat // tm), batch)

    def tile_rope(i):
        return jnp.where(i < n_lat // tm, i % (t_lat // tm), t_lat // tm)

    xs = jnp.concatenate([x.reshape(n_lat, d), ctx.reshape(n_ctx, d)], axis=0)
    cin = jnp.concatenate([c, c_ctx[None, :], jnp.zeros((8 - batch - 1, d), F32)], axis=0)
    mods = ada_table(cin, w_ada, b_ada)[:, :batch + 1, :]
    mods = mods.reshape(depth, (batch + 1) * N_MOD, 1, d)
    cos2, sin2 = _rope_tables(t_lat, tm)
    lbs = _lower_bounds(hg_lower_bounds)
    qw = ATTN_HEADS * HEAD_DIM
    kvw = ATTN_KV_HEADS * HEAD_DIM
    hg_col0 = (qw + 2 * kvw) // (HG_HEADS * HG_D)
    sg_col0 = hg_col0 + 5

    w_in_l = w_in[0].astype(BF16)

    for l in range(depth):
        mod = mods[l]
        last = l == depth - 1
        n_rows = n_lat if last else n_lat + n_ctx
        p = in_proj(xs, norm1_g[l], mod, w_in_l, tile_mod=tile_mod, tm=tm)
        q, k, vt = qk_prep(p, cos2, sin2, q_norm_g[l], k_norm_g[l], tile_rope=tile_rope, tm=tm)
        attn = attention(q, k, vt, batch=batch, t_lat=t_lat, t_ctx=t_ctx, ctx_queries=not last)
        o_f, w_up_l = hgrn_scan(p, lbs[0, l], [(w_up, l)], batch=batch, t_lat=t_lat, t_ctx=t_ctx,
                                col0=hg_col0, reverse=False)
        casts = [(w_out, l), (w_down, l)] + ([] if last else [(w_in, l + 1)])
        hg, w_out_l, w_down_l, *nxt = hgrn_scan(
            p, lbs[1, l], casts, batch=batch, t_lat=t_lat, t_ctx=t_ctx, col0=hg_col0,
            reverse=True, fwd_out=o_f, gain=hg_norm_g[l])
        w_in_l = nxt[0] if nxt else None
        bias_full = jnp.repeat(sg_b[l].T, SG_DIM, axis=1)
        sg = spatial_gate(p, sg_norm_g[l], sg_w[l].astype(BF16), bias_full, col0=sg_col0,
                          n_rows=n_rows, tm=tm)
        xs = out_proj(attn, hg, sg, w_out_l, xs, mod, tile_mod=tile_mod, n_rows=n_rows, tm=tm)
        xs = conv_ffn(xs, norm2_g[l], mod, w_up_l, conv_w[l], conv_b[l], w_down_l,
                      tile_mod=tile_mod, t_lat=t_lat, t_ctx=t_ctx, n_lat=n_lat, n_rows=n_rows,
                      tm=tm, final_gain=final_norm_g if last else None)

    return xs.reshape(batch, t_lat, d)
```
